```python
import jax, jax.numpy as jnp
from jax import lax
import numpy as np

D_MODEL = 2048
BATCH = 4
SEQ = 8192
DEPTH = 4

GRID_W = 64
CTX_LEN = 256
HEAD_DIM = 128
ROPE_THETA = 10000.0

RET_HEADS = 4
RET_DIM = RET_HEADS * HEAD_DIM
RET_CHUNK = 128
SWA_HEADS = 6
SWA_KV_HEADS = 2
SWA_GROUP = SWA_HEADS // SWA_KV_HEADS
SWA_WINDOW = 128
SWA_BLOCK = 128
MLA_HEADS = 6
MLA_Q_RANK = 512
MLA_KV_RANK = 256
MLA_NOPE_DIM = 128
MLA_ROPE_DIM = 64
MLA_V_DIM = 128
MLA_Q_BLOCK = 128
MLA_SCALE = (MLA_NOPE_DIM + MLA_ROPE_DIM) ** -0.5

MIX_WIDTH = RET_DIM + SWA_HEADS * HEAD_DIM + MLA_HEADS * MLA_V_DIM
IN_SPLITS = (RET_DIM, RET_DIM, RET_DIM, RET_DIM,
             SWA_HEADS * HEAD_DIM, SWA_KV_HEADS * HEAD_DIM, SWA_KV_HEADS * HEAD_DIM,
             MLA_Q_RANK, MLA_KV_RANK, MLA_ROPE_DIM)
IN_WIDTH = 4 * RET_DIM + (SWA_HEADS + 2 * SWA_KV_HEADS) * HEAD_DIM + MLA_Q_RANK + MLA_KV_RANK + MLA_ROPE_DIM

D_FF = 5632
CONV_W = 3
N_MOD = 6
LN_EPS = 1e-5
RMS_EPS = 1e-6
NEG_INF = -1e30
DEEPNORM_ALPHA = (2 * DEPTH) ** 0.25
DEEPNORM_BETA = (8 * DEPTH) ** -0.25

kernel_name = "hybrid_ret_swa_mla_dit_trunk"


def layer_norm(x, g, b):
    xf = x.astype(jnp.float32)
    mu = jnp.mean(xf, -1, keepdims=True)
    var = jnp.mean(jnp.square(xf - mu), -1, keepdims=True)
    return ((xf - mu) * lax.rsqrt(var + LN_EPS)).astype(x.dtype) * g + b


def rms_norm(x, g=None):
    xf = x.astype(jnp.float32)
    y = (xf * lax.rsqrt(jnp.mean(jnp.square(xf), -1, keepdims=True) + RMS_EPS)).astype(x.dtype)
    return y if g is None else y * g


def heads(t, n):
    return t.reshape(t.shape[:-1] + (n, t.shape[-1] // n))


def axial_rope_tables(n_tokens, dim):
    rows = n_tokens // GRID_W
    r, cc = jnp.meshgrid(jnp.arange(rows, dtype=jnp.float32), jnp.arange(GRID_W, dtype=jnp.float32), indexing="ij")
    r = r.reshape(-1)
    cc = cc.reshape(-1)
    n_freq = dim // 4
    inv = ROPE_THETA ** (-jnp.arange(n_freq, dtype=jnp.float32) / n_freq)
    ang_r = r[:, None] * inv
    ang_c = cc[:, None] * inv
    ang = jnp.concatenate([ang_r, ang_r, ang_c, ang_c], axis=-1)
    return jnp.cos(ang), jnp.sin(ang)


def apply_rope(x, cos, sin):
    nq = x.shape[-1] // 4
    xr = x.reshape(x.shape[:-1] + (2, 2, nq))
    rot = jnp.stack([-xr[..., 1, :], xr[..., 0, :]], axis=-2).reshape(x.shape)
    return x * cos[:, None, :].astype(x.dtype) + rot * sin[:, None, :].astype(x.dtype)


def retention_scan(q, k, v, log_gamma, state0):
    B, L, H, dk = q.shape
    dv = v.shape[-1]
    C = RET_CHUNK
    N = L // C
    dt = q.dtype
    qc = q.reshape(B, N, C, H, dk)
    kc = k.reshape(B, N, C, H, dk)
    vc = v.reshape(B, N, C, H, dv)
    pos = jnp.arange(C, dtype=jnp.float32)
    diff = pos[:, None] - pos[None, :]
    intra = jnp.where(diff >= 0, jnp.exp(log_gamma[:, None, None] * jnp.maximum(diff, 0.0)), 0.0).astype(dt)
    q_decay = jnp.exp(log_gamma[:, None] * (pos + 1.0)).astype(dt)
    k_decay = jnp.exp(log_gamma[:, None] * (C - 1.0 - pos)).astype(dt)
    chunk_decay = jnp.exp(log_gamma * C).astype(dt)[None, :, None, None]
    scores = jnp.einsum("bnihd,bnjhd->bnhij", qc, kc) * intra
    inner = jnp.einsum("bnhij,bnjhe->bnihe", scores, vc)
    kv = jnp.einsum("bnjhd,bnjhe,hj->nbhde", kc, vc, k_decay)

    def step(s, kv_n):
        return chunk_decay * s + kv_n, s

    s_final, s_in = lax.scan(step, state0, kv)
    cross = jnp.einsum("bnihd,nbhde,hi->bnihe", qc, s_in, q_decay)
    return (inner + cross).reshape(B, L, H, dv), s_final


def retention_bidir(q, k, v, lg_f, lg_b, s0_f, s0_b):
    out_f, s_f = retention_scan(q, k, v, lg_f, s0_f)
    out_b, s_b = retention_scan(q[:, ::-1], k[:, ::-1], v[:, ::-1], lg_b, s0_b)
    return out_f + out_b[:, ::-1], s_f, s_b


def softmax_with_sink(s, sink_kg):
    col = jnp.broadcast_to(sink_kg[:, :, None, None], s.shape[:-1] + (1,))
    p = jax.nn.softmax(jnp.concatenate([s, col], axis=-1), axis=-1)
    return p[..., :-1]


def swa_latent(q, k, v, k_ctx, v_ctx, sink_kg):
    B, L, H, d = q.shape
    Cb = SWA_BLOCK
    N = L // Cb
    qb = q.reshape(B, N, Cb, SWA_KV_HEADS, SWA_GROUP, d)

    def band(t):
        tp = jnp.pad(t, ((0, 0), (Cb, Cb), (0, 0), (0, 0))).reshape(B, N + 2, Cb, SWA_KV_HEADS, d)
        return jnp.concatenate([tp[:, :-2], tp[:, 1:-1], tp[:, 2:]], axis=2)

    kw, vw = band(k), band(v)
    scale = d ** -0.5
    s_loc = jnp.einsum("bnqkgd,bnskd->bnkgqs", qb, kw).astype(jnp.float32) * scale
    n_i = jnp.arange(N)[:, None, None]
    q_i = jnp.arange(Cb)[None, :, None]
    s_j = jnp.arange(3 * Cb)[None, None, :]
    key_pos = n_i * Cb + s_j - Cb
    valid = (jnp.abs(key_pos - (n_i * Cb + q_i)) <= SWA_WINDOW) & (key_pos >= 0) & (key_pos < L)
    s_loc = jnp.where(valid[None, :, None, None], s_loc, NEG_INF)
    s_ctx = jnp.einsum("bnqkgd,btkd->bnkgqt", qb, k_ctx).astype(jnp.float32) * scale
    p = softmax_with_sink(jnp.concatenate([s_loc, s_ctx], axis=-1), sink_kg).astype(v.dtype)
    out = (jnp.einsum("bnkgqs,bnskd->bnqkgd", p[..., :3 * Cb], vw)
           + jnp.einsum("bnkgqt,btkd->bnqkgd", p[..., 3 * Cb:], v_ctx))
    return out.reshape(B, L, H * d)


def swa_context(q, k, v, sink_kg):
    B, T, H, d = q.shape
    qg = q.reshape(B, T, SWA_KV_HEADS, SWA_GROUP, d)
    s = jnp.einsum("btkgd,bskd->bkgts", qg, k).astype(jnp.float32) * d ** -0.5
    p = softmax_with_sink(s, sink_kg).astype(v.dtype)
    return jnp.einsum("bkgts,bskd->btkgd", p, v).reshape(B, T, H * d)


def mla_project(cq, ckv, kr, q_norm, w_uq, kv_norm, w_ukv, cos, sin):
    B, L, _ = cq.shape
    q = (rms_norm(cq, q_norm) @ w_uq).reshape(B, L, MLA_HEADS, MLA_NOPE_DIM + MLA_ROPE_DIM)
    q_nope, q_rope = q[..., :MLA_NOPE_DIM], q[..., MLA_NOPE_DIM:]
    kv = (rms_norm(ckv, kv_norm) @ w_ukv).reshape(B, L, MLA_HEADS, MLA_NOPE_DIM + MLA_V_DIM)
    k_nope, v = kv[..., :MLA_NOPE_DIM], kv[..., MLA_NOPE_DIM:]
    k_rope = kr[:, :, None, :]
    if cos is not None:
        q_rope = apply_rope(q_rope, cos, sin)
        k_rope = apply_rope(k_rope, cos, sin)
    return q_nope, q_rope, k_nope, k_rope[:, :, 0], v


def mla_attend(q_nope, q_rope, k_nope, k_rope, v):
    s = (jnp.einsum("bqhd,bshd->bhqs", q_nope, k_nope)
         + jnp.einsum("bqhr,bsr->bhqs", q_rope, k_rope)).astype(jnp.float32) * MLA_SCALE
    p = jax.nn.softmax(s, axis=-1).astype(v.dtype)
    return jnp.einsum("bhqs,bshd->bqhd", p, v)


def mla_latent(qn, qr, kn, kr, v):
    B, L, H, _ = qn.shape
    N = L // MLA_Q_BLOCK

    def blocks(t):
        return jnp.moveaxis(t.reshape((B, N, MLA_Q_BLOCK) + t.shape[2:]), 1, 0)

    out = lax.map(lambda qs: mla_attend(qs[0], qs[1], kn, kr, v), (blocks(qn), blocks(qr)))
    return jnp.moveaxis(out, 0, 1).reshape(B, L, H * MLA_V_DIM)


def conv_ffn(h, w_up, conv_w, conv_b, w_down):
    L = h.shape[1]
    u, g = jnp.split(h @ w_up, 2, axis=-1)
    pad = CONV_W // 2
    gp = jnp.pad(g, ((0, 0), (pad, pad), (0, 0)))
    g = sum(gp[:, j:j + L] * conv_w[j] for j in range(CONV_W)) + conv_b
    return (jax.nn.silu(g) * u) @ w_down


def setup_inputs(seed: int = 0) -> dict:
    key = jax.random.key(seed)
    ks = jax.random.split(key, 24)
    f32 = jnp.float32

    def nrm(k, shape, scale):
        return jax.random.normal(k, shape, f32) * scale

    gam = 1.0 - 2.0 ** (-5.0 - np.arange(RET_HEADS))
    logit = jnp.asarray(np.log(gam / (1.0 - gam)), f32)
    return {
        "x": nrm(ks[0], (BATCH, SEQ, D_MODEL), 1.0),
        "c": nrm(ks[1], (BATCH, D_MODEL), 1.0),
        "ctx": nrm(ks[2], (BATCH, CTX_LEN, D_MODEL), 1.0),
        "c_ctx": nrm(ks[3], (D_MODEL,), 1.0),
        "ada_w": nrm(ks[4], (DEPTH, D_MODEL, N_MOD * D_MODEL), 0.5 * D_MODEL ** -0.5),
        "ada_b": nrm(ks[5], (DEPTH, N_MOD * D_MODEL), 0.02),
        "w_in": nrm(ks[6], (DEPTH, D_MODEL, IN_WIDTH), D_MODEL ** -0.5),
        "ret_decay_fwd": logit + nrm(ks[7], (DEPTH, RET_HEADS), 0.1),
        "ret_decay_bwd": logit + nrm(ks[8], (DEPTH, RET_HEADS), 0.1),
        "swa_sink": nrm(ks[9], (DEPTH, SWA_HEADS), 0.5),
        "mla_q_norm": 1.0 + nrm(ks[10], (DEPTH, MLA_Q_RANK), 0.02),
        "mla_w_uq": nrm(ks[11], (DEPTH, MLA_Q_RANK, MLA_HEADS * (MLA_NOPE_DIM + MLA_ROPE_DIM)), MLA_Q_RANK ** -0.5),
        "mla_kv_norm": 1.0 + nrm(ks[12], (DEPTH, MLA_KV_RANK), 0.02),
        "mla_w_ukv": nrm(ks[13], (DEPTH, MLA_KV_RANK, MLA_HEADS * (MLA_NOPE_DIM + MLA_V_DIM)), MLA_KV_RANK ** -0.5),
        "w_o": nrm(ks[14], (DEPTH, MIX_WIDTH, D_MODEL), DEEPNORM_BETA * MIX_WIDTH ** -0.5),
        "ln1_g": 1.0 + nrm(ks[15], (DEPTH, D_MODEL), 0.02),
        "ln1_b": nrm(ks[16], (DEPTH, D_MODEL), 0.02),
        "ffn_w_up": nrm(ks[17], (DEPTH, D_MODEL, 2 * D_FF), D_MODEL ** -0.5),
        "ffn_conv_w": nrm(ks[18], (DEPTH, CONV_W, D_FF), CONV_W ** -0.5),
        "ffn_conv_b": nrm(ks[19], (DEPTH, D_FF), 0.02),
        "ffn_w_down": nrm(ks[20], (DEPTH, D_FF, D_MODEL), DEEPNORM_BETA * D_FF ** -0.5),
        "ln2_g": 1.0 + nrm(ks[21], (DEPTH, D_MODEL), 0.02),
        "ln2_b": nrm(ks[22], (DEPTH, D_MODEL), 0.02),
    }


def reference(x, c, ctx, c_ctx, ada_w, ada_b, w_in, ret_decay_fwd, ret_decay_bwd, swa_sink,
              mla_q_norm, mla_w_uq, mla_kv_norm, mla_w_ukv, w_o, ln1_g, ln1_b,
              ffn_w_up, ffn_conv_w, ffn_conv_b, ffn_w_down, ln2_g, ln2_b):
    B, L, _ = x.shape
    T = ctx.shape[1]
    cos_h, sin_h = axial_rope_tables(L, HEAD_DIM)
    cos_m, sin_m = axial_rope_tables(L, MLA_ROPE_DIM)
    split_idx = np.cumsum(IN_SPLITS)[:-1].tolist()
    k_scale = HEAD_DIM ** -0.5
    sc = jax.nn.silu(c)
    scc = jax.nn.silu(c_ctx)
    xc = ctx
    for l in range(DEPTH):
        last = l == DEPTH - 1
        mx = [m[:, None, :] for m in jnp.split(sc @ ada_w[l] + ada_b[l], N_MOD, axis=-1)]
        mc = jnp.split(scc @ ada_w[l] + ada_b[l], N_MOD, axis=-1)
        hx = x * (1.0 + mx[1]) + mx[0]
        hc = xc * (1.0 + mc[1]) + mc[0]
        (rq, rk, rv, rg, sq, sk, sv, mcq, mckv, mkr) = jnp.split(hx @ w_in[l], split_idx, axis=-1)
        (rq_c, rk_c, rv_c, rg_c, sq_c, sk_c, sv_c, mcq_c, mckv_c, mkr_c) = jnp.split(hc @ w_in[l], split_idx, axis=-1)

        lg_f = jax.nn.log_sigmoid(ret_decay_fwd[l].astype(jnp.float32))
        lg_b = jax.nn.log_sigmoid(ret_decay_bwd[l].astype(jnp.float32))
        s0 = jnp.zeros((B, RET_HEADS, HEAD_DIM, HEAD_DIM), x.dtype)
        ret_c, st_f, st_b = retention_bidir(heads(rq_c, RET_HEADS), heads(rk_c, RET_HEADS) * k_scale,
                                            heads(rv_c, RET_HEADS), lg_f, lg_b, s0, s0)
        ret_x, _, _ = retention_bidir(apply_rope(heads(rq, RET_HEADS), cos_h, sin_h),
                                      apply_rope(heads(rk, RET_HEADS), cos_h, sin_h) * k_scale,
                                      heads(rv, RET_HEADS), lg_f, lg_b, st_f, st_b)
        y_ret = jax.nn.silu(rg) * rms_norm(ret_x).reshape(B, L, RET_DIM)

        sink_kg = swa_sink[l].astype(jnp.float32).reshape(SWA_KV_HEADS, SWA_GROUP)
        k_sc = heads(sk_c, SWA_KV_HEADS)
        v_sc = heads(sv_c, SWA_KV_HEADS)
        y_swa = swa_latent(apply_rope(heads(sq, SWA_HEADS), cos_h, sin_h),
                           apply_rope(heads(sk, SWA_KV_HEADS), cos_h, sin_h),
                           heads(sv, SWA_KV_HEADS), k_sc, v_sc, sink_kg)

        qn_x, qr_x, kn_x, kr_x, v_x = mla_project(mcq, mckv, mkr, mla_q_norm[l], mla_w_uq[l],
                                                  mla_kv_norm[l], mla_w_ukv[l], cos_m, sin_m)
        qn_c, qr_c, kn_c, kr_c, v_mc = mla_project(mcq_c, mckv_c, mkr_c, mla_q_norm[l], mla_w_uq[l],
                                                   mla_kv_norm[l], mla_w_ukv[l], None, None)
        y_mla = mla_latent(qn_x, qr_x, jnp.concatenate([kn_x, kn_c], axis=1),
                           jnp.concatenate([kr_x, kr_c], axis=1), jnp.concatenate([v_x, v_mc], axis=1))

        mix_x = jnp.concatenate([y_ret, y_swa, y_mla], axis=-1) @ w_o[l]
        x_a = layer_norm(DEEPNORM_ALPHA * x + (1.0 + mx[2]) * mix_x, ln1_g[l], ln1_b[l])
        f_x = conv_ffn(x_a * (1.0 + mx[4]) + mx[3], ffn_w_up[l], ffn_conv_w[l], ffn_conv_b[l], ffn_w_down[l])
        x_new = layer_norm(DEEPNORM_ALPHA * x_a + (1.0 + mx[5]) * f_x, ln2_g[l], ln2_b[l])

        if not last:
            y_ret_c = jax.nn.silu(rg_c) * rms_norm(ret_c).reshape(B, T, RET_DIM)
            y_swa_c = swa_context(heads(sq_c, SWA_HEADS), k_sc, v_sc, sink_kg)
            y_mla_c = mla_attend(qn_c, qr_c, kn_c, kr_c, v_mc).reshape(B, T, MLA_HEADS * MLA_V_DIM)
            mix_c = jnp.concatenate([y_ret_c, y_swa_c, y_mla_c], axis=-1) @ w_o[l]
            xc_a = layer_norm(DEEPNORM_ALPHA * xc + (1.0 + mc[2]) * mix_c, ln1_g[l], ln1_b[l])
            f_c = conv_ffn(xc_a * (1.0 + mc[4]) + mc[3], ffn_w_up[l], ffn_conv_w[l], ffn_conv_b[l], ffn_w_down[l])
            xc = layer_norm(DEEPNORM_ALPHA * xc_a + (1.0 + mc[5]) * f_c, ln2_g[l], ln2_b[l])
        x = x_new
    return x
```

```python
import functools

import jax
import jax.numpy as jnp
import numpy as np
from jax import lax
from jax.experimental import pallas as pl
from jax.experimental.pallas import tpu as pltpu

GRID_W = 64
HEAD_DIM = 128
ROPE_THETA = 10000.0
RET_HEADS = 4
RET_DIM = RET_HEADS * HEAD_DIM
RET_CHUNK = 128
SWA_HEADS = 6
SWA_KV_HEADS = 2
SWA_GROUP = SWA_HEADS // SWA_KV_HEADS
SWA_WINDOW = 128
MLA_HEADS = 6
MLA_Q_RANK = 512
MLA_KV_RANK = 256
MLA_NOPE_DIM = 128
MLA_ROPE_DIM = 64
MLA_V_DIM = 128
MLA_SCALE = (MLA_NOPE_DIM + MLA_ROPE_DIM) ** -0.5
MLA_QK_PAD = 256
N_MOD = 6
LN_EPS = 1e-5
RMS_EPS = 1e-6
NEG_INF = -1e30

RET_COLS = 4 * RET_DIM
SWA_COLS = (SWA_HEADS + 2 * SWA_KV_HEADS) * HEAD_DIM
MLA_COLS = MLA_Q_RANK + MLA_KV_RANK + MLA_ROPE_DIM
MLA_COLS_PAD = 896
IN_WIDTH = RET_COLS + SWA_COLS + MLA_COLS
IN_WIDTH_PAD = RET_COLS + SWA_COLS + MLA_COLS_PAD
MIX_WIDTH = RET_DIM + SWA_HEADS * HEAD_DIM + MLA_HEADS * MLA_V_DIM

LANES = 128
SUBLANES = 8
MOD_ROWS = 8
VMEM_LIMIT = 56 * 1024 * 1024
HALO = 16

BF16 = jnp.bfloat16
F32 = jnp.float32


def _cparams(*sem):
    return pltpu.CompilerParams(dimension_semantics=sem, vmem_limit_bytes=VMEM_LIMIT)


def _resident(block, index_map):
    return pl.BlockSpec(block, index_map, pipeline_mode=pl.Buffered(1))


def _dot(a, b):
    return jnp.dot(a, b, preferred_element_type=F32)


def _dot_nt(a, b):
    return lax.dot_general(a, b, (((1,), (1,)), ((), ())), preferred_element_type=F32)


def _silu(x):
    return x * (1.0 / (1.0 + jnp.exp(-x)))


def _rope(x, cos, sin_lo, sin_hi, nq):
    w = x.shape[-1]
    return x * cos + pltpu.roll(x, w - nq, 1) * sin_lo + pltpu.roll(x, nq, 1) * sin_hi


def _layer_norm(z, g, b):
    mu = jnp.mean(z, axis=-1, keepdims=True)
    zc = z - mu
    var = jnp.mean(zc * zc, axis=-1, keepdims=True)
    return zc * lax.rsqrt(var + LN_EPS) * g + b


def _rms(x):
    return x * lax.rsqrt(jnp.mean(x * x, axis=-1, keepdims=True) + RMS_EPS)


def _log_sigmoid(x):
    return -(jnp.maximum(-x, 0.0) + jnp.log(1.0 + jnp.exp(-jnp.abs(x))))


def _ada_kernel(c_ref, w_ref, b_ref, o_ref):
    sc = _silu(c_ref[...]).astype(BF16)
    o_ref[...] = _dot(sc, w_ref[...]) + b_ref[...]


def _ada_mod(cond, ada_w, ada_b):
    depth, d, n = ada_w.shape
    tn = min(n, 2048)
    return pl.pallas_call(
        _ada_kernel,
        grid=(depth, n // tn),
        in_specs=[
            pl.BlockSpec((MOD_ROWS, d), lambda l, j: (0, 0)),
            pl.BlockSpec((None, d, tn), lambda l, j: (l, 0, j)),
            pl.BlockSpec((None, 1, tn), lambda l, j: (l, 0, j)),
        ],
        out_specs=pl.BlockSpec((None, MOD_ROWS, tn), lambda l, j: (l, 0, j)),
        out_shape=jax.ShapeDtypeStruct((depth, MOD_ROWS, n), F32),
        name="ada_mod",
        compiler_params=_cparams("parallel", "parallel"),
    )(cond, ada_w, ada_b.reshape(depth, 1, n))


def _inproj_kernel(x_ref, mod_ref, w_ref, ret_ref, swa_ref, mla_ref):
    m = mod_ref[...]
    h = (x_ref[...] * (1.0 + m[1:2, :]) + m[0:1, :]).astype(BF16)
    ret_ref[...] = _dot(h, w_ref[:, 0:RET_COLS])
    swa_ref[...] = _dot(h, w_ref[:, RET_COLS:RET_COLS + SWA_COLS])
    mla_ref[...] = _dot(h, w_ref[:, RET_COLS + SWA_COLS:IN_WIDTH_PAD])


def _inproj(x2, mods, w_in, layer, mod_row0, rows_per_mod, tm):
    m_rows, d = x2.shape
    mod_idx = lambda i: (layer * MOD_ROWS + mod_row0 + (i * tm) // rows_per_mod, 0, 0)
    return pl.pallas_call(
        _inproj_kernel,
        grid=(m_rows // tm,),
        in_specs=[
            pl.BlockSpec((tm, d), lambda i: (i, 0)),
            pl.BlockSpec((None, N_MOD, d), mod_idx),
            _resident((None, d, IN_WIDTH_PAD), lambda i: (layer, 0, 0)),
        ],
        out_specs=[
            pl.BlockSpec((tm, RET_COLS), lambda i: (i, 0)),
            pl.BlockSpec((tm, SWA_COLS), lambda i: (i, 0)),
            pl.BlockSpec((tm, MLA_COLS_PAD), lambda i: (i, 0)),
        ],
        out_shape=[
            jax.ShapeDtypeStruct((m_rows, RET_COLS), F32),
            jax.ShapeDtypeStruct((m_rows, SWA_COLS), F32),
            jax.ShapeDtypeStruct((m_rows, MLA_COLS_PAD), F32),
        ],
        name="inproj",
        compiler_params=_cparams("parallel"),
    )(x2, mods, w_in)


def _decay_terms(dec_ref):
    lg = _log_sigmoid(dec_ref[...])
    return lg[0, 0:1, 0:1], lg[1, 0:1, 0:1]


def _ret_bwd_kernel(k_ref, v_ref, cos_ref, slo_ref, shi_ref, dec_ref, s0_ref,
                    sin_ref, sfin_ref, state, *, cpb):
    n = pl.program_id(2)
    c = RET_CHUNK

    @pl.when(n == 0)
    def _():
        state[...] = s0_ref[...]

    _, lg_b = _decay_terms(dec_ref)
    pos = lax.broadcasted_iota(jnp.int32, (c, 1), 0).astype(F32)
    kdec = jnp.exp(lg_b * pos)
    cdec = jnp.exp(lg_b * float(c))
    for ci in reversed(range(cpb)):
        rows = pl.ds(ci * c, c)
        sin_ref[ci] = state[...]
        k = _rope(k_ref[rows, :], cos_ref[rows, :], slo_ref[rows, :], shi_ref[rows, :],
                  HEAD_DIM // 4) * (HEAD_DIM ** -0.5)
        kd_t = (k * kdec).T.astype(BF16)
        state[...] = cdec * state[...] + _dot(kd_t, v_ref[rows, :].astype(BF16))

    @pl.when(n == pl.num_programs(2) - 1)
    def _():
        sfin_ref[...] = state[...]


def _ret_fwd_kernel(q_ref, k_ref, v_ref, g_ref, cos_ref, slo_ref, shi_ref, dec_ref, s0_ref, sb_ref,
                    y_ref, sfin_ref, state, *, cpb):
    n = pl.program_id(2)
    c = RET_CHUNK

    @pl.when(n == 0)
    def _():
        state[...] = s0_ref[...]

    lg_f, lg_b = _decay_terms(dec_ref)
    pos = lax.broadcasted_iota(jnp.int32, (c, 1), 0).astype(F32)
    ri = lax.broadcasted_iota(jnp.int32, (c, c), 0)
    cj = lax.broadcasted_iota(jnp.int32, (c, c), 1)
    diff = (ri - cj).astype(F32)
    intra = (jnp.where(diff >= 0, jnp.exp(lg_f * jnp.maximum(diff, 0.0)), 0.0)
             + jnp.where(diff <= 0, jnp.exp(lg_b * jnp.maximum(-diff, 0.0)), 0.0))
    qdec_f = jnp.exp(lg_f * (pos + 1.0))
    qdec_b = jnp.exp(lg_b * (float(c) - pos))
    kdec_f = jnp.exp(lg_f * (float(c) - 1.0 - pos))
    cdec_f = jnp.exp(lg_f * float(c))
    for ci in range(cpb):
        rows = pl.ds(ci * c, c)
        cos, slo, shi = cos_ref[rows, :], slo_ref[rows, :], shi_ref[rows, :]
        q = _rope(q_ref[rows, :], cos, slo, shi, HEAD_DIM // 4)
        k = _rope(k_ref[rows, :], cos, slo, shi, HEAD_DIM // 4) * (HEAD_DIM ** -0.5)
        v = v_ref[rows, :].astype(BF16)
        scores = _dot_nt(q.astype(BF16), k.astype(BF16)) * intra
        out = _dot(scores.astype(BF16), v)
        out = out + _dot((q * qdec_f).astype(BF16), state[...].astype(BF16))
        out = out + _dot((q * qdec_b).astype(BF16), sb_ref[ci].astype(BF16))
        y_ref[rows, :] = (_silu(g_ref[rows, :]) * _rms(out)).astype(y_ref.dtype)
        state[...] = cdec_f * state[...] + _dot((k * kdec_f).T.astype(BF16), v)

    @pl.when(n == pl.num_programs(2) - 1)
    def _():
        sfin_ref[...] = state[...]


def _retention(ret3, tabs, dec, layer, s0_f, s0_b):
    bsz, seq, _ = ret3.shape
    c = RET_CHUNK
    cpb = 4 if seq % (4 * c) == 0 else (2 if seq % (2 * c) == 0 else 1)
    rows = cpb * c
    nb = seq // rows
    cos, slo, shi = tabs
    h = RET_HEADS
    grid = (bsz, h, nb)
    state_spec = pl.BlockSpec((None, None, HEAD_DIM, HEAD_DIM), lambda b, hh, n: (b, hh, 0, 0))
    dec_spec = pl.BlockSpec((None, 2, None, SUBLANES, LANES), lambda b, hh, n: (layer, 0, hh, 0, 0))

    def col(j, rev):
        if rev:
            return pl.BlockSpec((None, rows, HEAD_DIM), lambda b, hh, n: (b, nb - 1 - n, j * h + hh))
        return pl.BlockSpec((None, rows, HEAD_DIM), lambda b, hh, n: (b, n, j * h + hh))

    def tab(rev):
        if rev:
            return pl.BlockSpec((rows, HEAD_DIM), lambda b, hh, n: (nb - 1 - n, 0))
        return pl.BlockSpec((rows, HEAD_DIM), lambda b, hh, n: (n, 0))

    sb_in, st_b = pl.pallas_call(
        functools.partial(_ret_bwd_kernel, cpb=cpb),
        grid=grid,
        in_specs=[col(1, True), col(2, True), tab(True), tab(True), tab(True), dec_spec, state_spec],
        out_specs=[
            pl.BlockSpec((None, None, cpb, HEAD_DIM, HEAD_DIM), lambda b, hh, n: (b, hh, nb - 1 - n, 0, 0)),
            state_spec,
        ],
        out_shape=[
            jax.ShapeDtypeStruct((bsz, h, seq // c, HEAD_DIM, HEAD_DIM), F32),
            jax.ShapeDtypeStruct((bsz, h, HEAD_DIM, HEAD_DIM), F32),
        ],
        scratch_shapes=[pltpu.VMEM((HEAD_DIM, HEAD_DIM), F32)],
        name="ret_bwd",
        compiler_params=_cparams("parallel", "parallel", "arbitrary"),
    )(ret3, ret3, cos, slo, shi, dec, s0_b)

    y, st_f = pl.pallas_call(
        functools.partial(_ret_fwd_kernel, cpb=cpb),
        grid=grid,
        in_specs=[col(0, False), col(1, False), col(2, False), col(3, False),
                  tab(False), tab(False), tab(False), dec_spec, state_spec,
                  pl.BlockSpec((None, None, cpb, HEAD_DIM, HEAD_DIM), lambda b, hh, n: (b, hh, n, 0, 0))],
        out_specs=[
            pl.BlockSpec((None, rows, HEAD_DIM), lambda b, hh, n: (b, n, hh)),
            state_spec,
        ],
        out_shape=[
            jax.ShapeDtypeStruct((bsz, seq, RET_DIM), BF16),
            jax.ShapeDtypeStruct((bsz, h, HEAD_DIM, HEAD_DIM), F32),
        ],
        scratch_shapes=[pltpu.VMEM((HEAD_DIM, HEAD_DIM), F32)],
        name="ret_fwd",
        compiler_params=_cparams("parallel", "parallel", "arbitrary"),
    )(ret3, ret3, ret3, ret3, cos, slo, shi, dec, s0_f, sb_in)
    return y, st_f, st_b


def _swa_kernel(q_ref, k_ref, v_ref, kp_ref, vp_ref, kn_ref, vn_ref, kc_ref, vc_ref,
                cos_ref, slo_ref, shi_ref, cosp_ref, slop_ref, ship_ref, cosn_ref, slon_ref, shin_ref,
                sink_ref, y_ref, *, tq, seq):
    i = pl.program_id(1)
    nq = HEAD_DIM // 4
    scale = HEAD_DIM ** -0.5
    cos, slo, shi = cos_ref[...], slo_ref[...], shi_ref[...]
    nk = tq + 2 * SWA_WINDOW
    q_pos = i * tq + lax.broadcasted_iota(jnp.int32, (tq, nk), 0)
    k_pos = i * tq - SWA_WINDOW + lax.broadcasted_iota(jnp.int32, (tq, nk), 1)
    valid = (jnp.abs(k_pos - q_pos) <= SWA_WINDOW) & (k_pos >= 0) & (k_pos < seq)
    for kv in range(SWA_KV_HEADS):
        ks = pl.ds(kv * HEAD_DIM, HEAD_DIM)
        kcat = jnp.concatenate([
            _rope(kp_ref[:, ks], cosp_ref[...], slop_ref[...], ship_ref[...], nq),
            _rope(k_ref[:, ks], cos, slo, shi, nq),
            _rope(kn_ref[:, ks], cosn_ref[...], slon_ref[...], shin_ref[...], nq)], axis=0).astype(BF16)
        vcat = jnp.concatenate([vp_ref[:, ks], v_ref[:, ks], vn_ref[:, ks]], axis=0).astype(BF16)
        kctx = kc_ref[:, ks].astype(BF16)
        vctx = vc_ref[:, ks].astype(BF16)
        for g in range(SWA_GROUP):
            hq = kv * SWA_GROUP + g
            hs = pl.ds(hq * HEAD_DIM, HEAD_DIM)
            q = _rope(q_ref[:, hs], cos, slo, shi, nq).astype(BF16)
            s_loc = jnp.where(valid, _dot_nt(q, kcat) * scale, NEG_INF)
            s_ctx = _dot_nt(q, kctx) * scale
            sink = sink_ref[hq, 0:1, 0:1]
            m = jnp.maximum(jnp.maximum(jnp.max(s_loc, axis=-1, keepdims=True),
                                        jnp.max(s_ctx, axis=-1, keepdims=True)), sink)
            p_loc = jnp.exp(s_loc - m)
            p_ctx = jnp.exp(s_ctx - m)
            den = (jnp.sum(p_loc, axis=-1, keepdims=True) + jnp.sum(p_ctx, axis=-1, keepdims=True)
                   + jnp.exp(sink - m))
            out = _dot(p_loc.astype(BF16), vcat) + _dot(p_ctx.astype(BF16), vctx)
            y_ref[:, hs] = (out / den).astype(y_ref.dtype)


def _swa_latent(swa3, swa_ctx3, tabs, sink, layer):
    bsz, seq, _ = swa3.shape
    t = swa_ctx3.shape[1]
    w = SWA_WINDOW
    tq = 256 if seq % 256 == 0 else w
    r = tq // w
    nblk = seq // w
    kvw = SWA_KV_HEADS * HEAD_DIM
    qw = SWA_HEADS * HEAD_DIM
    k_col, v_col = qw // kvw, qw // kvw + 1
    cos, slo, shi = tabs
    prev = lambda i: jnp.maximum(i * r - 1, 0)
    nxt = lambda i: jnp.minimum((i + 1) * r, nblk - 1)
    tab_specs = ([pl.BlockSpec((tq, HEAD_DIM), lambda b, i: (i, 0))] * 3
                 + [pl.BlockSpec((w, HEAD_DIM), lambda b, i: (prev(i), 0))] * 3
                 + [pl.BlockSpec((w, HEAD_DIM), lambda b, i: (nxt(i), 0))] * 3)
    return pl.pallas_call(
        functools.partial(_swa_kernel, tq=tq, seq=seq),
        grid=(bsz, seq // tq),
        in_specs=[
            pl.BlockSpec((None, tq, qw), lambda b, i: (b, i, 0)),
            pl.BlockSpec((None, tq, kvw), lambda b, i: (b, i, k_col)),
            pl.BlockSpec((None, tq, kvw), lambda b, i: (b, i, v_col)),
            pl.BlockSpec((None, w, kvw), lambda b, i: (b, prev(i), k_col)),
            pl.BlockSpec((None, w, kvw), lambda b, i: (b, prev(i), v_col)),
            pl.BlockSpec((None, w, kvw), lambda b, i: (b, nxt(i), k_col)),
            pl.BlockSpec((None, w, kvw), lambda b, i: (b, nxt(i), v_col)),
            pl.BlockSpec((None, t, kvw), lambda b, i: (b, 0, k_col)),
            pl.BlockSpec((None, t, kvw), lambda b, i: (b, 0, v_col)),
        ] + tab_specs + [
            pl.BlockSpec((None, SWA_HEADS, SUBLANES, LANES), lambda b, i: (layer, 0, 0, 0)),
        ],
        out_specs=pl.BlockSpec((None, tq, qw), lambda b, i: (b, i, 0)),
        out_shape=jax.ShapeDtypeStruct((bsz, seq, qw), BF16),
        name="swa_latent",
        compiler_params=_cparams("parallel", "parallel"),
    )(swa3, swa3, swa3, swa3, swa3, swa3, swa3, swa_ctx3, swa_ctx3,
      cos, slo, shi, cos, slo, shi, cos, slo, shi, sink)


def _mla_proj_kernel(x_ref, qn_ref, kvn_ref, wq_ref, wk_ref, wv_ref, cos_ref, slo_ref, shi_ref,
                     q_ref, k_ref, v_ref):
    nq = MLA_ROPE_DIM // 4
    cos, slo, shi = cos_ref[...], slo_ref[...], shi_ref[...]
    cq = (_rms(x_ref[:, 0:MLA_Q_RANK]) * qn_ref[...]).astype(BF16)
    q = _dot(cq, wq_ref[...])
    ckv = (_rms(x_ref[:, MLA_Q_RANK:MLA_Q_RANK + MLA_KV_RANK]) * kvn_ref[...]).astype(BF16)
    kn = _dot(ckv, wk_ref[...])
    v_ref[...] = _dot(ckv, wv_ref[...]).astype(v_ref.dtype)
    kr = _rope(x_ref[:, MLA_Q_RANK + MLA_KV_RANK:MLA_COLS_PAD], cos, slo, shi, nq).astype(k_ref.dtype)
    for h in range(MLA_HEADS):
        a = h * MLA_QK_PAD
        q_ref[:, a:a + MLA_NOPE_DIM] = q[:, a:a + MLA_NOPE_DIM].astype(q_ref.dtype)
        q_ref[:, a + MLA_NOPE_DIM:a + MLA_QK_PAD] = _rope(
            q[:, a + MLA_NOPE_DIM:a + MLA_QK_PAD], cos, slo, shi, nq).astype(q_ref.dtype)
        k_ref[:, a:a + MLA_NOPE_DIM] = kn[:, h * MLA_NOPE_DIM:(h + 1) * MLA_NOPE_DIM].astype(k_ref.dtype)
        k_ref[:, a + MLA_NOPE_DIM:a + MLA_QK_PAD] = kr


def _mla_project(mla2, seq, qn, kvn, wq, wk, wv, tabs, layer, tm):
    m_rows = mla2.shape[0]
    cos, slo, shi = tabs
    tpb = seq // tm
    qkw = MLA_HEADS * MLA_QK_PAD
    vw = MLA_HEADS * MLA_V_DIM
    lay = lambda i: (layer, 0, 0)
    tab = pl.BlockSpec((tm, LANES), lambda i: (i % tpb, 0))
    return pl.pallas_call(
        _mla_proj_kernel,
        grid=(m_rows // tm,),
        in_specs=[
            pl.BlockSpec((tm, MLA_COLS_PAD), lambda i: (i, 0)),
            pl.BlockSpec((None, 1, MLA_Q_RANK), lay),
            pl.BlockSpec((None, 1, MLA_KV_RANK), lay),
            pl.BlockSpec((None, MLA_Q_RANK, qkw), lay),
            pl.BlockSpec((None, MLA_KV_RANK, vw), lay),
            pl.BlockSpec((None, MLA_KV_RANK, vw), lay),
            tab, tab, tab,
        ],
        out_specs=[
            pl.BlockSpec((tm, qkw), lambda i: (i, 0)),
            pl.BlockSpec((tm, qkw), lambda i: (i, 0)),
            pl.BlockSpec((tm, vw), lambda i: (i, 0)),
        ],
        out_shape=[
            jax.ShapeDtypeStruct((m_rows, qkw), BF16),
            jax.ShapeDtypeStruct((m_rows, qkw), BF16),
            jax.ShapeDtypeStruct((m_rows, vw), BF16),
        ],
        name="mla_project",
        compiler_params=_cparams("parallel"),
    )(mla2, qn, kvn, wq, wk, wv, cos, slo, shi)


def _attn_kernel(*refs, n_src, tk, scale, has_sink):
    q_ref = refs[0]
    k_refs = refs[1:1 + n_src]
    v_refs = refs[1 + n_src:1 + 2 * n_src]
    pos = 1 + 2 * n_src
    sink_ref = refs[pos] if has_sink else None
    o_ref = refs[pos + (1 if has_sink else 0)]
    q = q_ref[...].astype(BF16)
    tq = q.shape[0]
    dv = o_ref.shape[-1]
    carry = (jnp.full((tq, 1), NEG_INF, F32), jnp.zeros((tq, 1), F32), jnp.zeros((tq, dv), F32))
    for k_ref, v_ref in zip(k_refs, v_refs):
        sk = k_ref.shape[0]
        tkk = min(tk, sk)

        def body(j, c, k_ref=k_ref, v_ref=v_ref, tkk=tkk):
            m, l, acc = c
            rows = pl.ds(pl.multiple_of(j * tkk, tkk), tkk)
            s = _dot_nt(q, k_ref[rows, :].astype(BF16)) * scale
            m_new = jnp.maximum(m, jnp.max(s, axis=-1, keepdims=True))
            alpha = jnp.exp(m - m_new)
            p = jnp.exp(s - m_new)
            l = alpha * l + jnp.sum(p, axis=-1, keepdims=True)
            acc = alpha * acc + _dot(p.astype(BF16), v_ref[rows, :].astype(BF16))
            return m_new, l, acc

        carry = lax.fori_loop(0, sk // tkk, body, carry)
    m, l, acc = carry
    if has_sink:
        sink = sink_ref[0:1, 0:1]
        m_new = jnp.maximum(m, sink)
        alpha = jnp.exp(m - m_new)
        l = alpha * l + jnp.exp(sink - m_new)
        acc = alpha * acc
    o_ref[...] = (acc / l).astype(o_ref.dtype)


def _attention(q3, ks, vs, *, n_heads, group, dq, dv, q_col0, k_col0, v_col0, scale, sink=None, layer=0,
               tq=512, tk=512):
    bsz, lq, _ = q3.shape
    tq = min(tq, lq)
    n_src = len(ks)
    in_specs = [pl.BlockSpec((None, tq, dq), lambda b, h, i: (b, i, q_col0 + h))]
    in_specs += [pl.BlockSpec((None, k.shape[1], dq), lambda b, h, i: (b, 0, k_col0 + h // group)) for k in ks]
    in_specs += [pl.BlockSpec((None, v.shape[1], dv), lambda b, h, i: (b, 0, v_col0 + h // group)) for v in vs]
    args = [q3] + list(ks) + list(vs)
    if sink is not None:
        in_specs.append(pl.BlockSpec((None, None, SUBLANES, LANES), lambda b, h, i: (layer, h, 0, 0)))
        args.append(sink)
    return pl.pallas_call(
        functools.partial(_attn_kernel, n_src=n_src, tk=tk, scale=scale, has_sink=sink is not None),
        grid=(bsz, n_heads, lq // tq),
        in_specs=in_specs,
        out_specs=pl.BlockSpec((None, tq, dv), lambda b, h, i: (b, i, h)),
        out_shape=jax.ShapeDtypeStruct((bsz, lq, n_heads * dv), BF16),
        name="attention",
        compiler_params=_cparams("parallel", "parallel", "arbitrary"),
    )(*args)


def _outproj_kernel(yr_ref, ys_ref, ym_ref, x_ref, mod_ref, w_ref, g_ref, b_ref, o_ref, *, alpha):
    a, b = RET_DIM, RET_DIM + SWA_HEADS * HEAD_DIM
    mix = (_dot(yr_ref[...], w_ref[0:a, :]) + _dot(ys_ref[...], w_ref[a:b, :])
           + _dot(ym_ref[...], w_ref[b:MIX_WIDTH, :]))
    z = alpha * x_ref[...] + (1.0 + mod_ref[2:3, :]) * mix
    o_ref[...] = _layer_norm(z, g_ref[...], b_ref[...])


def _outproj_ln(y_ret, y_swa, y_mla, x2, mods, w_o, ln_g, ln_b, layer, mod_row0, rows_per_mod, tm, alpha):
    m_rows, d = x2.shape
    mod_idx = lambda i: (layer * MOD_ROWS + mod_row0 + (i * tm) // rows_per_mod, 0, 0)
    lay = lambda i: (layer, 0, 0)
    row = lambda w: pl.BlockSpec((tm, w), lambda i: (i, 0))
    return pl.pallas_call(
        functools.partial(_outproj_kernel, alpha=alpha),
        grid=(m_rows // tm,),
        in_specs=[
            row(RET_DIM), row(SWA_HEADS * HEAD_DIM), row(MLA_HEADS * MLA_V_DIM), row(d),
            pl.BlockSpec((None, N_MOD, d), mod_idx),
            _resident((None, MIX_WIDTH, d), lay),
            pl.BlockSpec((None, 1, d), lay),
            pl.BlockSpec((None, 1, d), lay),
        ],
        out_specs=row(d),
        out_shape=jax.ShapeDtypeStruct((m_rows, d), F32),
        name="outproj_ln",
        compiler_params=_cparams("parallel"),
    )(y_ret, y_swa, y_mla, x2, mods, w_o, ln_g, ln_b)


def _ffn_kernel(x_ref, xp_ref, xn_ref, mod_ref, wu_ref, wg_ref, cw_ref, cb_ref, wd_ref, g_ref, b_ref,
                o_ref, h_ext, g_ext, acc, *, tm, seq, alpha):
    i = pl.program_id(0)
    f = pl.program_id(1)

    @pl.when(f == 0)
    def _():
        scale = 1.0 + mod_ref[4:5, :]
        shift = mod_ref[3:4, :]
        has_prev = jnp.where((i * tm) % seq != 0, 1.0, 0.0).astype(F32)
        has_next = jnp.where(((i + 1) * tm) % seq != 0, 1.0, 0.0).astype(F32)
        h_ext[0:HALO, :] = ((xp_ref[...] * scale + shift) * has_prev).astype(BF16)
        h_ext[HALO:HALO + tm, :] = (x_ref[...] * scale + shift).astype(BF16)
        h_ext[HALO + tm:tm + 2 * HALO, :] = ((xn_ref[...] * scale + shift) * has_next).astype(BF16)
        acc[...] = jnp.zeros_like(acc)

    u = _dot(h_ext[HALO:HALO + tm, :], wu_ref[...])
    g_ext[...] = _dot(h_ext[...], wg_ref[...])
    gc = (g_ext[HALO - 1:HALO - 1 + tm, :] * cw_ref[0:1, :] + g_ext[HALO:HALO + tm, :] * cw_ref[1:2, :]
          + g_ext[HALO + 1:HALO + 1 + tm, :] * cw_ref[2:3, :] + cb_ref[...])
    a = (_silu(gc) * u).astype(BF16)
    acc[...] += _dot(a, wd_ref[...])

    @pl.when(f == pl.num_programs(1) - 1)
    def _():
        xa = x_ref[...]
        z = alpha * xa + (1.0 + mod_ref[5:6, :]) * acc[...]
        o_ref[...] = _layer_norm(z, g_ref[...], b_ref[...])


def _ffn(x2, seq, mods, w_up, conv_w, conv_b, w_down, ln_g, ln_b, layer, mod_row0, rows_per_mod, tm, alpha):
    m_rows, d = x2.shape
    dff = w_down.shape[1]
    tf = 512 if dff % 512 == 0 else (256 if dff % 256 == 0 else LANES)
    nf = dff // tf
    hb = tm // HALO
    nhb = m_rows // HALO
    mod_idx = lambda i, f: (layer * MOD_ROWS + mod_row0 + (i * tm) // rows_per_mod, 0, 0)
    lay = lambda i, f: (layer, 0, 0)
    return pl.pallas_call(
        functools.partial(_ffn_kernel, tm=tm, seq=seq, alpha=alpha),
        grid=(m_rows // tm, nf),
        in_specs=[
            pl.BlockSpec((tm, d), lambda i, f: (i, 0)),
            pl.BlockSpec((HALO, d), lambda i, f: (jnp.maximum(i * hb - 1, 0), 0)),
            pl.BlockSpec((HALO, d), lambda i, f: (jnp.minimum((i + 1) * hb, nhb - 1), 0)),
            pl.BlockSpec((None, N_MOD, d), mod_idx),
            pl.BlockSpec((None, d, tf), lambda i, f: (layer, 0, f)),
            pl.BlockSpec((None, d, tf), lambda i, f: (layer, 0, nf + f)),
            pl.BlockSpec((None, 3, tf), lambda i, f: (layer, 0, f)),
            pl.BlockSpec((None, 1, tf), lambda i, f: (layer, 0, f)),
            pl.BlockSpec((None, tf, d), lambda i, f: (layer, f, 0)),
            pl.BlockSpec((None, 1, d), lay),
            pl.BlockSpec((None, 1, d), lay),
        ],
        out_specs=pl.BlockSpec((tm, d), lambda i, f: (i, 0)),
        out_shape=jax.ShapeDtypeStruct((m_rows, d), F32),
        scratch_shapes=[
            pltpu.VMEM((tm + 2 * HALO, d), BF16),
            pltpu.VMEM((tm + 2 * HALO, tf), F32),
            pltpu.VMEM((tm, d), F32),
        ],
        name="conv_ffn",
        compiler_params=_cparams("parallel", "arbitrary"),
    )(x2, x2, x2, mods, w_up, w_up, conv_w, conv_b, w_down, ln_g, ln_b)


def _rope_tables(n_tokens, dim):
    rows = n_tokens // GRID_W
    r = np.repeat(np.arange(rows, dtype=np.float32), GRID_W)
    cc = np.tile(np.arange(GRID_W, dtype=np.float32), rows)
    n_freq = dim // 4
    inv = jnp.asarray(ROPE_THETA, F32) ** (-jnp.arange(n_freq, dtype=F32) / n_freq)
    ang_r = jnp.asarray(r)[:, None] * inv
    ang_c = jnp.asarray(cc)[:, None] * inv
    ang = jnp.concatenate([ang_r, ang_r, ang_c, ang_c], axis=-1)
    cos, sin = jnp.cos(ang), jnp.sin(ang)
    lane = np.arange(dim)
    lo = jnp.asarray((lane % (2 * n_freq)) < n_freq)
    sin_lo = jnp.where(lo, -sin, 0.0)
    sin_hi = jnp.where(lo, 0.0, sin)
    pad = LANES - dim
    if pad:
        cos = jnp.pad(cos, ((0, 0), (0, pad)), constant_values=1.0)
        sin_lo = jnp.pad(sin_lo, ((0, 0), (0, pad)))
        sin_hi = jnp.pad(sin_hi, ((0, 0), (0, pad)))
    return cos, sin_lo, sin_hi


def _identity_tables(n_tokens):
    return (jnp.ones((n_tokens, LANES), F32), jnp.zeros((n_tokens, LANES), F32),
            jnp.zeros((n_tokens, LANES), F32))


def _lane_bcast(p):
    return jnp.broadcast_to(p.astype(F32)[..., None, None], p.shape + (SUBLANES, LANES))


def kernel(x, c, ctx, c_ctx, ada_w, ada_b, w_in, ret_decay_fwd, ret_decay_bwd, swa_sink, mla_q_norm, mla_w_uq,
           mla_kv_norm, mla_w_ukv, w_o, ln1_g, ln1_b, ffn_w_up, ffn_conv_w, ffn_conv_b, ffn_w_down, ln2_g, ln2_b):
    bsz, seq, d = x.shape
    t = ctx.shape[1]
    depth = w_in.shape[0]
    assert bsz + 1 <= MOD_ROWS and seq % RET_CHUNK == 0 and t % RET_CHUNK == 0
    assert seq % GRID_W == 0 and d % LANES == 0
    alpha = (2 * depth) ** 0.25

    ada_w_b = ada_w.astype(BF16)
    w_in_b = jnp.pad(w_in, ((0, 0), (0, 0), (0, IN_WIDTH_PAD - IN_WIDTH))).astype(BF16)
    w_o_b = w_o.astype(BF16)
    w_up_b = ffn_w_up.astype(BF16)
    w_down_b = ffn_w_down.astype(BF16)
    hq = MLA_NOPE_DIM + MLA_ROPE_DIM
    wq = mla_w_uq.reshape(depth, MLA_Q_RANK, MLA_HEADS, hq)
    wq = jnp.pad(wq, ((0, 0), (0, 0), (0, 0), (0, MLA_QK_PAD - hq)))
    wq = wq.reshape(depth, MLA_Q_RANK, MLA_HEADS * MLA_QK_PAD).astype(BF16)
    wkv = mla_w_ukv.reshape(depth, MLA_KV_RANK, MLA_HEADS, MLA_NOPE_DIM + MLA_V_DIM)
    wk = wkv[..., :MLA_NOPE_DIM].reshape(depth, MLA_KV_RANK, MLA_HEADS * MLA_NOPE_DIM).astype(BF16)
    wv = wkv[..., MLA_NOPE_DIM:].reshape(depth, MLA_KV_RANK, MLA_HEADS * MLA_V_DIM).astype(BF16)
    qn = mla_q_norm.reshape(depth, 1, MLA_Q_RANK)
    kvn = mla_kv_norm.reshape(depth, 1, MLA_KV_RANK)
    g1, b1 = ln1_g.reshape(depth, 1, d), ln1_b.reshape(depth, 1, d)
    g2, b2 = ln2_g.reshape(depth, 1, d), ln2_b.reshape(depth, 1, d)
    conv_b = ffn_conv_b.reshape(depth, 1, -1)
    dec = _lane_bcast(jnp.stack([ret_decay_fwd, ret_decay_bwd], axis=1))
    sink = _lane_bcast(swa_sink)

    tabs_h = _rope_tables(seq, HEAD_DIM)
    tabs_m = _rope_tables(seq, MLA_ROPE_DIM)
    tabs_id = _identity_tables(t)

    cond = jnp.zeros((MOD_ROWS, d), F32).at[:bsz].set(c).at[bsz].set(c_ctx)
    mods = _ada_mod(cond, ada_w_b, ada_b).reshape(depth * MOD_ROWS, N_MOD, d)

    tm_x = 512 if seq % 512 == 0 else RET_CHUNK
    tm_c = t if t <= 512 else RET_CHUNK
    zeros_state = jnp.zeros((bsz, RET_HEADS, HEAD_DIM, HEAD_DIM), F32)
    swa_scale = HEAD_DIM ** -0.5

    x2 = x.reshape(bsz * seq, d)
    xc2 = ctx.reshape(bsz * t, d)
    for l in range(depth):
        last = l == depth - 1
        ret_c, swa_c, mla_c = _inproj(xc2, mods, w_in_b, l, bsz, bsz * t, tm_c)
        swa_c3 = swa_c.reshape(bsz, t, SWA_COLS)
        y_ret_c, st_f, st_b = _retention(ret_c.reshape(bsz, t, RET_COLS), tabs_id, dec, l,
                                         zeros_state, zeros_state)
        q_c, k_c, v_c = _mla_project(mla_c, t, qn, kvn, wq, wk, wv, tabs_id, l, tm_c)
        k_c3 = k_c.reshape(bsz, t, -1)
        v_c3 = v_c.reshape(bsz, t, -1)

        ret_x, swa_x, mla_x = _inproj(x2, mods, w_in_b, l, 0, seq, tm_x)
        y_ret, _, _ = _retention(ret_x.reshape(bsz, seq, RET_COLS), tabs_h, dec, l, st_f, st_b)
        y_swa = _swa_latent(swa_x.reshape(bsz, seq, SWA_COLS), swa_c3, tabs_h, sink, l)
        q_x, k_x, v_x = _mla_project(mla_x, seq, qn, kvn, wq, wk, wv, tabs_m, l, tm_x)
        y_mla = _attention(q_x.reshape(bsz, seq, -1), [k_x.reshape(bsz, seq, -1), k_c3],
                           [v_x.reshape(bsz, seq, -1), v_c3], n_heads=MLA_HEADS, group=1,
                           dq=MLA_QK_PAD, dv=MLA_V_DIM, q_col0=0, k_col0=0, v_col0=0, scale=MLA_SCALE)
        x_a = _outproj_ln(y_ret.reshape(bsz * seq, -1), y_swa.reshape(bsz * seq, -1),
                          y_mla.reshape(bsz * seq, -1), x2, mods, w_o_b, g1, b1, l, 0, seq, tm_x, alpha)
        x_new = _ffn(x_a, seq, mods, w_up_b, ffn_conv_w, conv_b, w_down_b, g2, b2, l, 0, seq, tm_x, alpha)

        if not last:
            y_swa_c = _attention(swa_c3, [swa_c3], [swa_c3], n_heads=SWA_HEADS, group=SWA_GROUP,
                                 dq=HEAD_DIM, dv=HEAD_DIM, q_col0=0, k_col0=SWA_HEADS,
                                 v_col0=SWA_HEADS + SWA_KV_HEADS, scale=swa_scale, sink=sink, layer=l)
            y_mla_c = _attention(q_c.reshape(bsz, t, -1), [k_c3], [v_c3], n_heads=MLA_HEADS, group=1,
                                 dq=MLA_QK_PAD, dv=MLA_V_DIM, q_col0=0, k_col0=0, v_col0=0, scale=MLA_SCALE)
            xc_a = _outproj_ln(y_ret_c.reshape(bsz * t, -1), y_swa_c.reshape(bsz * t, -1),
                               y_mla_c.reshape(bsz * t, -1), xc2, mods, w_o_b, g1, b1, l, bsz, bsz * t,
                               tm_c, alpha)
            xc2 = _ffn(xc_a, t, mods, w_up_b, ffn_conv_w, conv_b, w_down_b, g2, b2, l, bsz, bsz * t,
                       tm_c, alpha)
        x2 = x_new
    return x2.reshape(bsz, seq, d)
```

```python
import functools

import jax
import jax.numpy as jnp
import numpy as np
from jax import lax
from jax.experimental import pallas as pl
from jax.experimental.pallas import tpu as pltpu

GRID_W = 64
HEAD_DIM = 128
ROPE_THETA = 10000.0
RET_HEADS = 4
RET_DIM = RET_HEADS * HEAD_DIM
RET_CHUNK = 128
SWA_HEADS = 6
SWA_KV_HEADS = 2
SWA_GROUP = SWA_HEADS // SWA_KV_HEADS
SWA_WINDOW = 128
MLA_HEADS = 6
MLA_Q_RANK = 512
MLA_KV_RANK = 256
MLA_NOPE_DIM = 128
MLA_ROPE_DIM = 64
MLA_V_DIM = 128
MLA_SCALE = (MLA_NOPE_DIM + MLA_ROPE_DIM) ** -0.5
MLA_QK_PAD = 256
N_MOD = 6
LN_EPS = 1e-5
RMS_EPS = 1e-6
NEG_INF = -1e30
LOG2E = 1.4426950408889634

RET_COLS = 4 * RET_DIM
SWA_COLS = (SWA_HEADS + 2 * SWA_KV_HEADS) * HEAD_DIM
MLA_COLS = MLA_Q_RANK + MLA_KV_RANK + MLA_ROPE_DIM
MLA_COLS_PAD = 896
IN_WIDTH = RET_COLS + SWA_COLS + MLA_COLS
IN_WIDTH_PAD = RET_COLS + SWA_COLS + MLA_COLS_PAD
MIX_WIDTH = RET_DIM + SWA_HEADS * HEAD_DIM + MLA_HEADS * MLA_V_DIM

LANES = 128
SUBLANES = 8
MOD_ROWS = 8
VMEM_LIMIT = 56 * 1024 * 1024
HALO = 16

BF16 = jnp.bfloat16
F32 = jnp.float32


def _cparams(*sem):
    return pltpu.CompilerParams(dimension_semantics=sem, vmem_limit_bytes=VMEM_LIMIT)


def _resident(block, index_map):
    return pl.BlockSpec(block, index_map, pipeline_mode=pl.Buffered(1))


def _dot(a, b):
    return jnp.dot(a, b, preferred_element_type=F32)


def _dot_nt(a, b):
    return lax.dot_general(a, b, (((1,), (1,)), ((), ())), preferred_element_type=F32)


def _silu(x):
    return x * (1.0 / (1.0 + jnp.exp(-x)))


def _rope(x, cos, sin_lo, sin_hi, nq):
    w = x.shape[-1]
    return x * cos + pltpu.roll(x, w - nq, 1) * sin_lo + pltpu.roll(x, nq, 1) * sin_hi


def _layer_norm(z, g, b):
    mu = jnp.mean(z, axis=-1, keepdims=True)
    zc = z - mu
    var = jnp.mean(zc * zc, axis=-1, keepdims=True)
    return zc * lax.rsqrt(var + LN_EPS) * g + b


def _rms(x):
    return x * lax.rsqrt(jnp.mean(x * x, axis=-1, keepdims=True) + RMS_EPS)


def _log_sigmoid(x):
    return -(jnp.maximum(-x, 0.0) + jnp.log(1.0 + jnp.exp(-jnp.abs(x))))


def _ada_kernel(c_ref, w_ref, b_ref, o_ref):
    sc = _silu(c_ref[...]).astype(BF16)
    o_ref[...] = _dot(sc, w_ref[...]) + b_ref[...]


def _ada_mod(cond, ada_w, ada_b):
    depth, d, n = ada_w.shape
    tn = min(n, 2048)
    return pl.pallas_call(
        _ada_kernel,
        grid=(depth, n // tn),
        in_specs=[
            pl.BlockSpec((MOD_ROWS, d), lambda l, j: (0, 0)),
            pl.BlockSpec((None, d, tn), lambda l, j: (l, 0, j)),
            pl.BlockSpec((None, 1, tn), lambda l, j: (l, 0, j)),
        ],
        out_specs=pl.BlockSpec((None, MOD_ROWS, tn), lambda l, j: (l, 0, j)),
        out_shape=jax.ShapeDtypeStruct((depth, MOD_ROWS, n), F32),
        name="ada_mod",
        compiler_params=_cparams("parallel", "parallel"),
    )(cond, ada_w, ada_b.reshape(depth, 1, n))


def _inproj_kernel(x_ref, mod_ref, w_ref, ret_ref, swa_ref, mla_ref):
    m = mod_ref[...]
    h = (x_ref[...] * (1.0 + m[1:2, :]) + m[0:1, :]).astype(BF16)
    ret_ref[...] = _dot(h, w_ref[:, 0:RET_COLS])
    swa_ref[...] = _dot(h, w_ref[:, RET_COLS:RET_COLS + SWA_COLS])
    mla_ref[...] = _dot(h, w_ref[:, RET_COLS + SWA_COLS:IN_WIDTH_PAD])


def _inproj(x2, mods, w_in, layer, mod_row0, rows_per_mod, tm):
    m_rows, d = x2.shape
    mod_idx = lambda i: (layer * MOD_ROWS + mod_row0 + (i * tm) // rows_per_mod, 0, 0)
    return pl.pallas_call(
        _inproj_kernel,
        grid=(m_rows // tm,),
        in_specs=[
            pl.BlockSpec((tm, d), lambda i: (i, 0)),
            pl.BlockSpec((None, N_MOD, d), mod_idx),
            _resident((None, d, IN_WIDTH_PAD), lambda i: (layer, 0, 0)),
        ],
        out_specs=[
            pl.BlockSpec((tm, RET_COLS), lambda i: (i, 0)),
            pl.BlockSpec((tm, SWA_COLS), lambda i: (i, 0)),
            pl.BlockSpec((tm, MLA_COLS_PAD), lambda i: (i, 0)),
        ],
        out_shape=[
            jax.ShapeDtypeStruct((m_rows, RET_COLS), F32),
            jax.ShapeDtypeStruct((m_rows, SWA_COLS), F32),
            jax.ShapeDtypeStruct((m_rows, MLA_COLS_PAD), F32),
        ],
        name="inproj",
        compiler_params=_cparams("parallel"),
    )(x2, mods, w_in)


def _decay_terms(dec_ref):
    lg = _log_sigmoid(dec_ref[...])
    return lg[0, 0:1, 0:1], lg[1, 0:1, 0:1]


def _ret_bwd_kernel(k_ref, v_ref, cos_ref, slo_ref, shi_ref, dec_ref, s0_ref,
                    sin_ref, sfin_ref, state, *, cpb):
    n = pl.program_id(2)
    c = RET_CHUNK

    @pl.when(n == 0)
    def _():
        state[...] = s0_ref[...]

    _, lg_b = _decay_terms(dec_ref)
    pos = lax.broadcasted_iota(jnp.int32, (c, 1), 0).astype(F32)
    kdec = jnp.exp(lg_b * pos)
    cdec = jnp.exp(lg_b * float(c))
    for ci in reversed(range(cpb)):
        rows = pl.ds(ci * c, c)
        sin_ref[ci] = state[...]
        k = _rope(k_ref[rows, :], cos_ref[rows, :], slo_ref[rows, :], shi_ref[rows, :],
                  HEAD_DIM // 4) * (HEAD_DIM ** -0.5)
        kd_t = (k * kdec).T.astype(BF16)
        state[...] = cdec * state[...] + _dot(kd_t, v_ref[rows, :].astype(BF16))

    @pl.when(n == pl.num_programs(2) - 1)
    def _():
        sfin_ref[...] = state[...]


def _ret_fwd_kernel(q_ref, k_ref, v_ref, g_ref, cos_ref, slo_ref, shi_ref, dec_ref, s0_ref, sb_ref,
                    y_ref, sfin_ref, state, *, cpb):
    n = pl.program_id(2)
    c = RET_CHUNK

    @pl.when(n == 0)
    def _():
        state[...] = s0_ref[...]

    lg_f, lg_b = _decay_terms(dec_ref)
    pos = lax.broadcasted_iota(jnp.int32, (c, 1), 0).astype(F32)
    ri = lax.broadcasted_iota(jnp.int32, (c, c), 0)
    cj = lax.broadcasted_iota(jnp.int32, (c, c), 1)
    diff = (ri - cj).astype(F32)
    intra = (jnp.where(diff >= 0, jnp.exp(lg_f * jnp.maximum(diff, 0.0)), 0.0)
             + jnp.where(diff <= 0, jnp.exp(lg_b * jnp.maximum(-diff, 0.0)), 0.0))
    qdec_f = jnp.exp(lg_f * (pos + 1.0))
    qdec_b = jnp.exp(lg_b * (float(c) - pos))
    kdec_f = jnp.exp(lg_f * (float(c) - 1.0 - pos))
    cdec_f = jnp.exp(lg_f * float(c))
    for ci in range(cpb):
        rows = pl.ds(ci * c, c)
        cos, slo, shi = cos_ref[rows, :], slo_ref[rows, :], shi_ref[rows, :]
        q = _rope(q_ref[rows, :], cos, slo, shi, HEAD_DIM // 4)
        k = _rope(k_ref[rows, :], cos, slo, shi, HEAD_DIM // 4) * (HEAD_DIM ** -0.5)
        v = v_ref[rows, :].astype(BF16)
        scores = _dot_nt(q.astype(BF16), k.astype(BF16)) * intra
        out = _dot(scores.astype(BF16), v)
        out = out + _dot((q * qdec_f).astype(BF16), state[...].astype(BF16))
        out = out + _dot((q * qdec_b).astype(BF16), sb_ref[ci].astype(BF16))
        y_ref[rows, :] = (_silu(g_ref[rows, :]) * _rms(out)).astype(y_ref.dtype)
        state[...] = cdec_f * state[...] + _dot((k * kdec_f).T.astype(BF16), v)

    @pl.when(n == pl.num_programs(2) - 1)
    def _():
        sfin_ref[...] = state[...]


def _retention(ret3, tabs, dec, layer, s0_f, s0_b):
    bsz, seq, _ = ret3.shape
    c = RET_CHUNK
    cpb = 4 if seq % (4 * c) == 0 else (2 if seq % (2 * c) == 0 else 1)
    rows = cpb * c
    nb = seq // rows
    cos, slo, shi = tabs
    h = RET_HEADS
    grid = (bsz, h, nb)
    state_spec = pl.BlockSpec((None, None, HEAD_DIM, HEAD_DIM), lambda b, hh, n: (b, hh, 0, 0))
    dec_spec = pl.BlockSpec((None, 2, None, SUBLANES, LANES), lambda b, hh, n: (layer, 0, hh, 0, 0))

    def col(j, rev):
        if rev:
            return pl.BlockSpec((None, rows, HEAD_DIM), lambda b, hh, n: (b, nb - 1 - n, j * h + hh))
        return pl.BlockSpec((None, rows, HEAD_DIM), lambda b, hh, n: (b, n, j * h + hh))

    def tab(rev):
        if rev:
            return pl.BlockSpec((rows, HEAD_DIM), lambda b, hh, n: (nb - 1 - n, 0))
        return pl.BlockSpec((rows, HEAD_DIM), lambda b, hh, n: (n, 0))

    sb_in, st_b = pl.pallas_call(
        functools.partial(_ret_bwd_kernel, cpb=cpb),
        grid=grid,
        in_specs=[col(1, True), col(2, True), tab(True), tab(True), tab(True), dec_spec, state_spec],
        out_specs=[
            pl.BlockSpec((None, None, cpb, HEAD_DIM, HEAD_DIM), lambda b, hh, n: (b, hh, nb - 1 - n, 0, 0)),
            state_spec,
        ],
        out_shape=[
            jax.ShapeDtypeStruct((bsz, h, seq // c, HEAD_DIM, HEAD_DIM), F32),
            jax.ShapeDtypeStruct((bsz, h, HEAD_DIM, HEAD_DIM), F32),
        ],
        scratch_shapes=[pltpu.VMEM((HEAD_DIM, HEAD_DIM), F32)],
        name="ret_bwd",
        compiler_params=_cparams("parallel", "parallel", "arbitrary"),
    )(ret3, ret3, cos, slo, shi, dec, s0_b)

    y, st_f = pl.pallas_call(
        functools.partial(_ret_fwd_kernel, cpb=cpb),
        grid=grid,
        in_specs=[col(0, False), col(1, False), col(2, False), col(3, False),
                  tab(False), tab(False), tab(False), dec_spec, state_spec,
                  pl.BlockSpec((None, None, cpb, HEAD_DIM, HEAD_DIM), lambda b, hh, n: (b, hh, n, 0, 0))],
        out_specs=[
            pl.BlockSpec((None, rows, HEAD_DIM), lambda b, hh, n: (b, n, hh)),
            state_spec,
        ],
        out_shape=[
            jax.ShapeDtypeStruct((bsz, seq, RET_DIM), BF16),
            jax.ShapeDtypeStruct((bsz, h, HEAD_DIM, HEAD_DIM), F32),
        ],
        scratch_shapes=[pltpu.VMEM((HEAD_DIM, HEAD_DIM), F32)],
        name="ret_fwd",
        compiler_params=_cparams("parallel", "parallel", "arbitrary"),
    )(ret3, ret3, ret3, ret3, cos, slo, shi, dec, s0_f, sb_in)
    return y, st_f, st_b


def _swa_kernel(q_ref, k_ref, v_ref, kp_ref, vp_ref, kn_ref, vn_ref, kc_ref, vc_ref,
                cos_ref, slo_ref, shi_ref, cosp_ref, slop_ref, ship_ref, cosn_ref, slon_ref, shin_ref,
                sink_ref, y_ref, *, tq, seq):
    i = pl.program_id(1)
    nq = HEAD_DIM // 4
    scale = HEAD_DIM ** -0.5
    cos, slo, shi = cos_ref[...], slo_ref[...], shi_ref[...]
    nk = tq + 2 * SWA_WINDOW
    q_pos = i * tq + lax.broadcasted_iota(jnp.int32, (tq, nk), 0)
    k_pos = i * tq - SWA_WINDOW + lax.broadcasted_iota(jnp.int32, (tq, nk), 1)
    valid = (jnp.abs(k_pos - q_pos) <= SWA_WINDOW) & (k_pos >= 0) & (k_pos < seq)
    for kv in range(SWA_KV_HEADS):
        ks = pl.ds(kv * HEAD_DIM, HEAD_DIM)
        kcat = jnp.concatenate([
            _rope(kp_ref[:, ks], cosp_ref[...], slop_ref[...], ship_ref[...], nq),
            _rope(k_ref[:, ks], cos, slo, shi, nq),
            _rope(kn_ref[:, ks], cosn_ref[...], slon_ref[...], shin_ref[...], nq)], axis=0).astype(BF16)
        vcat = jnp.concatenate([vp_ref[:, ks], v_ref[:, ks], vn_ref[:, ks]], axis=0).astype(BF16)
        kctx = kc_ref[:, ks].astype(BF16)
        vctx = vc_ref[:, ks].astype(BF16)
        for g in range(SWA_GROUP):
            hq = kv * SWA_GROUP + g
            hs = pl.ds(hq * HEAD_DIM, HEAD_DIM)
            q = _rope(q_ref[:, hs], cos, slo, shi, nq).astype(BF16)
            s_loc = jnp.where(valid, _dot_nt(q, kcat) * scale, NEG_INF)
            s_ctx = _dot_nt(q, kctx) * scale
            sink = sink_ref[hq, 0:1, 0:1]
            m = jnp.maximum(jnp.maximum(jnp.max(s_loc, axis=-1, keepdims=True),
                                        jnp.max(s_ctx, axis=-1, keepdims=True)), sink)
            p_loc = jnp.exp(s_loc - m)
            p_ctx = jnp.exp(s_ctx - m)
            den = (jnp.sum(p_loc, axis=-1, keepdims=True) + jnp.sum(p_ctx, axis=-1, keepdims=True)
                   + jnp.exp(sink - m))
            out = _dot(p_loc.astype(BF16), vcat) + _dot(p_ctx.astype(BF16), vctx)
            y_ref[:, hs] = (out / den).astype(y_ref.dtype)


def _swa_latent(swa3, swa_ctx3, tabs, sink, layer):
    bsz, seq, _ = swa3.shape
    t = swa_ctx3.shape[1]
    w = SWA_WINDOW
    tq = 256 if seq % 256 == 0 else w
    r = tq // w
    nblk = seq // w
    kvw = SWA_KV_HEADS * HEAD_DIM
    qw = SWA_HEADS * HEAD_DIM
    k_col, v_col = qw // kvw, qw // kvw + 1
    cos, slo, shi = tabs
    prev = lambda i: jnp.maximum(i * r - 1, 0)
    nxt = lambda i: jnp.minimum((i + 1) * r, nblk - 1)
    tab_specs = ([pl.BlockSpec((tq, HEAD_DIM), lambda b, i: (i, 0))] * 3
                 + [pl.BlockSpec((w, HEAD_DIM), lambda b, i: (prev(i), 0))] * 3
                 + [pl.BlockSpec((w, HEAD_DIM), lambda b, i: (nxt(i), 0))] * 3)
    return pl.pallas_call(
        functools.partial(_swa_kernel, tq=tq, seq=seq),
        grid=(bsz, seq // tq),
        in_specs=[
            pl.BlockSpec((None, tq, qw), lambda b, i: (b, i, 0)),
            pl.BlockSpec((None, tq, kvw), lambda b, i: (b, i, k_col)),
            pl.BlockSpec((None, tq, kvw), lambda b, i: (b, i, v_col)),
            pl.BlockSpec((None, w, kvw), lambda b, i: (b, prev(i), k_col)),
            pl.BlockSpec((None, w, kvw), lambda b, i: (b, prev(i), v_col)),
            pl.BlockSpec((None, w, kvw), lambda b, i: (b, nxt(i), k_col)),
            pl.BlockSpec((None, w, kvw), lambda b, i: (b, nxt(i), v_col)),
            pl.BlockSpec((None, t, kvw), lambda b, i: (b, 0, k_col)),
            pl.BlockSpec((None, t, kvw), lambda b, i: (b, 0, v_col)),
        ] + tab_specs + [
            pl.BlockSpec((None, SWA_HEADS, SUBLANES, LANES), lambda b, i: (layer, 0, 0, 0)),
        ],
        out_specs=pl.BlockSpec((None, tq, qw), lambda b, i: (b, i, 0)),
        out_shape=jax.ShapeDtypeStruct((bsz, seq, qw), BF16),
        name="swa_latent",
        compiler_params=_cparams("parallel", "parallel"),
    )(swa3, swa3, swa3, swa3, swa3, swa3, swa3, swa_ctx3, swa_ctx3,
      cos, slo, shi, cos, slo, shi, cos, slo, shi, sink)


def _mla_proj_kernel(x_ref, qn_ref, kvn_ref, wq_ref, wk_ref, wv_ref, cos_ref, slo_ref, shi_ref,
                     q_ref, k_ref, v_ref):
    nq = MLA_ROPE_DIM // 4
    cos, slo, shi = cos_ref[...], slo_ref[...], shi_ref[...]
    cq = (_rms(x_ref[:, 0:MLA_Q_RANK]) * qn_ref[...]).astype(BF16)
    q = _dot(cq, wq_ref[...]) * (MLA_SCALE * LOG2E)
    ckv =(_rms(x_ref[:, MLA_Q_RANK:MLA_Q_RANK + MLA_KV_RANK]) * kvn_ref[...]).astype(BF16)
    kn = _dot(ckv, wk_ref[...])
    v_ref[...] = _dot(ckv, wv_ref[...]).astype(v_ref.dtype)
    kr = _rope(x_ref[:, MLA_Q_RANK + MLA_KV_RANK:MLA_COLS_PAD], cos, slo, shi, nq).astype(k_ref.dtype)
    for h in range(MLA_HEADS):
        a = h * MLA_QK_PAD
        q_ref[:, a:a + MLA_NOPE_DIM] = q[:, a:a + MLA_NOPE_DIM].astype(q_ref.dtype)
        q_ref[:, a + MLA_NOPE_DIM:a + MLA_QK_PAD] = _rope(
            q[:, a + MLA_NOPE_DIM:a + MLA_QK_PAD], cos, slo, shi, nq).astype(q_ref.dtype)
        k_ref[:, a:a + MLA_NOPE_DIM] = kn[:, h * MLA_NOPE_DIM:(h + 1) * MLA_NOPE_DIM].astype(k_ref.dtype)
        k_ref[:, a + MLA_NOPE_DIM:a + MLA_QK_PAD] = kr


def _mla_project(mla2, seq, qn, kvn, wq, wk, wv, tabs, layer, tm):
    m_rows = mla2.shape[0]
    cos, slo, shi = tabs
    tpb = seq // tm
    qkw = MLA_HEADS * MLA_QK_PAD
    vw = MLA_HEADS * MLA_V_DIM
    lay = lambda i: (layer, 0, 0)
    tab = pl.BlockSpec((tm, LANES), lambda i: (i % tpb, 0))
    return pl.pallas_call(
        _mla_proj_kernel,
        grid=(m_rows // tm,),
        in_specs=[
            pl.BlockSpec((tm, MLA_COLS_PAD), lambda i: (i, 0)),
            pl.BlockSpec((None, 1, MLA_Q_RANK), lay),
            pl.BlockSpec((None, 1, MLA_KV_RANK), lay),
            pl.BlockSpec((None, MLA_Q_RANK, qkw), lay),
            pl.BlockSpec((None, MLA_KV_RANK, vw), lay),
            pl.BlockSpec((None, MLA_KV_RANK, vw), lay),
            tab, tab, tab,
        ],
        out_specs=[
            pl.BlockSpec((tm, qkw), lambda i: (i, 0)),
            pl.BlockSpec((tm, qkw), lambda i: (i, 0)),
            pl.BlockSpec((tm, vw), lambda i: (i, 0)),
        ],
        out_shape=[
            jax.ShapeDtypeStruct((m_rows, qkw), BF16),
            jax.ShapeDtypeStruct((m_rows, qkw), BF16),
            jax.ShapeDtypeStruct((m_rows, vw), BF16),
        ],
        name="mla_project",
        compiler_params=_cparams("parallel"),
    )(mla2, qn, kvn, wq, wk, wv, cos, slo, shi)


def _attn_kernel(*refs, n_src, q_scale, has_sink, nq):
    q_ref = refs[0]
    k_refs = refs[1:1 + n_src]
    v_refs = refs[1 + n_src:1 + 2 * n_src]
    pos = 1 + 2 * n_src
    sink_ref = refs[pos] if has_sink else None
    sink_b_ref = refs[pos + 1] if has_sink else None
    o_ref = refs[pos + (2 if has_sink else 0)]
    scratch = refs[pos + (3 if has_sink else 1):]
    s_refs = scratch[0:n_src]
    va_refs = scratch[n_src:2 * n_src]
    mb_ref = scratch[2 * n_src]
    t = pl.program_id(0)
    tq = q_ref.shape[0]
    dv = o_ref.shape[-1]
    tks = [s_ref.shape[-1] for s_ref in s_refs]

    @pl.when(t == 0)
    def _():
        for s_ref in s_refs:
            s_ref[...] = jnp.zeros(s_ref.shape, F32)
        mb_ref[...] = jnp.zeros(mb_ref.shape, F32)

    @pl.when(jnp.maximum(t - 1, 0) % nq == 0)
    def _():
        for v_ref, va in zip(v_refs, va_refs):
            va[:, 0:dv] = v_ref[...].astype(BF16)
            va[:, dv:2 * dv] = jnp.ones((va.shape[0], dv), BF16)

    q = q_ref[...]
    if q_scale != 1.0:
        q = q.astype(F32) * q_scale
    q = q.astype(BF16)
    m_prev = mb_ref[...]

    def lane_max(m_vec, s):
        for c in range(s.shape[-1] // LANES):
            m_vec = jnp.maximum(m_vec, s[:, c * LANES:(c + 1) * LANES])
        return m_vec

    m_vec = jnp.full((tq, LANES), NEG_INF, F32)
    acc = jnp.zeros((tq, 2 * dv), F32)
    for k_ref, s_ref, va, tkk in zip(k_refs, s_refs, va_refs, tks):
        for j in range(s_ref.shape[0]):
            rows = pl.ds(j * tkk, tkk)
            p = jnp.exp2(s_ref[j] - jnp.concatenate([m_prev] * (tkk // LANES), axis=1))
            acc = acc + _dot(p.astype(BF16), va[rows, :])
            s = _dot_nt(q, k_ref[rows, :].astype(BF16))
            s_ref[j] = s
            m_vec = lane_max(m_vec, s)

    l = acc[:, dv:2 * dv]
    if has_sink:
        l = l + jnp.exp2(sink_b_ref[0:1, 0:1] * LOG2E - m_prev)
    o_ref[...] = (acc[:, 0:dv] / l).astype(o_ref.dtype)

    m_row = jnp.max(m_vec, axis=-1, keepdims=True)
    if has_sink:
        m_row = jnp.maximum(m_row, sink_ref[0:1, 0:1] * LOG2E)
    mb_ref[...] = jnp.broadcast_to(m_row, (tq, LANES))


def _attention(q3, ks, vs, *, n_heads, group, dq, dv, q_col0, k_col0, v_col0, q_scale, sink=None, layer=0,
               tq=512, tk=512):
    bsz, lq, _ = q3.shape
    tq = min(tq, lq)
    assert dv == LANES
    n_src = len(ks)
    tks = [min(tk, k.shape[1]) for k in ks]
    nq = lq // tq
    n_blocks = bsz * n_heads * nq

    def cur(t):
        t = jnp.minimum(t, n_blocks - 1)
        return t // (n_heads * nq), (t // nq) % n_heads, t % nq

    def prev(t):
        return cur(jnp.maximum(t - 1, 0))

    def at(fn, spec):
        def index_map(t):
            b, h, i = fn(t)
            return spec(b, h, i)
        return index_map

    in_specs = [pl.BlockSpec((None, tq, dq), at(cur, lambda b, h, i: (b, i, q_col0 + h)))]
    in_specs += [pl.BlockSpec((None, k.shape[1], dq), at(cur, lambda b, h, i: (b, 0, k_col0 + h // group)))
                 for k in ks]
    in_specs += [pl.BlockSpec((None, v.shape[1], dv), at(prev, lambda b, h, i: (b, 0, v_col0 + h // group)))
                 for v in vs]
    args = [q3] + list(ks) + list(vs)
    if sink is not None:
        for fn in (cur, prev):
            in_specs.append(pl.BlockSpec((None, None, SUBLANES, LANES),
                                         at(fn, lambda b, h, i: (layer, h, 0, 0))))
            args.append(sink)
    scratch = [pltpu.VMEM((k.shape[1] // tkk, tq, tkk), F32) for k, tkk in zip(ks, tks)]
    scratch += [pltpu.VMEM((v.shape[1], 2 * dv), BF16) for v in vs]
    scratch += [pltpu.VMEM((tq, LANES), F32)]
    return pl.pallas_call(
        functools.partial(_attn_kernel, n_src=n_src, q_scale=q_scale, has_sink=sink is not None, nq=nq),
        grid=(n_blocks + 1,),
        in_specs=in_specs,
        out_specs=pl.BlockSpec((None, tq, dv), at(prev, lambda b, h, i: (b, i, h))),
        out_shape=jax.ShapeDtypeStruct((bsz, lq, n_heads * dv), BF16),
        scratch_shapes=scratch,
        name="attention",
        compiler_params=_cparams("arbitrary"),
    )(*args)


def _outproj_kernel(yr_ref, ys_ref, ym_ref, x_ref, mod_ref, w_ref, g_ref, b_ref, o_ref, *, alpha):
    a, b = RET_DIM, RET_DIM + SWA_HEADS * HEAD_DIM
    mix = (_dot(yr_ref[...], w_ref[0:a, :]) + _dot(ys_ref[...], w_ref[a:b, :])
           + _dot(ym_ref[...], w_ref[b:MIX_WIDTH, :]))
    z = alpha * x_ref[...] + (1.0 + mod_ref[2:3, :]) * mix
    o_ref[...] = _layer_norm(z, g_ref[...], b_ref[...])


def _outproj_ln(y_ret, y_swa, y_mla, x2, mods, w_o, ln_g, ln_b, layer, mod_row0, rows_per_mod, tm, alpha):
    m_rows, d = x2.shape
    mod_idx = lambda i: (layer * MOD_ROWS + mod_row0 + (i * tm) // rows_per_mod, 0, 0)
    lay = lambda i: (layer, 0, 0)
    row = lambda w: pl.BlockSpec((tm, w), lambda i: (i, 0))
    return pl.pallas_call(
        functools.partial(_outproj_kernel, alpha=alpha),
        grid=(m_rows // tm,),
        in_specs=[
            row(RET_DIM), row(SWA_HEADS * HEAD_DIM), row(MLA_HEADS * MLA_V_DIM), row(d),
            pl.BlockSpec((None, N_MOD, d), mod_idx),
            _resident((None, MIX_WIDTH, d), lay),
            pl.BlockSpec((None, 1, d), lay),
            pl.BlockSpec((None, 1, d), lay),
        ],
        out_specs=row(d),
        out_shape=jax.ShapeDtypeStruct((m_rows, d), F32),
        name="outproj_ln",
        compiler_params=_cparams("parallel"),
    )(y_ret, y_swa, y_mla, x2, mods, w_o, ln_g, ln_b)


def _ffn_kernel(x_ref, xp_ref, xn_ref, mod_ref, wu_ref, wg_ref, cw_ref, cb_ref, wd_ref, g_ref, b_ref,
                o_ref, h_ext, g_ext, acc, *, tm, seq, alpha):
    i = pl.program_id(0)
    f = pl.program_id(1)

    @pl.when(f == 0)
    def _():
        scale = 1.0 + mod_ref[4:5, :]
        shift = mod_ref[3:4, :]
        has_prev = jnp.where((i * tm) % seq != 0, 1.0, 0.0).astype(F32)
        has_next = jnp.where(((i + 1) * tm) % seq != 0, 1.0, 0.0).astype(F32)
        h_ext[0:HALO, :] = ((xp_ref[...] * scale + shift) * has_prev).astype(BF16)
        h_ext[HALO:HALO + tm, :] = (x_ref[...] * scale + shift).astype(BF16)
        h_ext[HALO + tm:tm + 2 * HALO, :] = ((xn_ref[...] * scale + shift) * has_next).astype(BF16)
        acc[...] = jnp.zeros_like(acc)

    u = _dot(h_ext[HALO:HALO + tm, :], wu_ref[...])
    g_ext[...] = _dot(h_ext[...], wg_ref[...])
    gc = (g_ext[HALO - 1:HALO - 1 + tm, :] * cw_ref[0:1, :] + g_ext[HALO:HALO + tm, :] * cw_ref[1:2, :]
          + g_ext[HALO + 1:HALO + 1 + tm, :] * cw_ref[2:3, :] + cb_ref[...])
    a = (_silu(gc) * u).astype(BF16)
    acc[...] += _dot(a, wd_ref[...])

    @pl.when(f == pl.num_programs(1) - 1)
    def _():
        xa = x_ref[...]
        z = alpha * xa + (1.0 + mod_ref[5:6, :]) * acc[...]
        o_ref[...] = _layer_norm(z, g_ref[...], b_ref[...])


def _ffn(x2, seq, mods, w_up, conv_w, conv_b, w_down, ln_g, ln_b, layer, mod_row0, rows_per_mod, tm, alpha):
    m_rows, d = x2.shape
    dff = w_down.shape[1]
    tf = 512 if dff % 512 == 0 else (256 if dff % 256 == 0 else LANES)
    nf = dff // tf
    hb = tm // HALO
    nhb = m_rows // HALO
    mod_idx = lambda i, f: (layer * MOD_ROWS + mod_row0 + (i * tm) // rows_per_mod, 0, 0)
    lay = lambda i, f: (layer, 0, 0)
    return pl.pallas_call(
        functools.partial(_ffn_kernel, tm=tm, seq=seq, alpha=alpha),
        grid=(m_rows // tm, nf),
        in_specs=[
            pl.BlockSpec((tm, d), lambda i, f: (i, 0)),
            pl.BlockSpec((HALO, d), lambda i, f: (jnp.maximum(i * hb - 1, 0), 0)),
            pl.BlockSpec((HALO, d), lambda i, f: (jnp.minimum((i + 1) * hb, nhb - 1), 0)),
            pl.BlockSpec((None, N_MOD, d), mod_idx),
            pl.BlockSpec((None, d, tf), lambda i, f: (layer, 0, f)),
            pl.BlockSpec((None, d, tf), lambda i, f: (layer, 0, nf + f)),
            pl.BlockSpec((None, 3, tf), lambda i, f: (layer, 0, f)),
            pl.BlockSpec((None, 1, tf), lambda i, f: (layer, 0, f)),
            pl.BlockSpec((None, tf, d), lambda i, f: (layer, f, 0)),
            pl.BlockSpec((None, 1, d), lay),
            pl.BlockSpec((None, 1, d), lay),
        ],
        out_specs=pl.BlockSpec((tm, d), lambda i, f: (i, 0)),
        out_shape=jax.ShapeDtypeStruct((m_rows, d), F32),
        scratch_shapes=[
            pltpu.VMEM((tm + 2 * HALO, d), BF16),
            pltpu.VMEM((tm + 2 * HALO, tf), F32),
            pltpu.VMEM((tm, d), F32),
        ],
        name="conv_ffn",
        compiler_params=_cparams("parallel", "arbitrary"),
    )(x2, x2, x2, mods, w_up, w_up, conv_w, conv_b, w_down, ln_g, ln_b)


def _rope_tables(n_tokens, dim):
    rows = n_tokens // GRID_W
    r = np.repeat(np.arange(rows, dtype=np.float32), GRID_W)
    cc = np.tile(np.arange(GRID_W, dtype=np.float32), rows)
    n_freq = dim // 4
    inv = jnp.asarray(ROPE_THETA, F32) ** (-jnp.arange(n_freq, dtype=F32) / n_freq)
    ang_r = jnp.asarray(r)[:, None] * inv
    ang_c = jnp.asarray(cc)[:, None] * inv
    ang = jnp.concatenate([ang_r, ang_r, ang_c, ang_c], axis=-1)
    cos, sin = jnp.cos(ang), jnp.sin(ang)
    lane = np.arange(dim)
    lo = jnp.asarray((lane % (2 * n_freq)) < n_freq)
    sin_lo = jnp.where(lo, -sin, 0.0)
    sin_hi = jnp.where(lo, 0.0, sin)
    pad = LANES - dim
    if pad:
        cos = jnp.pad(cos, ((0, 0), (0, pad)), constant_values=1.0)
        sin_lo = jnp.pad(sin_lo, ((0, 0), (0, pad)))
        sin_hi = jnp.pad(sin_hi, ((0, 0), (0, pad)))
    return cos, sin_lo, sin_hi


def _identity_tables(n_tokens):
    return (jnp.ones((n_tokens, LANES), F32), jnp.zeros((n_tokens, LANES), F32),
            jnp.zeros((n_tokens, LANES), F32))


def _lane_bcast(p):
    return jnp.broadcast_to(p.astype(F32)[..., None, None], p.shape + (SUBLANES, LANES))


def kernel(x, c, ctx, c_ctx, ada_w, ada_b, w_in, ret_decay_fwd, ret_decay_bwd, swa_sink, mla_q_norm, mla_w_uq,
           mla_kv_norm, mla_w_ukv, w_o, ln1_g, ln1_b, ffn_w_up, ffn_conv_w, ffn_conv_b, ffn_w_down, ln2_g, ln2_b):
    bsz, seq, d = x.shape
    t = ctx.shape[1]
    depth = w_in.shape[0]
    assert bsz + 1 <= MOD_ROWS and seq % RET_CHUNK == 0 and t % RET_CHUNK == 0
    assert seq % GRID_W == 0 and d % LANES == 0
    alpha = (2 * depth) ** 0.25

    ada_w_b = ada_w.astype(BF16)
    w_in_b = jnp.pad(w_in, ((0, 0), (0, 0), (0, IN_WIDTH_PAD - IN_WIDTH))).astype(BF16)
    w_o_b = w_o.astype(BF16)
    w_up_b = ffn_w_up.astype(BF16)
    w_down_b = ffn_w_down.astype(BF16)
    hq = MLA_NOPE_DIM + MLA_ROPE_DIM
    wq = mla_w_uq.reshape(depth, MLA_Q_RANK, MLA_HEADS, hq)
    wq = jnp.pad(wq, ((0, 0), (0, 0), (0, 0), (0, MLA_QK_PAD - hq)))
    wq = wq.reshape(depth, MLA_Q_RANK, MLA_HEADS * MLA_QK_PAD).astype(BF16)
    wkv = mla_w_ukv.reshape(depth, MLA_KV_RANK, MLA_HEADS, MLA_NOPE_DIM + MLA_V_DIM)
    wk = wkv[..., :MLA_NOPE_DIM].reshape(depth, MLA_KV_RANK, MLA_HEADS * MLA_NOPE_DIM).astype(BF16)
    wv = wkv[..., MLA_NOPE_DIM:].reshape(depth, MLA_KV_RANK, MLA_HEADS * MLA_V_DIM).astype(BF16)
    qn = mla_q_norm.reshape(depth, 1, MLA_Q_RANK)
    kvn = mla_kv_norm.reshape(depth, 1, MLA_KV_RANK)
    g1, b1 = ln1_g.reshape(depth, 1, d), ln1_b.reshape(depth, 1, d)
    g2, b2 = ln2_g.reshape(depth, 1, d), ln2_b.reshape(depth, 1, d)
    conv_b = ffn_conv_b.reshape(depth, 1, -1)
    dec = _lane_bcast(jnp.stack([ret_decay_fwd, ret_decay_bwd], axis=1))
    sink = _lane_bcast(swa_sink)

    tabs_h = _rope_tables(seq, HEAD_DIM)
    tabs_m = _rope_tables(seq, MLA_ROPE_DIM)
    tabs_id = _identity_tables(t)

    cond = jnp.zeros((MOD_ROWS, d), F32).at[:bsz].set(c).at[bsz].set(c_ctx)
    mods = _ada_mod(cond, ada_w_b, ada_b).reshape(depth * MOD_ROWS, N_MOD, d)

    tm_x = 512 if seq % 512 == 0 else RET_CHUNK
    tm_c = t if t <= 512 else RET_CHUNK
    zeros_state = jnp.zeros((bsz, RET_HEADS, HEAD_DIM, HEAD_DIM), F32)
    swa_scale = HEAD_DIM ** -0.5

    x2 = x.reshape(bsz * seq, d)
    xc2 = ctx.reshape(bsz * t, d)
    for l in range(depth):
        last = l == depth - 1
        ret_c, swa_c, mla_c = _inproj(xc2, mods, w_in_b, l, bsz, bsz * t, tm_c)
        swa_c3 = swa_c.reshape(bsz, t, SWA_COLS)
        y_ret_c, st_f, st_b = _retention(ret_c.reshape(bsz, t, RET_COLS), tabs_id, dec, l,
                                         zeros_state, zeros_state)
        q_c, k_c, v_c = _mla_project(mla_c, t, qn, kvn, wq, wk, wv, tabs_id, l, tm_c)
        k_c3 = k_c.reshape(bsz, t, -1)
        v_c3 = v_c.reshape(bsz, t, -1)

        ret_x, swa_x, mla_x = _inproj(x2, mods, w_in_b, l, 0, seq, tm_x)
        y_ret, _, _ = _retention(ret_x.reshape(bsz, seq, RET_COLS), tabs_h, dec, l, st_f, st_b)
        y_swa = _swa_latent(swa_x.reshape(bsz, seq, SWA_COLS), swa_c3, tabs_h, sink, l)
        q_x, k_x, v_x = _mla_project(mla_x, seq, qn, kvn, wq, wk, wv, tabs_m, l, tm_x)
        y_mla = _attention(q_x.reshape(bsz, seq, -1), [k_x.reshape(bsz, seq, -1), k_c3],
                           [v_x.reshape(bsz, seq, -1), v_c3], n_heads=MLA_HEADS, group=1,
                           dq=MLA_QK_PAD, dv=MLA_V_DIM, q_col0=0, k_col0=0, v_col0=0, q_scale=1.0)
        x_a = _outproj_ln(y_ret.reshape(bsz * seq, -1), y_swa.reshape(bsz * seq, -1),
                          y_mla.reshape(bsz * seq, -1), x2, mods, w_o_b, g1, b1, l, 0, seq, tm_x, alpha)
        x_new = _ffn(x_a, seq, mods, w_up_b, ffn_conv_w, conv_b, w_down_b, g2, b2, l, 0, seq, tm_x, alpha)

        if not last:
            y_swa_c = _attention(swa_c3, [swa_c3], [swa_c3], n_heads=SWA_HEADS, group=SWA_GROUP,
                                 dq=HEAD_DIM, dv=HEAD_DIM, q_col0=0, k_col0=SWA_HEADS,
                                 v_col0=SWA_HEADS + SWA_KV_HEADS, q_scale=swa_scale * LOG2E, sink=sink, layer=l)
            y_mla_c = _attention(q_c.reshape(bsz, t, -1), [k_c3], [v_c3], n_heads=MLA_HEADS, group=1,
                                 dq=MLA_QK_PAD, dv=MLA_V_DIM, q_col0=0, k_col0=0, v_col0=0, q_scale=1.0)
            xc_a = _outproj_ln(y_ret_c.reshape(bsz * t, -1), y_swa_c.reshape(bsz * t, -1),
                               y_mla_c.reshape(bsz * t, -1), xc2, mods, w_o_b, g1, b1, l, bsz, bsz * t,
                               tm_c, alpha)
            xc2 = _ffn(xc_a, t, mods, w_up_b, ffn_conv_w, conv_b, w_down_b, g2, b2, l, bsz, bsz * t,
                       tm_c, alpha)
        x2 = x_new
    return x2.reshape(bsz, seq, d)
```

```python
import functools

import jax
import jax.numpy as jnp
import numpy as np
from jax import lax
from jax.experimental import pallas as pl
from jax.experimental.pallas import tpu as pltpu

GRID_W = 64
HEAD_DIM = 128
ROPE_THETA = 10000.0
RET_HEADS = 4
RET_DIM = RET_HEADS * HEAD_DIM
RET_CHUNK = 128
SWA_HEADS = 6
SWA_KV_HEADS = 2
SWA_GROUP = SWA_HEADS // SWA_KV_HEADS
SWA_WINDOW = 128
MLA_HEADS = 6
MLA_Q_RANK = 512
MLA_KV_RANK = 256
MLA_NOPE_DIM = 128
MLA_ROPE_DIM = 64
MLA_V_DIM = 128
MLA_SCALE = (MLA_NOPE_DIM + MLA_ROPE_DIM) ** -0.5
MLA_QK_PAD = 256
N_MOD = 6
LN_EPS = 1e-5
RMS_EPS = 1e-6
NEG_INF = -1e30
LOG2E = 1.4426950408889634

RET_COLS = 4 * RET_DIM
SWA_COLS = (SWA_HEADS + 2 * SWA_KV_HEADS) * HEAD_DIM
MLA_COLS = MLA_Q_RANK + MLA_KV_RANK + MLA_ROPE_DIM
MLA_COLS_PAD = 896
IN_WIDTH = RET_COLS + SWA_COLS + MLA_COLS
IN_WIDTH_PAD = RET_COLS + SWA_COLS + MLA_COLS_PAD
MIX_WIDTH = RET_DIM + SWA_HEADS * HEAD_DIM + MLA_HEADS * MLA_V_DIM

LANES = 128
SUBLANES = 8
MOD_ROWS = 8
VMEM_LIMIT = 56 * 1024 * 1024
HALO = 16

BF16 = jnp.bfloat16
F32 = jnp.float32


def _cparams(*sem):
    return pltpu.CompilerParams(dimension_semantics=sem, vmem_limit_bytes=VMEM_LIMIT)


def _resident(block, index_map):
    return pl.BlockSpec(block, index_map, pipeline_mode=pl.Buffered(1))


def _dot(a, b):
    return jnp.dot(a, b, preferred_element_type=F32)


def _dot_nt(a, b):
    return lax.dot_general(a, b, (((1,), (1,)), ((), ())), preferred_element_type=F32)


def _silu(x):
    return x * (1.0 / (1.0 + jnp.exp(-x)))


def _rope(x, cos, sin_lo, sin_hi, nq):
    w = x.shape[-1]
    return x * cos + pltpu.roll(x, w - nq, 1) * sin_lo + pltpu.roll(x, nq, 1) * sin_hi


def _layer_norm(z, g, b):
    mu = jnp.mean(z, axis=-1, keepdims=True)
    zc = z - mu
    var = jnp.mean(zc * zc, axis=-1, keepdims=True)
    return zc * lax.rsqrt(var + LN_EPS) * g + b


def _rms(x):
    return x * lax.rsqrt(jnp.mean(x * x, axis=-1, keepdims=True) + RMS_EPS)


def _log_sigmoid(x):
    return -(jnp.maximum(-x, 0.0) + jnp.log(1.0 + jnp.exp(-jnp.abs(x))))


def _ada_kernel(c_ref, w_ref, b_ref, o_ref):
    sc = _silu(c_ref[...]).astype(BF16)
    o_ref[...] = _dot(sc, w_ref[...]) + b_ref[...]


def _ada_mod(cond, ada_w, ada_b):
    depth, d, n = ada_w.shape
    tn = min(n, 2048)
    return pl.pallas_call(
        _ada_kernel,
        grid=(depth, n // tn),
        in_specs=[
            pl.BlockSpec((MOD_ROWS, d), lambda l, j: (0, 0)),
            pl.BlockSpec((None, d, tn), lambda l, j: (l, 0, j)),
            pl.BlockSpec((None, 1, tn), lambda l, j: (l, 0, j)),
        ],
        out_specs=pl.BlockSpec((None, MOD_ROWS, tn), lambda l, j: (l, 0, j)),
        out_shape=jax.ShapeDtypeStruct((depth, MOD_ROWS, n), F32),
        name="ada_mod",
        compiler_params=_cparams("parallel", "parallel"),
    )(cond, ada_w, ada_b.reshape(depth, 1, n))


def _inproj_kernel(x_ref, mod_ref, w_ref, cos_ref, slo_ref, shi_ref, ret_ref, swa_ref, mla_ref):
    m = mod_ref[...]
    h = (x_ref[...] * (1.0 + m[1:2, :]) + m[0:1, :]).astype(BF16)
    cos, slo, shi = cos_ref[...], slo_ref[...], shi_ref[...]
    nq = HEAD_DIM // 4
    k_scale = HEAD_DIM ** -0.5

    def rope_heads(t, out_ref, col0, n_heads, scale):
        for hh in range(n_heads):
            c = col0 + hh * HEAD_DIM
            r = _rope(t[:, c:c + HEAD_DIM], cos, slo, shi, nq)
            out_ref[:, c:c + HEAD_DIM] = (r if scale is None else r * scale).astype(out_ref.dtype)

    ret = _dot(h, w_ref[:, 0:RET_COLS])
    rope_heads(ret, ret_ref, 0, RET_HEADS, None)
    rope_heads(ret, ret_ref, RET_DIM, RET_HEADS, k_scale)
    ret_ref[:, 2 * RET_DIM:RET_COLS] = ret[:, 2 * RET_DIM:RET_COLS]
    swa = _dot(h, w_ref[:, RET_COLS:RET_COLS + SWA_COLS])
    sq_cols = SWA_HEADS * HEAD_DIM
    sk_cols = SWA_KV_HEADS * HEAD_DIM
    rope_heads(swa, swa_ref, 0, SWA_HEADS, k_scale * LOG2E)
    rope_heads(swa, swa_ref, sq_cols, SWA_KV_HEADS, None)
    swa_ref[:, sq_cols + sk_cols:SWA_COLS] = swa[:, sq_cols + sk_cols:SWA_COLS].astype(swa_ref.dtype)
    mla_ref[...] = _dot(h, w_ref[:, RET_COLS + SWA_COLS:IN_WIDTH_PAD])


def _inproj(x2, seq, mods, w_in, tabs, layer, mod_row0, rows_per_mod, tm):
    m_rows, d = x2.shape
    mod_idx = lambda i: (layer * MOD_ROWS + mod_row0 + (i * tm) // rows_per_mod, 0, 0)
    tpb = seq // tm
    tab = pl.BlockSpec((tm, LANES), lambda i: (i % tpb, 0))
    return pl.pallas_call(
        _inproj_kernel,
        grid=(m_rows // tm,),
        in_specs=[
            pl.BlockSpec((tm, d), lambda i: (i, 0)),
            pl.BlockSpec((None, N_MOD, d), mod_idx),
            _resident((None, d, IN_WIDTH_PAD), lambda i: (layer, 0, 0)),
            tab, tab, tab,
        ],
        out_specs=[
            pl.BlockSpec((tm, RET_COLS), lambda i: (i, 0)),
            pl.BlockSpec((tm, SWA_COLS), lambda i: (i, 0)),
            pl.BlockSpec((tm, MLA_COLS_PAD), lambda i: (i, 0)),
        ],
        out_shape=[
            jax.ShapeDtypeStruct((m_rows, RET_COLS), F32),
            jax.ShapeDtypeStruct((m_rows, SWA_COLS), BF16),
            jax.ShapeDtypeStruct((m_rows, MLA_COLS_PAD), F32),
        ],
        name="inproj",
        compiler_params=_cparams("parallel"),
    )(x2, mods, w_in, *tabs)


def _decay_terms(dec_ref):
    lg = _log_sigmoid(dec_ref[...])
    return lg[0, 0:1, 0:1], lg[1, 0:1, 0:1]


def _ret_bwd_kernel(k_ref, v_ref, dec_ref, s0_ref, sin_ref, sfin_ref, state, *, cpb):
    n = pl.program_id(2)
    c = RET_CHUNK

    @pl.when(n == 0)
    def _():
        state[...] = s0_ref[...]

    _, lg_b = _decay_terms(dec_ref)
    pos = lax.broadcasted_iota(jnp.int32, (c, 1), 0).astype(F32)
    kdec = jnp.exp(lg_b * pos)
    cdec = jnp.exp(lg_b * float(c))
    kv = []
    for ci in range(cpb):
        rows = pl.ds(ci * c, c)
        kv.append(_dot((k_ref[rows, :] * kdec).T.astype(BF16), v_ref[rows, :].astype(BF16)))
    st = state[...]
    for ci in reversed(range(cpb)):
        sin_ref[ci] = st.astype(sin_ref.dtype)
        st = cdec * st + kv[ci]
    state[...] = st

    @pl.when(n == pl.num_programs(2) - 1)
    def _():
        sfin_ref[...] = st


def _ret_fwd_kernel(q_ref, k_ref, v_ref, g_ref, dec_ref, s0_ref, sb_ref, y_ref, sfin_ref, state, *, cpb):
    n = pl.program_id(2)
    c = RET_CHUNK

    @pl.when(n == 0)
    def _():
        state[...] = s0_ref[...]

    lg_f, lg_b = _decay_terms(dec_ref)
    pos = lax.broadcasted_iota(jnp.int32, (c, 1), 0).astype(F32)
    ri = lax.broadcasted_iota(jnp.int32, (c, c), 0)
    cj = lax.broadcasted_iota(jnp.int32, (c, c), 1)
    diff = (ri - cj).astype(F32)
    intra = (jnp.where(diff >= 0, jnp.exp(lg_f * jnp.maximum(diff, 0.0)), 0.0)
             + jnp.where(diff <= 0, jnp.exp(lg_b * jnp.maximum(-diff, 0.0)), 0.0))
    qdec_f = jnp.exp(lg_f * (pos + 1.0))
    qdec_b = jnp.exp(lg_b * (float(c) - pos))
    kdec_f = jnp.exp(lg_f * (float(c) - 1.0 - pos))
    cdec_f = jnp.exp(lg_f * float(c))
    lhs, vs, kv = [], [], []
    for ci in range(cpb):
        rows = pl.ds(ci * c, c)
        q, k = q_ref[rows, :], k_ref[rows, :]
        v = v_ref[rows, :].astype(BF16)
        scores = _dot_nt(q.astype(BF16), k.astype(BF16)) * intra
        lhs.append(jnp.concatenate([scores.astype(BF16), (q * qdec_f).astype(BF16),
                                    (q * qdec_b).astype(BF16)], axis=1))
        vs.append(v)
        kv.append(_dot((k * kdec_f).T.astype(BF16), v))
    st = state[...]
    for ci in range(cpb):
        rows = pl.ds(ci * c, c)
        rhs = jnp.concatenate([vs[ci], st.astype(BF16), sb_ref[ci].astype(BF16)], axis=0)
        out = _dot(lhs[ci], rhs)
        y_ref[rows, :] = (_silu(g_ref[rows, :]) * _rms(out)).astype(y_ref.dtype)
        st = cdec_f * st + kv[ci]
    state[...] = st

    @pl.when(n == pl.num_programs(2) - 1)
    def _():
        sfin_ref[...] = st


def _retention(ret3, dec, layer, s0_f, s0_b):
    bsz, seq, _ = ret3.shape
    c = RET_CHUNK
    cpb = next(n for n in (8, 4, 2, 1) if seq % (n * c) == 0)
    rows = cpb * c
    nb = seq // rows
    h = RET_HEADS
    grid = (bsz, h, nb)
    state_spec = pl.BlockSpec((None, None, HEAD_DIM, HEAD_DIM), lambda b, hh, n: (b, hh, 0, 0))
    dec_spec = pl.BlockSpec((None, 2, None, SUBLANES, LANES), lambda b, hh, n: (layer, 0, hh, 0, 0))

    def col(j, rev):
        if rev:
            return pl.BlockSpec((None, rows, HEAD_DIM), lambda b, hh, n: (b, nb - 1 - n, j * h + hh))
        return pl.BlockSpec((None, rows, HEAD_DIM), lambda b, hh, n: (b, n, j * h + hh))

    sb_in, st_b = pl.pallas_call(
        functools.partial(_ret_bwd_kernel, cpb=cpb),
        grid=grid,
        in_specs=[col(1, True), col(2, True), dec_spec, state_spec],
        out_specs=[
            pl.BlockSpec((None, None, cpb, HEAD_DIM, HEAD_DIM), lambda b, hh, n: (b, hh, nb - 1 - n, 0, 0)),
            state_spec,
        ],
        out_shape=[
            jax.ShapeDtypeStruct((bsz, h, seq // c, HEAD_DIM, HEAD_DIM), BF16),
            jax.ShapeDtypeStruct((bsz, h, HEAD_DIM, HEAD_DIM), F32),
        ],
        scratch_shapes=[pltpu.VMEM((HEAD_DIM, HEAD_DIM), F32)],
        name="ret_bwd",
        compiler_params=_cparams("parallel", "parallel", "arbitrary"),
    )(ret3, ret3, dec, s0_b)

    y, st_f = pl.pallas_call(
        functools.partial(_ret_fwd_kernel, cpb=cpb),
        grid=grid,
        in_specs=[col(0, False), col(1, False), col(2, False), col(3, False), dec_spec, state_spec,
                  pl.BlockSpec((None, None, cpb, HEAD_DIM, HEAD_DIM), lambda b, hh, n: (b, hh, n, 0, 0))],
        out_specs=[
            pl.BlockSpec((None, rows, HEAD_DIM), lambda b, hh, n: (b, n, hh)),
            state_spec,
        ],
        out_shape=[
            jax.ShapeDtypeStruct((bsz, seq, RET_DIM), BF16),
            jax.ShapeDtypeStruct((bsz, h, HEAD_DIM, HEAD_DIM), F32),
        ],
        scratch_shapes=[pltpu.VMEM((HEAD_DIM, HEAD_DIM), F32)],
        name="ret_fwd",
        compiler_params=_cparams("parallel", "parallel", "arbitrary"),
    )(ret3, ret3, ret3, ret3, dec, s0_f, sb_in)
    return y, st_f, st_b


def _swa_kernel(q_ref, k_ref, v_ref, kp_ref, vp_ref, kn_ref, vn_ref, kc_ref, vc_ref, sink_ref, y_ref,
                *, tq, seq):
    i = pl.program_id(1)
    w = SWA_WINDOW
    nloc = tq + 2 * w
    q_pos = i * tq + lax.broadcasted_iota(jnp.int32, (tq, nloc), 0)
    k_pos = i * tq - w + lax.broadcasted_iota(jnp.int32, (tq, nloc), 1)
    valid = (jnp.abs(k_pos - q_pos) <= w) & (k_pos >= 0) & (k_pos < seq)
    nk = nloc + kc_ref.shape[0]
    ones = jnp.ones((nk, HEAD_DIM), BF16)
    for kv in range(SWA_KV_HEADS):
        ks = pl.ds(kv * HEAD_DIM, HEAD_DIM)
        kall = jnp.concatenate([kp_ref[:, ks], k_ref[:, ks], kn_ref[:, ks], kc_ref[:, ks]], axis=0)
        vall = jnp.concatenate(
            [jnp.concatenate([vp_ref[:, ks], v_ref[:, ks], vn_ref[:, ks], vc_ref[:, ks]], axis=0), ones], axis=1)
        for g in range(SWA_GROUP):
            hq = kv * SWA_GROUP + g
            hs = pl.ds(hq * HEAD_DIM, HEAD_DIM)
            s = _dot_nt(q_ref[:, hs], kall)
            s = jnp.concatenate([jnp.where(valid, s[:, 0:nloc], NEG_INF), s[:, nloc:]], axis=1)
            sink = sink_ref[hq, 0:1, 0:1] * LOG2E
            m = jnp.maximum(jnp.max(s, axis=-1, keepdims=True), sink)
            acc = _dot(jnp.exp2(s - m).astype(BF16), vall)
            den = acc[:, HEAD_DIM:2 * HEAD_DIM] + jnp.exp2(sink - m)
            y_ref[:, hs] = (acc[:, 0:HEAD_DIM] / den).astype(y_ref.dtype)


def _swa_latent(swa3, swa_ctx3, sink, layer):
    bsz, seq, _ = swa3.shape
    t = swa_ctx3.shape[1]
    w = SWA_WINDOW
    tq = 256 if seq % 256 == 0 else w
    r = tq // w
    nblk = seq // w
    kvw = SWA_KV_HEADS * HEAD_DIM
    qw = SWA_HEADS * HEAD_DIM
    k_col, v_col = qw // kvw, qw // kvw + 1
    prev = lambda i: jnp.maximum(i * r - 1, 0)
    nxt = lambda i: jnp.minimum((i + 1) * r, nblk - 1)
    return pl.pallas_call(
        functools.partial(_swa_kernel, tq=tq, seq=seq),
        grid=(bsz, seq // tq),
        in_specs=[
            pl.BlockSpec((None, tq, qw), lambda b, i: (b, i, 0)),
            pl.BlockSpec((None, tq, kvw), lambda b, i: (b, i, k_col)),
            pl.BlockSpec((None, tq, kvw), lambda b, i: (b, i, v_col)),
            pl.BlockSpec((None, w, kvw), lambda b, i: (b, prev(i), k_col)),
            pl.BlockSpec((None, w, kvw), lambda b, i: (b, prev(i), v_col)),
            pl.BlockSpec((None, w, kvw), lambda b, i: (b, nxt(i), k_col)),
            pl.BlockSpec((None, w, kvw), lambda b, i: (b, nxt(i), v_col)),
            pl.BlockSpec((None, t, kvw), lambda b, i: (b, 0, k_col)),
            pl.BlockSpec((None, t, kvw), lambda b, i: (b, 0, v_col)),
            pl.BlockSpec((None, SWA_HEADS, SUBLANES, LANES), lambda b, i: (layer, 0, 0, 0)),
        ],
        out_specs=pl.BlockSpec((None, tq, qw), lambda b, i: (b, i, 0)),
        out_shape=jax.ShapeDtypeStruct((bsz, seq, qw), BF16),
        name="swa_latent",
        compiler_params=_cparams("parallel", "parallel"),
    )(swa3, swa3, swa3, swa3, swa3, swa3, swa3, swa_ctx3, swa_ctx3, sink)


def _mla_proj_kernel(x_ref, qn_ref, kvn_ref, wq_ref, wk_ref, wv_ref, cos_ref, slo_ref, shi_ref,
                     q_ref, k_ref, v_ref):
    nq = MLA_ROPE_DIM // 4
    cos, slo, shi = cos_ref[...], slo_ref[...], shi_ref[...]
    cq = (_rms(x_ref[:, 0:MLA_Q_RANK]) * qn_ref[...]).astype(BF16)
    q = _dot(cq, wq_ref[...]) * (MLA_SCALE * LOG2E)
    ckv =(_rms(x_ref[:, MLA_Q_RANK:MLA_Q_RANK + MLA_KV_RANK]) * kvn_ref[...]).astype(BF16)
    kn = _dot(ckv, wk_ref[...])
    v_ref[...] = _dot(ckv, wv_ref[...]).astype(v_ref.dtype)
    kr = _rope(x_ref[:, MLA_Q_RANK + MLA_KV_RANK:MLA_COLS_PAD], cos, slo, shi, nq).astype(k_ref.dtype)
    for h in range(MLA_HEADS):
        a = h * MLA_QK_PAD
        q_ref[:, a:a + MLA_NOPE_DIM] = q[:, a:a + MLA_NOPE_DIM].astype(q_ref.dtype)
        q_ref[:, a + MLA_NOPE_DIM:a + MLA_QK_PAD] = _rope(
            q[:, a + MLA_NOPE_DIM:a + MLA_QK_PAD], cos, slo, shi, nq).astype(q_ref.dtype)
        k_ref[:, a:a + MLA_NOPE_DIM] = kn[:, h * MLA_NOPE_DIM:(h + 1) * MLA_NOPE_DIM].astype(k_ref.dtype)
        k_ref[:, a + MLA_NOPE_DIM:a + MLA_QK_PAD] = kr


def _mla_project(mla2, seq, qn, kvn, wq, wk, wv, tabs, layer, tm):
    m_rows = mla2.shape[0]
    cos, slo, shi = tabs
    tpb = seq // tm
    qkw = MLA_HEADS * MLA_QK_PAD
    vw = MLA_HEADS * MLA_V_DIM
    lay = lambda i: (layer, 0, 0)
    tab = pl.BlockSpec((tm, LANES), lambda i: (i % tpb, 0))
    return pl.pallas_call(
        _mla_proj_kernel,
        grid=(m_rows // tm,),
        in_specs=[
            pl.BlockSpec((tm, MLA_COLS_PAD), lambda i: (i, 0)),
            pl.BlockSpec((None, 1, MLA_Q_RANK), lay),
            pl.BlockSpec((None, 1, MLA_KV_RANK), lay),
            pl.BlockSpec((None, MLA_Q_RANK, qkw), lay),
            pl.BlockSpec((None, MLA_KV_RANK, vw), lay),
            pl.BlockSpec((None, MLA_KV_RANK, vw), lay),
            tab, tab, tab,
        ],
        out_specs=[
            pl.BlockSpec((tm, qkw), lambda i: (i, 0)),
            pl.BlockSpec((tm, qkw), lambda i: (i, 0)),
            pl.BlockSpec((tm, vw), lambda i: (i, 0)),
        ],
        out_shape=[
            jax.ShapeDtypeStruct((m_rows, qkw), BF16),
            jax.ShapeDtypeStruct((m_rows, qkw), BF16),
            jax.ShapeDtypeStruct((m_rows, vw), BF16),
        ],
        name="mla_project",
        compiler_params=_cparams("parallel"),
    )(mla2, qn, kvn, wq, wk, wv, cos, slo, shi)


def _attn_kernel(*refs, n_src, q_scale, has_sink, nq):
    q_ref = refs[0]
    k_refs = refs[1:1 + n_src]
    v_refs = refs[1 + n_src:1 + 2 * n_src]
    pos = 1 + 2 * n_src
    sink_ref = refs[pos] if has_sink else None
    sink_b_ref = refs[pos + 1] if has_sink else None
    o_ref = refs[pos + (2 if has_sink else 0)]
    scratch = refs[pos + (3 if has_sink else 1):]
    s_refs = scratch[0:n_src]
    va_refs = scratch[n_src:2 * n_src]
    mb_ref = scratch[2 * n_src]
    t = pl.program_id(0)
    tq = q_ref.shape[0]
    dv = o_ref.shape[-1]
    tks = [s_ref.shape[-1] for s_ref in s_refs]

    @pl.when(t == 0)
    def _():
        for s_ref in s_refs:
            s_ref[...] = jnp.zeros(s_ref.shape, F32)
        mb_ref[...] = jnp.zeros(mb_ref.shape, F32)

    @pl.when(jnp.maximum(t - 1, 0) % nq == 0)
    def _():
        for v_ref, va in zip(v_refs, va_refs):
            va[:, 0:dv] = v_ref[...].astype(BF16)
            va[:, dv:2 * dv] = jnp.ones((va.shape[0], dv), BF16)

    q = q_ref[...]
    if q_scale != 1.0:
        q = q.astype(F32) * q_scale
    q = q.astype(BF16)
    m_prev = mb_ref[...]

    def lane_max(m_vec, s):
        for c in range(s.shape[-1] // LANES):
            m_vec = jnp.maximum(m_vec, s[:, c * LANES:(c + 1) * LANES])
        return m_vec

    m_vec = jnp.full((tq, LANES), NEG_INF, F32)
    acc = jnp.zeros((tq, 2 * dv), F32)
    for k_ref, s_ref, va, tkk in zip(k_refs, s_refs, va_refs, tks):
        for j in range(s_ref.shape[0]):
            rows = pl.ds(j * tkk, tkk)
            p = jnp.exp2(s_ref[j] - jnp.concatenate([m_prev] * (tkk // LANES), axis=1))
            acc = acc + _dot(p.astype(BF16), va[rows, :])
            s = _dot_nt(q, k_ref[rows, :].astype(BF16))
            s_ref[j] = s
            m_vec = lane_max(m_vec, s)

    l = acc[:, dv:2 * dv]
    if has_sink:
        l = l + jnp.exp2(sink_b_ref[0:1, 0:1] * LOG2E - m_prev)
    o_ref[...] = (acc[:, 0:dv] / l).astype(o_ref.dtype)

    m_row = jnp.max(m_vec, axis=-1, keepdims=True)
    if has_sink:
        m_row = jnp.maximum(m_row, sink_ref[0:1, 0:1] * LOG2E)
    mb_ref[...] = jnp.broadcast_to(m_row, (tq, LANES))


def _attention(q3, ks, vs, *, n_heads, group, dq, dv, q_col0, k_col0, v_col0, q_scale, sink=None, layer=0,
               tq=512, tk=512):
    bsz, lq, _ = q3.shape
    tq = min(tq, lq)
    assert dv == LANES
    n_src = len(ks)
    tks = [min(tk, k.shape[1]) for k in ks]
    nq = lq // tq
    n_blocks = bsz * n_heads * nq

    def cur(t):
        t = jnp.minimum(t, n_blocks - 1)
        return t // (n_heads * nq), (t // nq) % n_heads, t % nq

    def prev(t):
        return cur(jnp.maximum(t - 1, 0))

    def at(fn, spec):
        def index_map(t):
            b, h, i = fn(t)
            return spec(b, h, i)
        return index_map

    in_specs = [pl.BlockSpec((None, tq, dq), at(cur, lambda b, h, i: (b, i, q_col0 + h)))]
    in_specs += [pl.BlockSpec((None, k.shape[1], dq), at(cur, lambda b, h, i: (b, 0, k_col0 + h // group)))
                 for k in ks]
    in_specs += [pl.BlockSpec((None, v.shape[1], dv), at(prev, lambda b, h, i: (b, 0, v_col0 + h // group)))
                 for v in vs]
    args = [q3] + list(ks) + list(vs)
    if sink is not None:
        for fn in (cur, prev):
            in_specs.append(pl.BlockSpec((None, None, SUBLANES, LANES),
                                         at(fn, lambda b, h, i: (layer, h, 0, 0))))
            args.append(sink)
    scratch = [pltpu.VMEM((k.shape[1] // tkk, tq, tkk), F32) for k, tkk in zip(ks, tks)]
    scratch += [pltpu.VMEM((v.shape[1], 2 * dv), BF16) for v in vs]
    scratch += [pltpu.VMEM((tq, LANES), F32)]
    return pl.pallas_call(
        functools.partial(_attn_kernel, n_src=n_src, q_scale=q_scale, has_sink=sink is not None, nq=nq),
        grid=(n_blocks + 1,),
        in_specs=in_specs,
        out_specs=pl.BlockSpec((None, tq, dv), at(prev, lambda b, h, i: (b, i, h))),
        out_shape=jax.ShapeDtypeStruct((bsz, lq, n_heads * dv), BF16),
        scratch_shapes=scratch,
        name="attention",
        compiler_params=_cparams("arbitrary"),
    )(*args)


def _outproj_kernel(yr_ref, ys_ref, ym_ref, x_ref, mod_ref, w_ref, g_ref, b_ref, o_ref, *, alpha):
    a, b = RET_DIM, RET_DIM + SWA_HEADS * HEAD_DIM
    mix = (_dot(yr_ref[...], w_ref[0:a, :]) + _dot(ys_ref[...], w_ref[a:b, :])
           + _dot(ym_ref[...], w_ref[b:MIX_WIDTH, :]))
    z = alpha * x_ref[...] + (1.0 + mod_ref[2:3, :]) * mix
    o_ref[...] = _layer_norm(z, g_ref[...], b_ref[...])


def _outproj_ln(y_ret, y_swa, y_mla, x2, mods, w_o, ln_g, ln_b, layer, mod_row0, rows_per_mod, tm, alpha):
    m_rows, d = x2.shape
    mod_idx = lambda i: (layer * MOD_ROWS + mod_row0 + (i * tm) // rows_per_mod, 0, 0)
    lay = lambda i: (layer, 0, 0)
    row = lambda w: pl.BlockSpec((tm, w), lambda i: (i, 0))
    return pl.pallas_call(
        functools.partial(_outproj_kernel, alpha=alpha),
        grid=(m_rows // tm,),
        in_specs=[
            row(RET_DIM), row(SWA_HEADS * HEAD_DIM), row(MLA_HEADS * MLA_V_DIM), row(d),
            pl.BlockSpec((None, N_MOD, d), mod_idx),
            _resident((None, MIX_WIDTH, d), lay),
            pl.BlockSpec((None, 1, d), lay),
            pl.BlockSpec((None, 1, d), lay),
        ],
        out_specs=row(d),
        out_shape=jax.ShapeDtypeStruct((m_rows, d), F32),
        name="outproj_ln",
        compiler_params=_cparams("parallel"),
    )(y_ret, y_swa, y_mla, x2, mods, w_o, ln_g, ln_b)


def _ffn_kernel(x_ref, xp_ref, xn_ref, mod_ref, wu_ref, wg_ref, cw_ref, cb_ref, wd_ref, g_ref, b_ref,
                o_ref, h_ext, g_ext, acc, *, tm, seq, alpha):
    i = pl.program_id(0)
    f = pl.program_id(1)

    @pl.when(f == 0)
    def _():
        scale = 1.0 + mod_ref[4:5, :]
        shift = mod_ref[3:4, :]
        has_prev = jnp.where((i * tm) % seq != 0, 1.0, 0.0).astype(F32)
        has_next = jnp.where(((i + 1) * tm) % seq != 0, 1.0, 0.0).astype(F32)
        h_ext[0:HALO, :] = ((xp_ref[...] * scale + shift) * has_prev).astype(BF16)
        h_ext[HALO:HALO + tm, :] = (x_ref[...] * scale + shift).astype(BF16)
        h_ext[HALO + tm:tm + 2 * HALO, :] = ((xn_ref[...] * scale + shift) * has_next).astype(BF16)
        acc[...] = jnp.zeros_like(acc)

    g_ext[...] = _dot(h_ext[...], wg_ref[...])
    u = _dot(h_ext[HALO:HALO + tm, :], wu_ref[...])
    gc =(g_ext[HALO - 1:HALO - 1 + tm, :] * cw_ref[0:1, :] + g_ext[HALO:HALO + tm, :] * cw_ref[1:2, :]
          + g_ext[HALO + 1:HALO + 1 + tm, :] * cw_ref[2:3, :] + cb_ref[...])
    a = (_silu(gc) * u).astype(BF16)
    acc[...] += _dot(a, wd_ref[...])

    @pl.when(f == pl.num_programs(1) - 1)
    def _():
        xa = x_ref[...]
        z = alpha * xa + (1.0 + mod_ref[5:6, :]) * acc[...]
        o_ref[...] = _layer_norm(z, g_ref[...], b_ref[...])


def _ffn(x2, seq, mods, w_up, conv_w, conv_b, w_down, ln_g, ln_b, layer, mod_row0, rows_per_mod, tm, alpha):
    m_rows, d = x2.shape
    dff = w_down.shape[1]
    tf = 512 if dff % 512 == 0 else (256 if dff % 256 == 0 else LANES)
    nf = dff // tf
    hb = tm // HALO
    nhb = m_rows // HALO
    mod_idx = lambda i, f: (layer * MOD_ROWS + mod_row0 + (i * tm) // rows_per_mod, 0, 0)
    lay = lambda i, f: (layer, 0, 0)
    return pl.pallas_call(
        functools.partial(_ffn_kernel, tm=tm, seq=seq, alpha=alpha),
        grid=(m_rows // tm, nf),
        in_specs=[
            pl.BlockSpec((tm, d), lambda i, f: (i, 0)),
            pl.BlockSpec((HALO, d), lambda i, f: (jnp.maximum(i * hb - 1, 0), 0)),
            pl.BlockSpec((HALO, d), lambda i, f: (jnp.minimum((i + 1) * hb, nhb - 1), 0)),
            pl.BlockSpec((None, N_MOD, d), mod_idx),
            pl.BlockSpec((None, d, tf), lambda i, f: (layer, 0, f)),
            pl.BlockSpec((None, d, tf), lambda i, f: (layer, 0, nf + f)),
            pl.BlockSpec((None, 3, tf), lambda i, f: (layer, 0, f)),
            pl.BlockSpec((None, 1, tf), lambda i, f: (layer, 0, f)),
            pl.BlockSpec((None, tf, d), lambda i, f: (layer, f, 0)),
            pl.BlockSpec((None, 1, d), lay),
            pl.BlockSpec((None, 1, d), lay),
        ],
        out_specs=pl.BlockSpec((tm, d), lambda i, f: (i, 0)),
        out_shape=jax.ShapeDtypeStruct((m_rows, d), F32),
        scratch_shapes=[
            pltpu.VMEM((tm + 2 * HALO, d), BF16),
            pltpu.VMEM((tm + 2 * HALO, tf), F32),
            pltpu.VMEM((tm, d), F32),
        ],
        name="conv_ffn",
        compiler_params=_cparams("parallel", "arbitrary"),
    )(x2, x2, x2, mods, w_up, w_up, conv_w, conv_b, w_down, ln_g, ln_b)


def _rope_tables(n_tokens, dim):
    rows = n_tokens // GRID_W
    r = np.repeat(np.arange(rows, dtype=np.float32), GRID_W)
    cc = np.tile(np.arange(GRID_W, dtype=np.float32), rows)
    n_freq = dim // 4
    inv = jnp.asarray(ROPE_THETA, F32) ** (-jnp.arange(n_freq, dtype=F32) / n_freq)
    ang_r = jnp.asarray(r)[:, None] * inv
    ang_c = jnp.asarray(cc)[:, None] * inv
    ang = jnp.concatenate([ang_r, ang_r, ang_c, ang_c], axis=-1)
    cos, sin = jnp.cos(ang), jnp.sin(ang)
    lane = np.arange(dim)
    lo = jnp.asarray((lane % (2 * n_freq)) < n_freq)
    sin_lo = jnp.where(lo, -sin, 0.0)
    sin_hi = jnp.where(lo, 0.0, sin)
    pad = LANES - dim
    if pad:
        cos = jnp.pad(cos, ((0, 0), (0, pad)), constant_values=1.0)
        sin_lo = jnp.pad(sin_lo, ((0, 0), (0, pad)))
        sin_hi = jnp.pad(sin_hi, ((0, 0), (0, pad)))
    return cos, sin_lo, sin_hi


def _identity_tables(n_tokens):
    return (jnp.ones((n_tokens, LANES), F32), jnp.zeros((n_tokens, LANES), F32),
            jnp.zeros((n_tokens, LANES), F32))


def _lane_bcast(p):
    return jnp.broadcast_to(p.astype(F32)[..., None, None], p.shape + (SUBLANES, LANES))


def kernel(x, c, ctx, c_ctx, ada_w, ada_b, w_in, ret_decay_fwd, ret_decay_bwd, swa_sink, mla_q_norm, mla_w_uq,
           mla_kv_norm, mla_w_ukv, w_o, ln1_g, ln1_b, ffn_w_up, ffn_conv_w, ffn_conv_b, ffn_w_down, ln2_g, ln2_b):
    bsz, seq, d = x.shape
    t = ctx.shape[1]
    depth = w_in.shape[0]
    assert bsz + 1 <= MOD_ROWS and seq % RET_CHUNK == 0 and t % RET_CHUNK == 0
    assert seq % GRID_W == 0 and d % LANES == 0
    alpha = (2 * depth) ** 0.25

    ada_w_b = ada_w.astype(BF16)
    w_in_b = jnp.pad(w_in, ((0, 0), (0, 0), (0, IN_WIDTH_PAD - IN_WIDTH))).astype(BF16)
    w_o_b = w_o.astype(BF16)
    w_up_b = ffn_w_up.astype(BF16)
    w_down_b = ffn_w_down.astype(BF16)
    hq = MLA_NOPE_DIM + MLA_ROPE_DIM
    wq = mla_w_uq.reshape(depth, MLA_Q_RANK, MLA_HEADS, hq)
    wq = jnp.pad(wq, ((0, 0), (0, 0), (0, 0), (0, MLA_QK_PAD - hq)))
    wq = wq.reshape(depth, MLA_Q_RANK, MLA_HEADS * MLA_QK_PAD).astype(BF16)
    wkv = mla_w_ukv.reshape(depth, MLA_KV_RANK, MLA_HEADS, MLA_NOPE_DIM + MLA_V_DIM)
    wk = wkv[..., :MLA_NOPE_DIM].reshape(depth, MLA_KV_RANK, MLA_HEADS * MLA_NOPE_DIM).astype(BF16)
    wv = wkv[..., MLA_NOPE_DIM:].reshape(depth, MLA_KV_RANK, MLA_HEADS * MLA_V_DIM).astype(BF16)
    qn = mla_q_norm.reshape(depth, 1, MLA_Q_RANK)
    kvn = mla_kv_norm.reshape(depth, 1, MLA_KV_RANK)
    g1, b1 = ln1_g.reshape(depth, 1, d), ln1_b.reshape(depth, 1, d)
    g2, b2 = ln2_g.reshape(depth, 1, d), ln2_b.reshape(depth, 1, d)
    conv_b = ffn_conv_b.reshape(depth, 1, -1)
    dec = _lane_bcast(jnp.stack([ret_decay_fwd, ret_decay_bwd], axis=1))
    sink = _lane_bcast(swa_sink)

    tabs_h = _rope_tables(seq, HEAD_DIM)
    tabs_m = _rope_tables(seq, MLA_ROPE_DIM)
    tabs_id = _identity_tables(t)

    cond = jnp.zeros((MOD_ROWS, d), F32).at[:bsz].set(c).at[bsz].set(c_ctx)
    mods = _ada_mod(cond, ada_w_b, ada_b).reshape(depth * MOD_ROWS, N_MOD, d)

    tm_x = 512 if seq % 512 == 0 else RET_CHUNK
    tm_c = t if t <= 512 else RET_CHUNK
    zeros_state = jnp.zeros((bsz, RET_HEADS, HEAD_DIM, HEAD_DIM), F32)

    x2 = x.reshape(bsz * seq, d)
    xc2 = ctx.reshape(bsz * t, d)
    for l in range(depth):
        last = l == depth - 1
        ret_c, swa_c, mla_c = _inproj(xc2, t, mods, w_in_b, tabs_id, l, bsz, bsz * t, tm_c)
        swa_c3 = swa_c.reshape(bsz, t, SWA_COLS)
        y_ret_c, st_f, st_b = _retention(ret_c.reshape(bsz, t, RET_COLS), dec, l, zeros_state, zeros_state)
        q_c, k_c, v_c = _mla_project(mla_c, t, qn, kvn, wq, wk, wv, tabs_id, l, tm_c)
        k_c3 = k_c.reshape(bsz, t, -1)
        v_c3 = v_c.reshape(bsz, t, -1)

        ret_x, swa_x, mla_x = _inproj(x2, seq, mods, w_in_b, tabs_h, l, 0, seq, tm_x)
        y_ret, _, _ = _retention(ret_x.reshape(bsz, seq, RET_COLS), dec, l, st_f, st_b)
        y_swa = _swa_latent(swa_x.reshape(bsz, seq, SWA_COLS), swa_c3, sink, l)
        q_x, k_x, v_x = _mla_project(mla_x, seq, qn, kvn, wq, wk, wv, tabs_m, l, tm_x)
        y_mla = _attention(q_x.reshape(bsz, seq, -1), [k_x.reshape(bsz, seq, -1), k_c3],
                           [v_x.reshape(bsz, seq, -1), v_c3], n_heads=MLA_HEADS, group=1,
                           dq=MLA_QK_PAD, dv=MLA_V_DIM, q_col0=0, k_col0=0, v_col0=0, q_scale=1.0)
        x_a = _outproj_ln(y_ret.reshape(bsz * seq, -1), y_swa.reshape(bsz * seq, -1),
                          y_mla.reshape(bsz * seq, -1), x2, mods, w_o_b, g1, b1, l, 0, seq, tm_x, alpha)
        x_new = _ffn(x_a, seq, mods, w_up_b, ffn_conv_w, conv_b, w_down_b, g2, b2, l, 0, seq, tm_x, alpha)

        if not last:
            y_swa_c = _attention(swa_c3, [swa_c3], [swa_c3], n_heads=SWA_HEADS, group=SWA_GROUP,
                                 dq=HEAD_DIM, dv=HEAD_DIM, q_col0=0, k_col0=SWA_HEADS,
                                 v_col0=SWA_HEADS + SWA_KV_HEADS, q_scale=1.0, sink=sink, layer=l)
            y_mla_c = _attention(q_c.reshape(bsz, t, -1), [k_c3], [v_c3], n_heads=MLA_HEADS, group=1,
                                 dq=MLA_QK_PAD, dv=MLA_V_DIM, q_col0=0, k_col0=0, v_col0=0, q_scale=1.0)
            xc_a = _outproj_ln(y_ret_c.reshape(bsz * t, -1), y_swa_c.reshape(bsz * t, -1),
                               y_mla_c.reshape(bsz * t, -1), xc2, mods, w_o_b, g1, b1, l, bsz, bsz * t,
                               tm_c, alpha)
            xc2 = _ffn(xc_a, t, mods, w_up_b, ffn_conv_w, conv_b, w_down_b, g2, b2, l, bsz, bsz * t,
                       tm_c, alpha)
        x2 = x_new
    return x2.reshape(bsz, seq, d)
```

```python
import functools

import jax
import jax.numpy as jnp
import numpy as np
from jax import lax
from jax.experimental import pallas as pl
from jax.experimental.pallas import tpu as pltpu

GRID_W = 64
HEAD_DIM = 128
ROPE_THETA = 10000.0
RET_HEADS = 4
RET_DIM = RET_HEADS * HEAD_DIM
RET_CHUNK = 128
SWA_HEADS = 6
SWA_KV_HEADS = 2
SWA_GROUP = SWA_HEADS // SWA_KV_HEADS
SWA_WINDOW = 128
MLA_HEADS = 6
MLA_Q_RANK = 512
MLA_KV_RANK = 256
MLA_NOPE_DIM = 128
MLA_ROPE_DIM = 64
MLA_V_DIM = 128
MLA_SCALE = (MLA_NOPE_DIM + MLA_ROPE_DIM) ** -0.5
MLA_QK_PAD = 256
N_MOD = 6
LN_EPS = 1e-5
RMS_EPS = 1e-6
NEG_INF = -1e30
LOG2E = 1.4426950408889634

RET_COLS = 4 * RET_DIM
SWA_COLS = (SWA_HEADS + 2 * SWA_KV_HEADS) * HEAD_DIM
MLA_COLS = MLA_Q_RANK + MLA_KV_RANK + MLA_ROPE_DIM
MLA_COLS_PAD = 896
IN_WIDTH = RET_COLS + SWA_COLS + MLA_COLS
IN_WIDTH_PAD = RET_COLS + SWA_COLS + MLA_COLS_PAD
MIX_WIDTH = RET_DIM + SWA_HEADS * HEAD_DIM + MLA_HEADS * MLA_V_DIM

LANES = 128
SUBLANES = 8
MOD_ROWS = 8
VMEM_LIMIT = 56 * 1024 * 1024
HALO = 16

BF16 = jnp.bfloat16
F32 = jnp.float32


def _cparams(*sem):
    return pltpu.CompilerParams(dimension_semantics=sem, vmem_limit_bytes=VMEM_LIMIT)


def _resident(block, index_map):
    return pl.BlockSpec(block, index_map, pipeline_mode=pl.Buffered(1))


def _dot(a, b):
    return jnp.dot(a, b, preferred_element_type=F32)


def _dot_nt(a, b):
    return lax.dot_general(a, b, (((1,), (1,)), ((), ())), preferred_element_type=F32)


def _silu(x):
    return x * (1.0 / (1.0 + jnp.exp(-x)))


def _rope(x, cos, sin_lo, sin_hi, nq):
    w = x.shape[-1]
    return x * cos + pltpu.roll(x, w - nq, 1) * sin_lo + pltpu.roll(x, nq, 1) * sin_hi


def _layer_norm(z, g, b):
    mu = jnp.mean(z, axis=-1, keepdims=True)
    zc = z - mu
    var = jnp.mean(zc * zc, axis=-1, keepdims=True)
    return zc * lax.rsqrt(var + LN_EPS) * g + b


def _rms(x):
    return x * lax.rsqrt(jnp.mean(x * x, axis=-1, keepdims=True) + RMS_EPS)


def _log_sigmoid(x):
    return -(jnp.maximum(-x, 0.0) + jnp.log(1.0 + jnp.exp(-jnp.abs(x))))


def _ada_kernel(c_ref, w_ref, b_ref, o_ref):
    sc = _silu(c_ref[...]).astype(BF16)
    o_ref[...] = _dot(sc, w_ref[...].astype(BF16)) + b_ref[...]


def _ada_mod(cond, ada_w, ada_b):
    depth, d, n = ada_w.shape
    tn = next(c for c in (1024, 512, 256, LANES) if n % c == 0)
    return pl.pallas_call(
        _ada_kernel,
        grid=(depth, n // tn),
        in_specs=[
            pl.BlockSpec((MOD_ROWS, d), lambda l, j: (0, 0)),
            pl.BlockSpec((None, d, tn), lambda l, j: (l, 0, j)),
            pl.BlockSpec((None, 1, tn), lambda l, j: (l, 0, j)),
        ],
        out_specs=pl.BlockSpec((None, MOD_ROWS, tn), lambda l, j: (l, 0, j)),
        out_shape=jax.ShapeDtypeStruct((depth, MOD_ROWS, n), F32),
        name="ada_mod",
        compiler_params=_cparams("parallel", "parallel"),
    )(cond, ada_w, ada_b.reshape(depth, 1, n))


def _inproj_kernel(x_ref, mod_ref, w_ref, cos_ref, slo_ref, shi_ref, ret_ref, swa_ref, mla_ref):
    m = mod_ref[...]
    h = (x_ref[...] * (1.0 + m[1:2, :]) + m[0:1, :]).astype(BF16)
    cos, slo, shi = cos_ref[...], slo_ref[...], shi_ref[...]
    nq = HEAD_DIM // 4
    k_scale = HEAD_DIM ** -0.5

    def rope(t, c, scale):
        r = _rope(t[:, c:c + HEAD_DIM], cos, slo, shi, nq)
        return r if scale is None else r * scale

    def rope_heads(t, out_ref, col0, n_heads, scale):
        for hh in range(n_heads):
            c = col0 + hh * HEAD_DIM
            out_ref[:, c:c + HEAD_DIM] = rope(t, c, scale).astype(out_ref.dtype)

    ret = _dot(h, w_ref[:, 0:RET_COLS])
    for j in range(RET_COLS // HEAD_DIM):
        c = j * HEAD_DIM
        if j < RET_HEADS:
            ret_ref[j] = rope(ret, c, None)
        elif j < 2 * RET_HEADS:
            ret_ref[j] = rope(ret, c, k_scale)
        else:
            ret_ref[j] = ret[:, c:c + HEAD_DIM]
    swa = _dot(h, w_ref[:, RET_COLS:RET_COLS + SWA_COLS])
    sq_cols = SWA_HEADS * HEAD_DIM
    sk_cols = SWA_KV_HEADS * HEAD_DIM
    rope_heads(swa, swa_ref, 0, SWA_HEADS, k_scale * LOG2E)
    rope_heads(swa, swa_ref, sq_cols, SWA_KV_HEADS, None)
    swa_ref[:, sq_cols + sk_cols:SWA_COLS] = swa[:, sq_cols + sk_cols:SWA_COLS].astype(swa_ref.dtype)
    mla_ref[...] = _dot(h, w_ref[:, RET_COLS + SWA_COLS:IN_WIDTH_PAD])


def _inproj(x2, seq, mods, w_in, tabs, layer, mod_row0, rows_per_mod, tm):
    m_rows, d = x2.shape
    mod_idx = lambda i: (layer * MOD_ROWS + mod_row0 + (i * tm) // rows_per_mod, 0, 0)
    tpb = seq // tm
    tab = pl.BlockSpec((tm, LANES), lambda i: (i % tpb, 0))
    return pl.pallas_call(
        _inproj_kernel,
        grid=(m_rows // tm,),
        in_specs=[
            pl.BlockSpec((tm, d), lambda i: (i, 0)),
            pl.BlockSpec((None, N_MOD, d), mod_idx),
            _resident((None, d, IN_WIDTH_PAD), lambda i: (layer, 0, 0)),
            tab, tab, tab,
        ],
        out_specs=[
            pl.BlockSpec((RET_COLS // HEAD_DIM, tm, HEAD_DIM), lambda i: (0, i, 0)),
            pl.BlockSpec((tm, SWA_COLS), lambda i: (i, 0)),
            pl.BlockSpec((tm, MLA_COLS_PAD), lambda i: (i, 0)),
        ],
        out_shape=[
            jax.ShapeDtypeStruct((RET_COLS // HEAD_DIM, m_rows, HEAD_DIM), F32),
            jax.ShapeDtypeStruct((m_rows, SWA_COLS), BF16),
            jax.ShapeDtypeStruct((m_rows, MLA_COLS_PAD), F32),
        ],
        name="inproj",
        compiler_params=_cparams("parallel"),
    )(x2, mods, w_in, *tabs)


def _decay_terms(dec_ref):
    lg = _log_sigmoid(dec_ref[...])
    return lg[0, 0:1, 0:1], lg[1, 0:1, 0:1]


def _ret_bwd_kernel(k_ref, v_ref, dec_ref, s0_ref, sin_ref, sfin_ref, state, *, cpb):
    n = pl.program_id(2)
    c = RET_CHUNK

    @pl.when(n == 0)
    def _():
        state[...] = s0_ref[...]

    _, lg_b = _decay_terms(dec_ref)
    pos = lax.broadcasted_iota(jnp.int32, (c, 1), 0).astype(F32)
    kdec = jnp.exp(lg_b * pos)
    cdec = jnp.exp(lg_b * float(c))
    kv = []
    for ci in range(cpb):
        rows = pl.ds(ci * c, c)
        kv.append(_dot((k_ref[rows, :] * kdec).T.astype(BF16), v_ref[rows, :].astype(BF16)))
    st = state[...]
    for ci in reversed(range(cpb)):
        sin_ref[ci] = st.astype(sin_ref.dtype)
        st = cdec * st + kv[ci]
    state[...] = st

    @pl.when(n == pl.num_programs(2) - 1)
    def _():
        sfin_ref[...] = st


def _ret_fwd_kernel(q_ref, k_ref, v_ref, g_ref, dec_ref, s0_ref, sb_ref, y_ref, sfin_ref, state, *, cpb):
    n = pl.program_id(2)
    c = RET_CHUNK

    @pl.when(n == 0)
    def _():
        state[...] = s0_ref[...]

    lg_f, lg_b = _decay_terms(dec_ref)
    pos = lax.broadcasted_iota(jnp.int32, (c, 1), 0).astype(F32)
    ri = lax.broadcasted_iota(jnp.int32, (c, c), 0)
    cj = lax.broadcasted_iota(jnp.int32, (c, c), 1)
    diff = (ri - cj).astype(F32)
    intra = (jnp.where(diff >= 0, jnp.exp(lg_f * jnp.maximum(diff, 0.0)), 0.0)
             + jnp.where(diff <= 0, jnp.exp(lg_b * jnp.maximum(-diff, 0.0)), 0.0))
    qdec_f = jnp.exp(lg_f * (pos + 1.0))
    qdec_b = jnp.exp(lg_b * (float(c) - pos))
    kdec_f = jnp.exp(lg_f * (float(c) - 1.0 - pos))
    cdec_f = jnp.exp(lg_f * float(c))
    lhs, vs, kv = [], [], []
    for ci in range(cpb):
        rows = pl.ds(ci * c, c)
        q, k = q_ref[rows, :], k_ref[rows, :]
        v = v_ref[rows, :].astype(BF16)
        scores = _dot_nt(q.astype(BF16), k.astype(BF16)) * intra
        lhs.append(jnp.concatenate([scores.astype(BF16), (q * qdec_f).astype(BF16),
                                    (q * qdec_b).astype(BF16)], axis=1))
        vs.append(v)
        kv.append(_dot((k * kdec_f).T.astype(BF16), v))
    st = state[...]
    for ci in range(cpb):
        rows = pl.ds(ci * c, c)
        rhs = jnp.concatenate([vs[ci], st.astype(BF16), sb_ref[ci].astype(BF16)], axis=0)
        out = _dot(lhs[ci], rhs)
        y_ref[rows, :] = (_silu(g_ref[rows, :]) * _rms(out)).astype(y_ref.dtype)
        st = cdec_f * st + kv[ci]
    state[...] = st

    @pl.when(n == pl.num_programs(2) - 1)
    def _():
        sfin_ref[...] = st


def _retention(ret3, dec, layer, s0_f, s0_b):
    _, bsz, seq, _ = ret3.shape
    c = RET_CHUNK
    cpb = next(n for n in (8, 4, 2, 1) if seq % (n * c) == 0)
    rows = cpb * c
    nb = seq // rows
    h = RET_HEADS
    grid = (bsz, h, nb)
    state_spec = pl.BlockSpec((None, None, HEAD_DIM, HEAD_DIM), lambda b, hh, n: (b, hh, 0, 0))
    dec_spec = pl.BlockSpec((None, 2, None, SUBLANES, LANES), lambda b, hh, n: (layer, 0, hh, 0, 0))

    def col(j, rev):
        if rev:
            return pl.BlockSpec((None, None, rows, HEAD_DIM), lambda b, hh, n: (j * h + hh, b, nb - 1 - n, 0))
        return pl.BlockSpec((None, None, rows, HEAD_DIM), lambda b, hh, n: (j * h + hh, b, n, 0))

    sb_in, st_b = pl.pallas_call(
        functools.partial(_ret_bwd_kernel, cpb=cpb),
        grid=grid,
        in_specs=[col(1, True), col(2, True), dec_spec, state_spec],
        out_specs=[
            pl.BlockSpec((None, None, cpb, HEAD_DIM, HEAD_DIM), lambda b, hh, n: (b, hh, nb - 1 - n, 0, 0)),
            state_spec,
        ],
        out_shape=[
            jax.ShapeDtypeStruct((bsz, h, seq // c, HEAD_DIM, HEAD_DIM), BF16),
            jax.ShapeDtypeStruct((bsz, h, HEAD_DIM, HEAD_DIM), F32),
        ],
        scratch_shapes=[pltpu.VMEM((HEAD_DIM, HEAD_DIM), F32)],
        name="ret_bwd",
        compiler_params=_cparams("parallel", "parallel", "arbitrary"),
    )(ret3, ret3, dec, s0_b)

    y, st_f = pl.pallas_call(
        functools.partial(_ret_fwd_kernel, cpb=cpb),
        grid=grid,
        in_specs=[col(0, False), col(1, False), col(2, False), col(3, False), dec_spec, state_spec,
                  pl.BlockSpec((None, None, cpb, HEAD_DIM, HEAD_DIM), lambda b, hh, n: (b, hh, n, 0, 0))],
        out_specs=[
            pl.BlockSpec((None, rows, HEAD_DIM), lambda b, hh, n: (b, n, hh)),
            state_spec,
        ],
        out_shape=[
            jax.ShapeDtypeStruct((bsz, seq, RET_DIM), BF16),
            jax.ShapeDtypeStruct((bsz, h, HEAD_DIM, HEAD_DIM), F32),
        ],
        scratch_shapes=[pltpu.VMEM((HEAD_DIM, HEAD_DIM), F32)],
        name="ret_fwd",
        compiler_params=_cparams("parallel", "parallel", "arbitrary"),
    )(ret3, ret3, ret3, ret3, dec, s0_f, sb_in)
    return y, st_f, st_b


def _swa_kernel(q_ref, k_ref, v_ref, kp_ref, vp_ref, kn_ref, vn_ref, kc_ref, vc_ref, sink_ref, y_ref,
                *, tq, seq):
    i = pl.program_id(1)
    w = SWA_WINDOW
    nloc = tq + 2 * w
    q_pos = i * tq + lax.broadcasted_iota(jnp.int32, (tq, nloc), 0)
    k_pos = i * tq - w + lax.broadcasted_iota(jnp.int32, (tq, nloc), 1)
    valid = (jnp.abs(k_pos - q_pos) <= w) & (k_pos >= 0) & (k_pos < seq)
    nk = nloc + kc_ref.shape[0]
    ones = jnp.ones((nk, HEAD_DIM), BF16)
    for kv in range(SWA_KV_HEADS):
        ks = pl.ds(kv * HEAD_DIM, HEAD_DIM)
        kall = jnp.concatenate([kp_ref[:, ks], k_ref[:, ks], kn_ref[:, ks], kc_ref[:, ks]], axis=0)
        vall = jnp.concatenate(
            [jnp.concatenate([vp_ref[:, ks], v_ref[:, ks], vn_ref[:, ks], vc_ref[:, ks]], axis=0), ones], axis=1)
        for g in range(SWA_GROUP):
            hq = kv * SWA_GROUP + g
            hs = pl.ds(hq * HEAD_DIM, HEAD_DIM)
            s = _dot_nt(q_ref[:, hs], kall)
            s = jnp.concatenate([jnp.where(valid, s[:, 0:nloc], NEG_INF), s[:, nloc:]], axis=1)
            sink = sink_ref[hq, 0:1, 0:1] * LOG2E
            m = jnp.maximum(jnp.max(s, axis=-1, keepdims=True), sink)
            acc = _dot(jnp.exp2(s - m).astype(BF16), vall)
            den = acc[:, HEAD_DIM:2 * HEAD_DIM] + jnp.exp2(sink - m)
            y_ref[:, hs] = (acc[:, 0:HEAD_DIM] / den).astype(y_ref.dtype)


def _swa_latent(swa3, swa_ctx3, sink, layer):
    bsz, seq, _ = swa3.shape
    t = swa_ctx3.shape[1]
    w = SWA_WINDOW
    tq = 256 if seq % 256 == 0 else w
    r = tq // w
    nblk = seq // w
    kvw = SWA_KV_HEADS * HEAD_DIM
    qw = SWA_HEADS * HEAD_DIM
    k_col, v_col = qw // kvw, qw // kvw + 1
    prev = lambda i: jnp.maximum(i * r - 1, 0)
    nxt = lambda i: jnp.minimum((i + 1) * r, nblk - 1)
    return pl.pallas_call(
        functools.partial(_swa_kernel, tq=tq, seq=seq),
        grid=(bsz, seq // tq),
        in_specs=[
            pl.BlockSpec((None, tq, qw), lambda b, i: (b, i, 0)),
            pl.BlockSpec((None, tq, kvw), lambda b, i: (b, i, k_col)),
            pl.BlockSpec((None, tq, kvw), lambda b, i: (b, i, v_col)),
            pl.BlockSpec((None, w, kvw), lambda b, i: (b, prev(i), k_col)),
            pl.BlockSpec((None, w, kvw), lambda b, i: (b, prev(i), v_col)),
            pl.BlockSpec((None, w, kvw), lambda b, i: (b, nxt(i), k_col)),
            pl.BlockSpec((None, w, kvw), lambda b, i: (b, nxt(i), v_col)),
            pl.BlockSpec((None, t, kvw), lambda b, i: (b, 0, k_col)),
            pl.BlockSpec((None, t, kvw), lambda b, i: (b, 0, v_col)),
            pl.BlockSpec((None, SWA_HEADS, SUBLANES, LANES), lambda b, i: (layer, 0, 0, 0)),
        ],
        out_specs=pl.BlockSpec((None, tq, qw), lambda b, i: (b, i, 0)),
        out_shape=jax.ShapeDtypeStruct((bsz, seq, qw), BF16),
        name="swa_latent",
        compiler_params=_cparams("parallel", "parallel"),
    )(swa3, swa3, swa3, swa3, swa3, swa3, swa3, swa_ctx3, swa_ctx3, sink)


def _mla_proj_kernel(x_ref, qn_ref, kvn_ref, wq_ref, wk_ref, wv_ref, cos_ref, slo_ref, shi_ref,
                     q_ref, k_ref, v_ref):
    nq = MLA_ROPE_DIM // 4
    cos, slo, shi = cos_ref[...], slo_ref[...], shi_ref[...]
    cq = (_rms(x_ref[:, 0:MLA_Q_RANK]) * qn_ref[...]).astype(BF16)
    q = _dot(cq, wq_ref[...]) * (MLA_SCALE * LOG2E)
    ckv =(_rms(x_ref[:, MLA_Q_RANK:MLA_Q_RANK + MLA_KV_RANK]) * kvn_ref[...]).astype(BF16)
    kn = _dot(ckv, wk_ref[...])
    v_ref[...] = _dot(ckv, wv_ref[...]).astype(v_ref.dtype)
    kr = _rope(x_ref[:, MLA_Q_RANK + MLA_KV_RANK:MLA_COLS_PAD], cos, slo, shi, nq).astype(k_ref.dtype)
    for h in range(MLA_HEADS):
        a = h * MLA_QK_PAD
        q_ref[:, a:a + MLA_NOPE_DIM] = q[:, a:a + MLA_NOPE_DIM].astype(q_ref.dtype)
        q_ref[:, a + MLA_NOPE_DIM:a + MLA_QK_PAD] = _rope(
            q[:, a + MLA_NOPE_DIM:a + MLA_QK_PAD], cos, slo, shi, nq).astype(q_ref.dtype)
        k_ref[:, a:a + MLA_NOPE_DIM] = kn[:, h * MLA_NOPE_DIM:(h + 1) * MLA_NOPE_DIM].astype(k_ref.dtype)
        k_ref[:, a + MLA_NOPE_DIM:a + MLA_QK_PAD] = kr


def _mla_project(mla2, seq, qn, kvn, wq, wk, wv, tabs, layer, tm):
    m_rows = mla2.shape[0]
    cos, slo, shi = tabs
    tpb = seq // tm
    qkw = MLA_HEADS * MLA_QK_PAD
    vw = MLA_HEADS * MLA_V_DIM
    lay = lambda i: (layer, 0, 0)
    tab = pl.BlockSpec((tm, LANES), lambda i: (i % tpb, 0))
    return pl.pallas_call(
        _mla_proj_kernel,
        grid=(m_rows // tm,),
        in_specs=[
            pl.BlockSpec((tm, MLA_COLS_PAD), lambda i: (i, 0)),
            pl.BlockSpec((None, 1, MLA_Q_RANK), lay),
            pl.BlockSpec((None, 1, MLA_KV_RANK), lay),
            pl.BlockSpec((None, MLA_Q_RANK, qkw), lay),
            pl.BlockSpec((None, MLA_KV_RANK, vw), lay),
            pl.BlockSpec((None, MLA_KV_RANK, vw), lay),
            tab, tab, tab,
        ],
        out_specs=[
            pl.BlockSpec((tm, qkw), lambda i: (i, 0)),
            pl.BlockSpec((tm, qkw), lambda i: (i, 0)),
            pl.BlockSpec((tm, vw), lambda i: (i, 0)),
        ],
        out_shape=[
            jax.ShapeDtypeStruct((m_rows, qkw), BF16),
            jax.ShapeDtypeStruct((m_rows, qkw), BF16),
            jax.ShapeDtypeStruct((m_rows, vw), BF16),
        ],
        name="mla_project",
        compiler_params=_cparams("parallel"),
    )(mla2, qn, kvn, wq, wk, wv, cos, slo, shi)


def _attn_kernel(*refs, n_src, q_scale, has_sink, nq):
    q_ref = refs[0]
    k_refs = refs[1:1 + n_src]
    v_refs = refs[1 + n_src:1 + 2 * n_src]
    pos = 1 + 2 * n_src
    sink_ref = refs[pos] if has_sink else None
    sink_b_ref = refs[pos + 1] if has_sink else None
    o_ref = refs[pos + (2 if has_sink else 0)]
    scratch = refs[pos + (3 if has_sink else 1):]
    s_refs = scratch[0:n_src]
    va_refs = scratch[n_src:2 * n_src]
    mb_ref = scratch[2 * n_src]
    t = pl.program_id(0)
    tq = q_ref.shape[0]
    dv = o_ref.shape[-1]
    tks = [s_ref.shape[-1] for s_ref in s_refs]

    @pl.when(t == 0)
    def _():
        for s_ref in s_refs:
            s_ref[...] = jnp.zeros(s_ref.shape, F32)
        mb_ref[...] = jnp.zeros(mb_ref.shape, F32)

    @pl.when(jnp.maximum(t - 1, 0) % nq == 0)
    def _():
        for v_ref, va in zip(v_refs, va_refs):
            va[:, 0:dv] = v_ref[...].astype(BF16)
            va[:, dv:2 * dv] = jnp.ones((va.shape[0], dv), BF16)

    q = q_ref[...]
    if q_scale != 1.0:
        q = q.astype(F32) * q_scale
    q = q.astype(BF16)
    m_prev = mb_ref[...]

    def lane_max(m_vec, s):
        for c in range(s.shape[-1] // LANES):
            m_vec = jnp.maximum(m_vec, s[:, c * LANES:(c + 1) * LANES])
        return m_vec

    m_vec = jnp.full((tq, LANES), NEG_INF, F32)
    acc = jnp.zeros((tq, 2 * dv), F32)
    for k_ref, s_ref, va, tkk in zip(k_refs, s_refs, va_refs, tks):
        for j in range(s_ref.shape[0]):
            rows = pl.ds(j * tkk, tkk)
            p = jnp.exp2(s_ref[j] - jnp.concatenate([m_prev] * (tkk // LANES), axis=1))
            acc = acc + _dot(p.astype(BF16), va[rows, :])
            s = _dot_nt(q, k_ref[rows, :].astype(BF16))
            s_ref[j] = s
            m_vec = lane_max(m_vec, s)

    l = acc[:, dv:2 * dv]
    if has_sink:
        l = l + jnp.exp2(sink_b_ref[0:1, 0:1] * LOG2E - m_prev)
    o_ref[...] = (acc[:, 0:dv] / l).astype(o_ref.dtype)

    m_row = jnp.max(m_vec, axis=-1, keepdims=True)
    if has_sink:
        m_row = jnp.maximum(m_row, sink_ref[0:1, 0:1] * LOG2E)
    mb_ref[...] = jnp.broadcast_to(m_row, (tq, LANES))


def _attention(q3, ks, vs, *, n_heads, group, dq, dv, q_col0, k_col0, v_col0, q_scale, sink=None, layer=0,
               tq=512, tk=512):
    bsz, lq, _ = q3.shape
    tq = min(tq, lq)
    assert dv == LANES
    n_src = len(ks)
    tks = [min(tk, k.shape[1]) for k in ks]
    nq = lq // tq
    n_blocks = bsz * n_heads * nq

    def cur(t):
        t = jnp.minimum(t, n_blocks - 1)
        return t // (n_heads * nq), (t // nq) % n_heads, t % nq

    def prev(t):
        return cur(jnp.maximum(t - 1, 0))

    def at(fn, spec):
        def index_map(t):
            b, h, i = fn(t)
            return spec(b, h, i)
        return index_map

    in_specs = [pl.BlockSpec((None, tq, dq), at(cur, lambda b, h, i: (b, i, q_col0 + h)))]
    in_specs += [pl.BlockSpec((None, k.shape[1], dq), at(cur, lambda b, h, i: (b, 0, k_col0 + h // group)))
                 for k in ks]
    in_specs += [pl.BlockSpec((None, v.shape[1], dv), at(prev, lambda b, h, i: (b, 0, v_col0 + h // group)))
                 for v in vs]
    args = [q3] + list(ks) + list(vs)
    if sink is not None:
        for fn in (cur, prev):
            in_specs.append(pl.BlockSpec((None, None, SUBLANES, LANES),
                                         at(fn, lambda b, h, i: (layer, h, 0, 0))))
            args.append(sink)
    scratch = [pltpu.VMEM((k.shape[1] // tkk, tq, tkk), F32) for k, tkk in zip(ks, tks)]
    scratch += [pltpu.VMEM((v.shape[1], 2 * dv), BF16) for v in vs]
    scratch += [pltpu.VMEM((tq, LANES), F32)]
    return pl.pallas_call(
        functools.partial(_attn_kernel, n_src=n_src, q_scale=q_scale, has_sink=sink is not None, nq=nq),
        grid=(n_blocks + 1,),
        in_specs=in_specs,
        out_specs=pl.BlockSpec((None, tq, dv), at(prev, lambda b, h, i: (b, i, h))),
        out_shape=jax.ShapeDtypeStruct((bsz, lq, n_heads * dv), BF16),
        scratch_shapes=scratch,
        name="attention",
        compiler_params=_cparams("arbitrary"),
    )(*args)


def _outproj_kernel(yr_ref, ys_ref, ym_ref, x_ref, mod_ref, w_ref, g_ref, b_ref, o_ref, *, alpha):
    a, b = RET_DIM, RET_DIM + SWA_HEADS * HEAD_DIM
    mix = (_dot(yr_ref[...], w_ref[0:a, :]) + _dot(ys_ref[...], w_ref[a:b, :])
           + _dot(ym_ref[...], w_ref[b:MIX_WIDTH, :]))
    z = alpha * x_ref[...] + (1.0 + mod_ref[2:3, :]) * mix
    o_ref[...] = _layer_norm(z, g_ref[...], b_ref[...])


def _outproj_ln(y_ret, y_swa, y_mla, x2, mods, w_o, ln_g, ln_b, layer, mod_row0, rows_per_mod, tm, alpha):
    m_rows, d = x2.shape
    mod_idx = lambda i: (layer * MOD_ROWS + mod_row0 + (i * tm) // rows_per_mod, 0, 0)
    lay = lambda i: (layer, 0, 0)
    row = lambda w: pl.BlockSpec((tm, w), lambda i: (i, 0))
    return pl.pallas_call(
        functools.partial(_outproj_kernel, alpha=alpha),
        grid=(m_rows // tm,),
        in_specs=[
            row(RET_DIM), row(SWA_HEADS * HEAD_DIM), row(MLA_HEADS * MLA_V_DIM), row(d),
            pl.BlockSpec((None, N_MOD, d), mod_idx),
            _resident((None, MIX_WIDTH, d), lay),
            pl.BlockSpec((None, 1, d), lay),
            pl.BlockSpec((None, 1, d), lay),
        ],
        out_specs=row(d),
        out_shape=jax.ShapeDtypeStruct((m_rows, d), F32),
        name="outproj_ln",
        compiler_params=_cparams("parallel"),
    )(y_ret, y_swa, y_mla, x2, mods, w_o, ln_g, ln_b)


def _ffn_kernel(x_ref, xp_ref, xn_ref, mod_ref, wu_ref, wg_ref, cw_ref, cb_ref, wd_ref, g_ref, b_ref,
                o_ref, h_ext, g_ext, *, tm, seq, alpha):
    i = pl.program_id(0)
    f = pl.program_id(1)

    @pl.when(f == 0)
    def _():
        scale = 1.0 + mod_ref[4:5, :]
        shift = mod_ref[3:4, :]
        has_prev = jnp.where((i * tm) % seq != 0, 1.0, 0.0).astype(F32)
        has_next = jnp.where(((i + 1) * tm) % seq != 0, 1.0, 0.0).astype(F32)
        h_ext[0:HALO, :] = ((xp_ref[...] * scale + shift) * has_prev).astype(BF16)
        h_ext[HALO:HALO + tm, :] = (x_ref[...] * scale + shift).astype(BF16)
        h_ext[HALO + tm:tm + 2 * HALO, :] = ((xn_ref[...] * scale + shift) * has_next).astype(BF16)
        o_ref[...] = jnp.zeros_like(o_ref)

    g_ext[...] = _dot(h_ext[...], wg_ref[...])
    u = _dot(h_ext[HALO:HALO + tm, :], wu_ref[...])
    gc =(g_ext[HALO - 1:HALO - 1 + tm, :] * cw_ref[0:1, :] + g_ext[HALO:HALO + tm, :] * cw_ref[1:2, :]
          + g_ext[HALO + 1:HALO + 1 + tm, :] * cw_ref[2:3, :] + cb_ref[...])
    a = (_silu(gc) * u).astype(BF16)
    o_ref[...] += _dot(a, wd_ref[...])

    @pl.when(f == pl.num_programs(1) - 1)
    def _():
        xa = x_ref[...]
        z = alpha * xa + (1.0 + mod_ref[5:6, :]) * o_ref[...]
        o_ref[...] = _layer_norm(z, g_ref[...], b_ref[...])


def _ffn(x2, seq, mods, w_up, conv_w, conv_b, w_down, ln_g, ln_b, layer, mod_row0, rows_per_mod, tm, alpha):
    m_rows, d = x2.shape
    dff = w_down.shape[1]
    tf = 512 if dff % 512 == 0 else (256 if dff % 256 == 0 else LANES)
    nf = dff // tf
    hb = tm // HALO
    nhb = m_rows // HALO
    mod_idx = lambda i, f: (layer * MOD_ROWS + mod_row0 + (i * tm) // rows_per_mod, 0, 0)
    lay = lambda i, f: (layer, 0, 0)
    return pl.pallas_call(
        functools.partial(_ffn_kernel, tm=tm, seq=seq, alpha=alpha),
        grid=(m_rows // tm, nf),
        in_specs=[
            pl.BlockSpec((tm, d), lambda i, f: (i, 0)),
            pl.BlockSpec((HALO, d), lambda i, f: (jnp.maximum(i * hb - 1, 0), 0)),
            pl.BlockSpec((HALO, d), lambda i, f: (jnp.minimum((i + 1) * hb, nhb - 1), 0)),
            pl.BlockSpec((None, N_MOD, d), mod_idx),
            pl.BlockSpec((None, d, tf), lambda i, f: (layer, 0, f)),
            pl.BlockSpec((None, d, tf), lambda i, f: (layer, 0, nf + f)),
            pl.BlockSpec((None, 3, tf), lambda i, f: (layer, 0, f)),
            pl.BlockSpec((None, 1, tf), lambda i, f: (layer, 0, f)),
            pl.BlockSpec((None, tf, d), lambda i, f: (layer, f, 0)),
            pl.BlockSpec((None, 1, d), lay),
            pl.BlockSpec((None, 1, d), lay),
        ],
        out_specs=pl.BlockSpec((tm, d), lambda i, f: (i, 0)),
        out_shape=jax.ShapeDtypeStruct((m_rows, d), F32),
        scratch_shapes=[
            pltpu.VMEM((tm + 2 * HALO, d), BF16),
            pltpu.VMEM((tm + 2 * HALO, tf), F32),
        ],
        name="conv_ffn",
        compiler_params=_cparams("parallel", "arbitrary"),
    )(x2, x2, x2, mods, w_up, w_up, conv_w, conv_b, w_down, ln_g, ln_b)


def _rope_tables(n_tokens, dim):
    rows = n_tokens // GRID_W
    r = np.repeat(np.arange(rows, dtype=np.float32), GRID_W)
    cc = np.tile(np.arange(GRID_W, dtype=np.float32), rows)
    n_freq = dim // 4
    inv = jnp.asarray(ROPE_THETA, F32) ** (-jnp.arange(n_freq, dtype=F32) / n_freq)
    ang_r = jnp.asarray(r)[:, None] * inv
    ang_c = jnp.asarray(cc)[:, None] * inv
    ang = jnp.concatenate([ang_r, ang_r, ang_c, ang_c], axis=-1)
    cos, sin = jnp.cos(ang), jnp.sin(ang)
    lane = np.arange(dim)
    lo = jnp.asarray((lane % (2 * n_freq)) < n_freq)
    sin_lo = jnp.where(lo, -sin, 0.0)
    sin_hi = jnp.where(lo, 0.0, sin)
    pad = LANES - dim
    if pad:
        cos = jnp.pad(cos, ((0, 0), (0, pad)), constant_values=1.0)
        sin_lo = jnp.pad(sin_lo, ((0, 0), (0, pad)))
        sin_hi = jnp.pad(sin_hi, ((0, 0), (0, pad)))
    return cos, sin_lo, sin_hi


def _identity_tables(n_tokens):
    return (jnp.ones((n_tokens, LANES), F32), jnp.zeros((n_tokens, LANES), F32),
            jnp.zeros((n_tokens, LANES), F32))


def _lane_bcast(p):
    return jnp.broadcast_to(p.astype(F32)[..., None, None], p.shape + (SUBLANES, LANES))


def kernel(x, c, ctx, c_ctx, ada_w, ada_b, w_in, ret_decay_fwd, ret_decay_bwd, swa_sink, mla_q_norm, mla_w_uq,
           mla_kv_norm, mla_w_ukv, w_o, ln1_g, ln1_b, ffn_w_up, ffn_conv_w, ffn_conv_b, ffn_w_down, ln2_g, ln2_b):
    bsz, seq, d = x.shape
    t = ctx.shape[1]
    depth = w_in.shape[0]
    assert bsz + 1 <= MOD_ROWS and seq % RET_CHUNK == 0 and t % RET_CHUNK == 0
    assert seq % GRID_W == 0 and d % LANES == 0
    alpha = (2 * depth) ** 0.25

    w_in_b = jnp.pad(w_in, ((0, 0), (0, 0), (0, IN_WIDTH_PAD - IN_WIDTH))).astype(BF16)
    w_o_b = w_o.astype(BF16)
    w_up_b = ffn_w_up.astype(BF16)
    w_down_b = ffn_w_down.astype(BF16)
    hq = MLA_NOPE_DIM + MLA_ROPE_DIM
    wq = mla_w_uq.reshape(depth, MLA_Q_RANK, MLA_HEADS, hq)
    wq = jnp.pad(wq, ((0, 0), (0, 0), (0, 0), (0, MLA_QK_PAD - hq)))
    wq = wq.reshape(depth, MLA_Q_RANK, MLA_HEADS * MLA_QK_PAD).astype(BF16)
    wkv = mla_w_ukv.reshape(depth, MLA_KV_RANK, MLA_HEADS, MLA_NOPE_DIM + MLA_V_DIM)
    wk = wkv[..., :MLA_NOPE_DIM].reshape(depth, MLA_KV_RANK, MLA_HEADS * MLA_NOPE_DIM).astype(BF16)
    wv = wkv[..., MLA_NOPE_DIM:].reshape(depth, MLA_KV_RANK, MLA_HEADS * MLA_V_DIM).astype(BF16)
    qn = mla_q_norm.reshape(depth, 1, MLA_Q_RANK)
    kvn = mla_kv_norm.reshape(depth, 1, MLA_KV_RANK)
    g1, b1 = ln1_g.reshape(depth, 1, d), ln1_b.reshape(depth, 1, d)
    g2, b2 = ln2_g.reshape(depth, 1, d), ln2_b.reshape(depth, 1, d)
    conv_b = ffn_conv_b.reshape(depth, 1, -1)
    dec = _lane_bcast(jnp.stack([ret_decay_fwd, ret_decay_bwd], axis=1))
    sink = _lane_bcast(swa_sink)

    tabs_h = _rope_tables(seq, HEAD_DIM)
    tabs_m = _rope_tables(seq, MLA_ROPE_DIM)
    tabs_id = _identity_tables(t)

    cond = jnp.zeros((MOD_ROWS, d), F32).at[:bsz].set(c).at[bsz].set(c_ctx)
    mods = _ada_mod(cond, ada_w, ada_b).reshape(depth * MOD_ROWS, N_MOD, d)

    tm_x = 512 if seq % 512 == 0 else RET_CHUNK
    tm_f = tm_x
    tm_c = t if t <= 512 else RET_CHUNK
    zeros_state = jnp.zeros((bsz, RET_HEADS, HEAD_DIM, HEAD_DIM), F32)

    x2 = x.reshape(bsz * seq, d)
    xc2 = ctx.reshape(bsz * t, d)
    for l in range(depth):
        last = l == depth - 1
        ret_c, swa_c, mla_c = _inproj(xc2, t, mods, w_in_b, tabs_id, l, bsz, bsz * t, tm_c)
        swa_c3 = swa_c.reshape(bsz, t, SWA_COLS)
        y_ret_c, st_f, st_b = _retention(ret_c.reshape(-1, bsz, t, HEAD_DIM), dec, l, zeros_state, zeros_state)
        q_c, k_c, v_c = _mla_project(mla_c, t, qn, kvn, wq, wk, wv, tabs_id, l, tm_c)
        k_c3 = k_c.reshape(bsz, t, -1)
        v_c3 = v_c.reshape(bsz, t, -1)

        ret_x, swa_x, mla_x = _inproj(x2, seq, mods, w_in_b, tabs_h, l, 0, seq, tm_x)
        y_ret, _, _ = _retention(ret_x.reshape(-1, bsz, seq, HEAD_DIM), dec, l, st_f, st_b)
        y_swa = _swa_latent(swa_x.reshape(bsz, seq, SWA_COLS), swa_c3, sink, l)
        q_x, k_x, v_x = _mla_project(mla_x, seq, qn, kvn, wq, wk, wv, tabs_m, l, tm_x)
        y_mla = _attention(q_x.reshape(bsz, seq, -1), [k_x.reshape(bsz, seq, -1), k_c3],
                           [v_x.reshape(bsz, seq, -1), v_c3], n_heads=MLA_HEADS, group=1,
                           dq=MLA_QK_PAD, dv=MLA_V_DIM, q_col0=0, k_col0=0, v_col0=0, q_scale=1.0)
        x_a = _outproj_ln(y_ret.reshape(bsz * seq, -1), y_swa.reshape(bsz * seq, -1),
                          y_mla.reshape(bsz * seq, -1), x2, mods, w_o_b, g1, b1, l, 0, seq, tm_x, alpha)
        x_new = _ffn(x_a, seq, mods, w_up_b, ffn_conv_w, conv_b, w_down_b, g2, b2, l, 0, seq, tm_f, alpha)

        if not last:
            y_swa_c = _attention(swa_c3, [swa_c3], [swa_c3], n_heads=SWA_HEADS, group=SWA_GROUP,
                                 dq=HEAD_DIM, dv=HEAD_DIM, q_col0=0, k_col0=SWA_HEADS,
                                 v_col0=SWA_HEADS + SWA_KV_HEADS, q_scale=1.0, sink=sink, layer=l)
            y_mla_c = _attention(q_c.reshape(bsz, t, -1), [k_c3], [v_c3], n_heads=MLA_HEADS, group=1,
                                 dq=MLA_QK_PAD, dv=MLA_V_DIM, q_col0=0, k_col0=0, v_col0=0, q_scale=1.0)
            xc_a = _outproj_ln(y_ret_c.reshape(bsz * t, -1), y_swa_c.reshape(bsz * t, -1),
                               y_mla_c.reshape(bsz * t, -1), xc2, mods, w_o_b, g1, b1, l, bsz, bsz * t,
                               tm_c, alpha)
            xc2 = _ffn(xc_a, t, mods, w_up_b, ffn_conv_w, conv_b, w_down_b, g2, b2, l, bsz, bsz * t,
                       tm_c, alpha)
        x2 = x_new
    return x2.reshape(bsz, seq, d)
```

```python
import functools

import jax
import jax.numpy as jnp
import numpy as np
from jax import lax
from jax.experimental import pallas as pl
from jax.experimental.pallas import tpu as pltpu

GRID_W = 64
HEAD_DIM = 128
ROPE_THETA = 10000.0
RET_HEADS = 4
RET_DIM = RET_HEADS * HEAD_DIM
RET_CHUNK = 128
SWA_HEADS = 6
SWA_KV_HEADS = 2
SWA_GROUP = SWA_HEADS // SWA_KV_HEADS
SWA_WINDOW = 128
MLA_HEADS = 6
MLA_Q_RANK = 512
MLA_KV_RANK = 256
MLA_NOPE_DIM = 128
MLA_ROPE_DIM = 64
MLA_V_DIM = 128
MLA_SCALE = (MLA_NOPE_DIM + MLA_ROPE_DIM) ** -0.5
MLA_QK_PAD = 256
N_MOD = 6
LN_EPS = 1e-5
RMS_EPS = 1e-6
NEG_INF = -1e30
LOG2E = 1.4426950408889634

RET_COLS = 4 * RET_DIM
SWA_COLS = (SWA_HEADS + 2 * SWA_KV_HEADS) * HEAD_DIM
MLA_COLS = MLA_Q_RANK + MLA_KV_RANK + MLA_ROPE_DIM
MLA_COLS_PAD = 896
IN_WIDTH = RET_COLS + SWA_COLS + MLA_COLS
IN_WIDTH_PAD = RET_COLS + SWA_COLS + MLA_COLS_PAD
MIX_WIDTH = RET_DIM + SWA_HEADS * HEAD_DIM + MLA_HEADS * MLA_V_DIM

LANES = 128
SUBLANES = 8
MOD_ROWS = 8
VMEM_LIMIT = 56 * 1024 * 1024
HALO = 16

BF16 = jnp.bfloat16
F32 = jnp.float32


def _cparams(*sem):
    return pltpu.CompilerParams(dimension_semantics=sem, vmem_limit_bytes=VMEM_LIMIT)


def _resident(block, index_map):
    return pl.BlockSpec(block, index_map, pipeline_mode=pl.Buffered(1))


def _dot(a, b):
    return jnp.dot(a, b, preferred_element_type=F32)


def _dot_nt(a, b):
    return lax.dot_general(a, b, (((1,), (1,)), ((), ())), preferred_element_type=F32)


def _silu(x):
    return x * (1.0 / (1.0 + jnp.exp(-x)))


def _rope(x, cos, sin_lo, sin_hi, nq):
    w = x.shape[-1]
    return x * cos + pltpu.roll(x, w - nq, 1) * sin_lo + pltpu.roll(x, nq, 1) * sin_hi


def _layer_norm(z, g, b):
    mu = jnp.mean(z, axis=-1, keepdims=True)
    zc = z - mu
    var = jnp.mean(zc * zc, axis=-1, keepdims=True)
    return zc * lax.rsqrt(var + LN_EPS) * g + b


def _rms(x):
    return x * lax.rsqrt(jnp.mean(x * x, axis=-1, keepdims=True) + RMS_EPS)


def _log_sigmoid(x):
    return -(jnp.maximum(-x, 0.0) + jnp.log(1.0 + jnp.exp(-jnp.abs(x))))


def _ada_kernel(c_ref, w_ref, b_ref, o_ref):
    sc = _silu(c_ref[...]).astype(BF16)
    o_ref[...] = _dot(sc, w_ref[...].astype(BF16)) + b_ref[...]


def _ada_mod(cond, ada_w, ada_b):
    depth, d, n = ada_w.shape
    tn = next(c for c in (1024, 512, 256, LANES) if n % c == 0)
    return pl.pallas_call(
        _ada_kernel,
        grid=(depth, n // tn),
        in_specs=[
            pl.BlockSpec((MOD_ROWS, d), lambda l, j: (0, 0)),
            pl.BlockSpec((None, d, tn), lambda l, j: (l, 0, j)),
            pl.BlockSpec((None, 1, tn), lambda l, j: (l, 0, j)),
        ],
        out_specs=pl.BlockSpec((None, MOD_ROWS, tn), lambda l, j: (l, 0, j)),
        out_shape=jax.ShapeDtypeStruct((depth, MOD_ROWS, n), F32),
        name="ada_mod",
        compiler_params=_cparams("parallel", "parallel"),
    )(cond, ada_w, ada_b.reshape(depth, 1, n))


def _inproj_kernel(x_ref, mod_ref, w_ref, cos_ref, slo_ref, shi_ref, ret_ref, swa_ref, mla_ref):
    m = mod_ref[...]
    h = (x_ref[...] * (1.0 + m[1:2, :]) + m[0:1, :]).astype(BF16)
    cos, slo, shi = cos_ref[...], slo_ref[...], shi_ref[...]
    nq = HEAD_DIM // 4
    k_scale = HEAD_DIM ** -0.5

    def rope(t, c, scale):
        r = _rope(t[:, c:c + HEAD_DIM], cos, slo, shi, nq)
        return r if scale is None else r * scale

    def rope_heads(t, out_ref, col0, n_heads, scale):
        for hh in range(n_heads):
            c = col0 + hh * HEAD_DIM
            out_ref[:, c:c + HEAD_DIM] = rope(t, c, scale).astype(out_ref.dtype)

    ret = _dot(h, w_ref[:, 0:RET_COLS])
    for j in range(RET_COLS // HEAD_DIM):
        c = j * HEAD_DIM
        if j < RET_HEADS:
            ret_ref[j] = rope(ret, c, None)
        elif j < 2 * RET_HEADS:
            ret_ref[j] = rope(ret, c, k_scale)
        else:
            ret_ref[j] = ret[:, c:c + HEAD_DIM]
    swa = _dot(h, w_ref[:, RET_COLS:RET_COLS + SWA_COLS])
    sq_cols = SWA_HEADS * HEAD_DIM
    sk_cols = SWA_KV_HEADS * HEAD_DIM
    rope_heads(swa, swa_ref, 0, SWA_HEADS, k_scale * LOG2E)
    rope_heads(swa, swa_ref, sq_cols, SWA_KV_HEADS, None)
    swa_ref[:, sq_cols + sk_cols:SWA_COLS] = swa[:, sq_cols + sk_cols:SWA_COLS].astype(swa_ref.dtype)
    mla_ref[...] = _dot(h, w_ref[:, RET_COLS + SWA_COLS:IN_WIDTH_PAD])


def _inproj(x2, seq, mods, w_in, tabs, layer, mod_row0, rows_per_mod, tm):
    m_rows, d = x2.shape
    mod_idx = lambda i: (layer * MOD_ROWS + mod_row0 + (i * tm) // rows_per_mod, 0, 0)
    tpb = seq // tm
    tab = pl.BlockSpec((tm, LANES), lambda i: (i % tpb, 0))
    return pl.pallas_call(
        _inproj_kernel,
        grid=(m_rows // tm,),
        in_specs=[
            pl.BlockSpec((tm, d), lambda i: (i, 0)),
            pl.BlockSpec((None, N_MOD, d), mod_idx),
            _resident((None, d, IN_WIDTH_PAD), lambda i: (layer, 0, 0)),
            tab, tab, tab,
        ],
        out_specs=[
            pl.BlockSpec((RET_COLS // HEAD_DIM, tm, HEAD_DIM), lambda i: (0, i, 0)),
            pl.BlockSpec((tm, SWA_COLS), lambda i: (i, 0)),
            pl.BlockSpec((tm, MLA_COLS_PAD), lambda i: (i, 0)),
        ],
        out_shape=[
            jax.ShapeDtypeStruct((RET_COLS // HEAD_DIM, m_rows, HEAD_DIM), F32),
            jax.ShapeDtypeStruct((m_rows, SWA_COLS), BF16),
            jax.ShapeDtypeStruct((m_rows, MLA_COLS_PAD), F32),
        ],
        name="inproj",
        compiler_params=_cparams("parallel"),
    )(x2, mods, w_in, *tabs)


def _decay_terms(dec_ref):
    lg = _log_sigmoid(dec_ref[...])
    return lg[0, 0:1, 0:1], lg[1, 0:1, 0:1]


def _ret_bwd_kernel(k_ref, v_ref, dec_ref, s0_ref, sin_ref, sfin_ref, state, *, cpb):
    n = pl.program_id(2)
    c = RET_CHUNK

    @pl.when(n == 0)
    def _():
        state[...] = s0_ref[...]

    _, lg_b = _decay_terms(dec_ref)
    pos = lax.broadcasted_iota(jnp.int32, (c, 1), 0).astype(F32)
    kdec = jnp.exp(lg_b * pos)
    cdec = jnp.exp(lg_b * float(c))
    kv = []
    for ci in range(cpb):
        rows = pl.ds(ci * c, c)
        kv.append(_dot((k_ref[rows, :] * kdec).T.astype(BF16), v_ref[rows, :].astype(BF16)))
    st = state[...]
    for ci in reversed(range(cpb)):
        sin_ref[ci] = st.astype(sin_ref.dtype)
        st = cdec * st + kv[ci]
    state[...] = st

    @pl.when(n == pl.num_programs(2) - 1)
    def _():
        sfin_ref[...] = st


def _ret_fwd_kernel(q_ref, k_ref, v_ref, g_ref, dec_ref, s0_ref, sb_ref, y_ref, sfin_ref, state, *, cpb):
    n = pl.program_id(2)
    c = RET_CHUNK

    @pl.when(n == 0)
    def _():
        state[...] = s0_ref[...]

    lg_f, lg_b = _decay_terms(dec_ref)
    pos = lax.broadcasted_iota(jnp.int32, (c, 1), 0).astype(F32)
    ri = lax.broadcasted_iota(jnp.int32, (c, c), 0)
    cj = lax.broadcasted_iota(jnp.int32, (c, c), 1)
    diff = (ri - cj).astype(F32)
    intra = (jnp.where(diff >= 0, jnp.exp(lg_f * jnp.maximum(diff, 0.0)), 0.0)
             + jnp.where(diff <= 0, jnp.exp(lg_b * jnp.maximum(-diff, 0.0)), 0.0))
    qdec_f = jnp.exp(lg_f * (pos + 1.0))
    qdec_b = jnp.exp(lg_b * (float(c) - pos))
    kdec_f = jnp.exp(lg_f * (float(c) - 1.0 - pos))
    cdec_f = jnp.exp(lg_f * float(c))
    lhs, vs, kv = [], [], []
    for ci in range(cpb):
        rows = pl.ds(ci * c, c)
        q, k = q_ref[rows, :], k_ref[rows, :]
        v = v_ref[rows, :].astype(BF16)
        scores = _dot_nt(q.astype(BF16), k.astype(BF16)) * intra
        lhs.append(jnp.concatenate([scores.astype(BF16), (q * qdec_f).astype(BF16),
                                    (q * qdec_b).astype(BF16)], axis=1))
        vs.append(v)
        kv.append(_dot((k * kdec_f).T.astype(BF16), v))
    st = state[...]
    for ci in range(cpb):
        rows = pl.ds(ci * c, c)
        rhs = jnp.concatenate([vs[ci], st.astype(BF16), sb_ref[ci].astype(BF16)], axis=0)
        out = _dot(lhs[ci], rhs)
        y_ref[rows, :] = (_silu(g_ref[rows, :]) * _rms(out)).astype(y_ref.dtype)
        st = cdec_f * st + kv[ci]
    state[...] = st

    @pl.when(n == pl.num_programs(2) - 1)
    def _():
        sfin_ref[...] = st


def _retention(ret3, dec, layer, s0_f, s0_b):
    _, bsz, seq, _ = ret3.shape
    c = RET_CHUNK
    cpb = next(n for n in (16, 8, 4, 2, 1) if seq % (n * c) == 0)
    rows = cpb * c
    nb = seq // rows
    h = RET_HEADS
    grid = (bsz, h, nb)
    state_spec = pl.BlockSpec((None, None, HEAD_DIM, HEAD_DIM), lambda b, hh, n: (b, hh, 0, 0))
    dec_spec = pl.BlockSpec((None, 2, None, SUBLANES, LANES), lambda b, hh, n: (layer, 0, hh, 0, 0))

    def col(j, rev):
        if rev:
            return pl.BlockSpec((None, None, rows, HEAD_DIM), lambda b, hh, n: (j * h + hh, b, nb - 1 - n, 0))
        return pl.BlockSpec((None, None, rows, HEAD_DIM), lambda b, hh, n: (j * h + hh, b, n, 0))

    sb_in, st_b = pl.pallas_call(
        functools.partial(_ret_bwd_kernel, cpb=cpb),
        grid=grid,
        in_specs=[col(1, True), col(2, True), dec_spec, state_spec],
        out_specs=[
            pl.BlockSpec((None, None, cpb, HEAD_DIM, HEAD_DIM), lambda b, hh, n: (b, hh, nb - 1 - n, 0, 0)),
            state_spec,
        ],
        out_shape=[
            jax.ShapeDtypeStruct((bsz, h, seq // c, HEAD_DIM, HEAD_DIM), BF16),
            jax.ShapeDtypeStruct((bsz, h, HEAD_DIM, HEAD_DIM), F32),
        ],
        scratch_shapes=[pltpu.VMEM((HEAD_DIM, HEAD_DIM), F32)],
        name="ret_bwd",
        compiler_params=_cparams("parallel", "parallel", "arbitrary"),
    )(ret3, ret3, dec, s0_b)

    y, st_f = pl.pallas_call(
        functools.partial(_ret_fwd_kernel, cpb=cpb),
        grid=grid,
        in_specs=[col(0, False), col(1, False), col(2, False), col(3, False), dec_spec, state_spec,
                  pl.BlockSpec((None, None, cpb, HEAD_DIM, HEAD_DIM), lambda b, hh, n: (b, hh, n, 0, 0))],
        out_specs=[
            pl.BlockSpec((None, rows, HEAD_DIM), lambda b, hh, n: (b, n, hh)),
            state_spec,
        ],
        out_shape=[
            jax.ShapeDtypeStruct((bsz, seq, RET_DIM), BF16),
            jax.ShapeDtypeStruct((bsz, h, HEAD_DIM, HEAD_DIM), F32),
        ],
        scratch_shapes=[pltpu.VMEM((HEAD_DIM, HEAD_DIM), F32)],
        name="ret_fwd",
        compiler_params=_cparams("parallel", "parallel", "arbitrary"),
    )(ret3, ret3, ret3, ret3, dec, s0_f, sb_in)
    return y, st_f, st_b


def _swa_kernel(q_ref, k_ref, v_ref, kp_ref, vp_ref, kn_ref, vn_ref, kc_ref, vc_ref, sink_ref, y_ref,
                *, tq, seq):
    i = pl.program_id(1)
    w = SWA_WINDOW
    nloc = tq + 2 * w
    q_pos = i * tq + lax.broadcasted_iota(jnp.int32, (tq, nloc), 0)
    k_pos = i * tq - w + lax.broadcasted_iota(jnp.int32, (tq, nloc), 1)
    valid = (jnp.abs(k_pos - q_pos) <= w) & (k_pos >= 0) & (k_pos < seq)
    nk = nloc + kc_ref.shape[0]
    ones = jnp.ones((nk, HEAD_DIM), BF16)
    for kv in range(SWA_KV_HEADS):
        ks = pl.ds(kv * HEAD_DIM, HEAD_DIM)
        kall = jnp.concatenate([kp_ref[:, ks], k_ref[:, ks], kn_ref[:, ks], kc_ref[:, ks]], axis=0)
        vall = jnp.concatenate(
            [jnp.concatenate([vp_ref[:, ks], v_ref[:, ks], vn_ref[:, ks], vc_ref[:, ks]], axis=0), ones], axis=1)
        for g in range(SWA_GROUP):
            hq = kv * SWA_GROUP + g
            hs = pl.ds(hq * HEAD_DIM, HEAD_DIM)
            s = _dot_nt(q_ref[:, hs], kall)
            s = jnp.concatenate([jnp.where(valid, s[:, 0:nloc], NEG_INF), s[:, nloc:]], axis=1)
            sink = sink_ref[hq, 0:1, 0:1] * LOG2E
            m = jnp.maximum(jnp.max(s, axis=-1, keepdims=True), sink)
            acc = _dot(jnp.exp2(s - m).astype(BF16), vall)
            den = acc[:, HEAD_DIM:2 * HEAD_DIM] + jnp.exp2(sink - m)
            y_ref[:, hs] = (acc[:, 0:HEAD_DIM] / den).astype(y_ref.dtype)


def _swa_latent(swa3, swa_ctx3, sink, layer):
    bsz, seq, _ = swa3.shape
    t = swa_ctx3.shape[1]
    w = SWA_WINDOW
    tq = 256 if seq % 256 == 0 else w
    r = tq // w
    nblk = seq // w
    kvw = SWA_KV_HEADS * HEAD_DIM
    qw = SWA_HEADS * HEAD_DIM
    k_col, v_col = qw // kvw, qw // kvw + 1
    prev = lambda i: jnp.maximum(i * r - 1, 0)
    nxt = lambda i: jnp.minimum((i + 1) * r, nblk - 1)
    return pl.pallas_call(
        functools.partial(_swa_kernel, tq=tq, seq=seq),
        grid=(bsz, seq // tq),
        in_specs=[
            pl.BlockSpec((None, tq, qw), lambda b, i: (b, i, 0)),
            pl.BlockSpec((None, tq, kvw), lambda b, i: (b, i, k_col)),
            pl.BlockSpec((None, tq, kvw), lambda b, i: (b, i, v_col)),
            pl.BlockSpec((None, w, kvw), lambda b, i: (b, prev(i), k_col)),
            pl.BlockSpec((None, w, kvw), lambda b, i: (b, prev(i), v_col)),
            pl.BlockSpec((None, w, kvw), lambda b, i: (b, nxt(i), k_col)),
            pl.BlockSpec((None, w, kvw), lambda b, i: (b, nxt(i), v_col)),
            pl.BlockSpec((None, t, kvw), lambda b, i: (b, 0, k_col)),
            pl.BlockSpec((None, t, kvw), lambda b, i: (b, 0, v_col)),
            pl.BlockSpec((None, SWA_HEADS, SUBLANES, LANES), lambda b, i: (layer, 0, 0, 0)),
        ],
        out_specs=pl.BlockSpec((None, tq, qw), lambda b, i: (b, i, 0)),
        out_shape=jax.ShapeDtypeStruct((bsz, seq, qw), BF16),
        name="swa_latent",
        compiler_params=_cparams("parallel", "parallel"),
    )(swa3, swa3, swa3, swa3, swa3, swa3, swa3, swa_ctx3, swa_ctx3, sink)


def _mla_proj_kernel(x_ref, qn_ref, kvn_ref, wq_ref, wk_ref, wv_ref, cos_ref, slo_ref, shi_ref,
                     q_ref, k_ref, v_ref):
    nq = MLA_ROPE_DIM // 4
    cos, slo, shi = cos_ref[...], slo_ref[...], shi_ref[...]
    cq = (_rms(x_ref[:, 0:MLA_Q_RANK]) * qn_ref[...]).astype(BF16)
    q = _dot(cq, wq_ref[...]) * (MLA_SCALE * LOG2E)
    ckv =(_rms(x_ref[:, MLA_Q_RANK:MLA_Q_RANK + MLA_KV_RANK]) * kvn_ref[...]).astype(BF16)
    kn = _dot(ckv, wk_ref[...])
    v_ref[...] = _dot(ckv, wv_ref[...]).astype(v_ref.dtype)
    kr = _rope(x_ref[:, MLA_Q_RANK + MLA_KV_RANK:MLA_COLS_PAD], cos, slo, shi, nq).astype(k_ref.dtype)
    for h in range(MLA_HEADS):
        a = h * MLA_QK_PAD
        q_ref[:, a:a + MLA_NOPE_DIM] = q[:, a:a + MLA_NOPE_DIM].astype(q_ref.dtype)
        q_ref[:, a + MLA_NOPE_DIM:a + MLA_QK_PAD] = _rope(
            q[:, a + MLA_NOPE_DIM:a + MLA_QK_PAD], cos, slo, shi, nq).astype(q_ref.dtype)
        k_ref[:, a:a + MLA_NOPE_DIM] = kn[:, h * MLA_NOPE_DIM:(h + 1) * MLA_NOPE_DIM].astype(k_ref.dtype)
        k_ref[:, a + MLA_NOPE_DIM:a + MLA_QK_PAD] = kr


def _mla_project(mla2, seq, qn, kvn, wq, wk, wv, tabs, layer, tm):
    m_rows = mla2.shape[0]
    cos, slo, shi = tabs
    tpb = seq // tm
    qkw = MLA_HEADS * MLA_QK_PAD
    vw = MLA_HEADS * MLA_V_DIM
    lay = lambda i: (layer, 0, 0)
    tab = pl.BlockSpec((tm, LANES), lambda i: (i % tpb, 0))
    return pl.pallas_call(
        _mla_proj_kernel,
        grid=(m_rows // tm,),
        in_specs=[
            pl.BlockSpec((tm, MLA_COLS_PAD), lambda i: (i, 0)),
            pl.BlockSpec((None, 1, MLA_Q_RANK), lay),
            pl.BlockSpec((None, 1, MLA_KV_RANK), lay),
            pl.BlockSpec((None, MLA_Q_RANK, qkw), lay),
            pl.BlockSpec((None, MLA_KV_RANK, vw), lay),
            pl.BlockSpec((None, MLA_KV_RANK, vw), lay),
            tab, tab, tab,
        ],
        out_specs=[
            pl.BlockSpec((tm, qkw), lambda i: (i, 0)),
            pl.BlockSpec((tm, qkw), lambda i: (i, 0)),
            pl.BlockSpec((tm, vw), lambda i: (i, 0)),
        ],
        out_shape=[
            jax.ShapeDtypeStruct((m_rows, qkw), BF16),
            jax.ShapeDtypeStruct((m_rows, qkw), BF16),
            jax.ShapeDtypeStruct((m_rows, vw), BF16),
        ],
        name="mla_project",
        compiler_params=_cparams("parallel"),
    )(mla2, qn, kvn, wq, wk, wv, cos, slo, shi)


def _attn_kernel(*refs, n_src, q_scale, has_sink, nq):
    q_ref = refs[0]
    k_refs = refs[1:1 + n_src]
    v_refs = refs[1 + n_src:1 + 2 * n_src]
    pos = 1 + 2 * n_src
    sink_ref = refs[pos] if has_sink else None
    sink_b_ref = refs[pos + 1] if has_sink else None
    o_ref = refs[pos + (2 if has_sink else 0)]
    scratch = refs[pos + (3 if has_sink else 1):]
    s_refs = scratch[0:n_src]
    va_refs = scratch[n_src:2 * n_src]
    mb_ref = scratch[2 * n_src]
    t = pl.program_id(0)
    tq = q_ref.shape[0]
    dv = o_ref.shape[-1]
    tks = [s_ref.shape[-1] for s_ref in s_refs]

    @pl.when(t == 0)
    def _():
        for s_ref in s_refs:
            s_ref[...] = jnp.zeros(s_ref.shape, F32)
        mb_ref[...] = jnp.zeros(mb_ref.shape, F32)

    @pl.when(jnp.maximum(t - 1, 0) % nq == 0)
    def _():
        for v_ref, va in zip(v_refs, va_refs):
            va[:, 0:dv] = v_ref[...].astype(BF16)
            va[:, dv:2 * dv] = jnp.ones((va.shape[0], dv), BF16)

    q = q_ref[...]
    if q_scale != 1.0:
        q = q.astype(F32) * q_scale
    q = q.astype(BF16)
    m_prev = mb_ref[...]

    def lane_max(m_vec, s):
        for c in range(s.shape[-1] // LANES):
            m_vec = jnp.maximum(m_vec, s[:, c * LANES:(c + 1) * LANES])
        return m_vec

    m_vec = jnp.full((tq, LANES), NEG_INF, F32)
    acc = jnp.zeros((tq, 2 * dv), F32)
    for k_ref, s_ref, va, tkk in zip(k_refs, s_refs, va_refs, tks):
        for j in range(s_ref.shape[0]):
            rows = pl.ds(j * tkk, tkk)
            p = jnp.exp2(s_ref[j] - jnp.concatenate([m_prev] * (tkk // LANES), axis=1))
            acc = acc + _dot(p.astype(BF16), va[rows, :])
            s = _dot_nt(q, k_ref[rows, :].astype(BF16))
            s_ref[j] = s
            m_vec = lane_max(m_vec, s)

    l = acc[:, dv:2 * dv]
    if has_sink:
        l = l + jnp.exp2(sink_b_ref[0:1, 0:1] * LOG2E - m_prev)
    o_ref[...] = (acc[:, 0:dv] / l).astype(o_ref.dtype)

    m_row = jnp.max(m_vec, axis=-1, keepdims=True)
    if has_sink:
        m_row = jnp.maximum(m_row, sink_ref[0:1, 0:1] * LOG2E)
    mb_ref[...] = jnp.broadcast_to(m_row, (tq, LANES))


def _attention(q3, ks, vs, *, n_heads, group, dq, dv, q_col0, k_col0, v_col0, q_scale, sink=None, layer=0,
               tq=512, tk=512):
    bsz, lq, _ = q3.shape
    tq = min(tq, lq)
    assert dv == LANES
    n_src = len(ks)
    tks = [min(tk, k.shape[1]) for k in ks]
    nq = lq // tq
    n_blocks = bsz * n_heads * nq

    def cur(t):
        t = jnp.minimum(t, n_blocks - 1)
        return t // (n_heads * nq), (t // nq) % n_heads, t % nq

    def prev(t):
        return cur(jnp.maximum(t - 1, 0))

    def at(fn, spec):
        def index_map(t):
            b, h, i = fn(t)
            return spec(b, h, i)
        return index_map

    in_specs = [pl.BlockSpec((None, tq, dq), at(cur, lambda b, h, i: (b, i, q_col0 + h)))]
    in_specs += [pl.BlockSpec((None, k.shape[1], dq), at(cur, lambda b, h, i: (b, 0, k_col0 + h // group)))
                 for k in ks]
    in_specs += [pl.BlockSpec((None, v.shape[1], dv), at(prev, lambda b, h, i: (b, 0, v_col0 + h // group)))
                 for v in vs]
    args = [q3] + list(ks) + list(vs)
    if sink is not None:
        for fn in (cur, prev):
            in_specs.append(pl.BlockSpec((None, None, SUBLANES, LANES),
                                         at(fn, lambda b, h, i: (layer, h, 0, 0))))
            args.append(sink)
    scratch = [pltpu.VMEM((k.shape[1] // tkk, tq, tkk), F32) for k, tkk in zip(ks, tks)]
    scratch += [pltpu.VMEM((v.shape[1], 2 * dv), BF16) for v in vs]
    scratch += [pltpu.VMEM((tq, LANES), F32)]
    return pl.pallas_call(
        functools.partial(_attn_kernel, n_src=n_src, q_scale=q_scale, has_sink=sink is not None, nq=nq),
        grid=(n_blocks + 1,),
        in_specs=in_specs,
        out_specs=pl.BlockSpec((None, tq, dv), at(prev, lambda b, h, i: (b, i, h))),
        out_shape=jax.ShapeDtypeStruct((bsz, lq, n_heads * dv), BF16),
        scratch_shapes=scratch,
        name="attention",
        compiler_params=_cparams("arbitrary"),
    )(*args)


def _outproj_kernel(yr_ref, ys_ref, ym_ref, x_ref, mod_ref, w_ref, g_ref, b_ref, o_ref, *, alpha):
    a, b = RET_DIM, RET_DIM + SWA_HEADS * HEAD_DIM
    mix = (_dot(yr_ref[...], w_ref[0:a, :]) + _dot(ys_ref[...], w_ref[a:b, :])
           + _dot(ym_ref[...], w_ref[b:MIX_WIDTH, :]))
    z = alpha * x_ref[...] + (1.0 + mod_ref[2:3, :]) * mix
    o_ref[...] = _layer_norm(z, g_ref[...], b_ref[...])


def _outproj_ln(y_ret, y_swa, y_mla, x2, mods, w_o, ln_g, ln_b, layer, mod_row0, rows_per_mod, tm, alpha):
    m_rows, d = x2.shape
    mod_idx = lambda i: (layer * MOD_ROWS + mod_row0 + (i * tm) // rows_per_mod, 0, 0)
    lay = lambda i: (layer, 0, 0)
    row = lambda w: pl.BlockSpec((tm, w), lambda i: (i, 0))
    return pl.pallas_call(
        functools.partial(_outproj_kernel, alpha=alpha),
        grid=(m_rows // tm,),
        in_specs=[
            row(RET_DIM), row(SWA_HEADS * HEAD_DIM), row(MLA_HEADS * MLA_V_DIM), row(d),
            pl.BlockSpec((None, N_MOD, d), mod_idx),
            _resident((None, MIX_WIDTH, d), lay),
            pl.BlockSpec((None, 1, d), lay),
            pl.BlockSpec((None, 1, d), lay),
        ],
        out_specs=row(d),
        out_shape=jax.ShapeDtypeStruct((m_rows, d), F32),
        name="outproj_ln",
        compiler_params=_cparams("parallel"),
    )(y_ret, y_swa, y_mla, x2, mods, w_o, ln_g, ln_b)


def _ffn_kernel(x_ref, xp_ref, xn_ref, mod_ref, wu_ref, wg_ref, cw_ref, cb_ref, wd_ref, g_ref, b_ref,
                o_ref, h_ext, g_ext, *, tm, seq, alpha):
    i = pl.program_id(0)
    f = pl.program_id(1)

    @pl.when(f == 0)
    def _():
        scale = 1.0 + mod_ref[4:5, :]
        shift = mod_ref[3:4, :]
        has_prev = jnp.where((i * tm) % seq != 0, 1.0, 0.0).astype(F32)
        has_next = jnp.where(((i + 1) * tm) % seq != 0, 1.0, 0.0).astype(F32)
        h_ext[0:HALO, :] = ((xp_ref[...] * scale + shift) * has_prev).astype(BF16)
        h_ext[HALO:HALO + tm, :] = (x_ref[...] * scale + shift).astype(BF16)
        h_ext[HALO + tm:tm + 2 * HALO, :] = ((xn_ref[...] * scale + shift) * has_next).astype(BF16)
        o_ref[...] = jnp.zeros_like(o_ref)

    g_ext[...] = _dot(h_ext[...], wg_ref[...])
    u = _dot(h_ext[HALO:HALO + tm, :], wu_ref[...])
    gc =(g_ext[HALO - 1:HALO - 1 + tm, :] * cw_ref[0:1, :] + g_ext[HALO:HALO + tm, :] * cw_ref[1:2, :]
          + g_ext[HALO + 1:HALO + 1 + tm, :] * cw_ref[2:3, :] + cb_ref[...])
    a = (_silu(gc) * u).astype(BF16)
    o_ref[...] += _dot(a, wd_ref[...])

    @pl.when(f == pl.num_programs(1) - 1)
    def _():
        xa = x_ref[...]
        z = alpha * xa + (1.0 + mod_ref[5:6, :]) * o_ref[...]
        o_ref[...] = _layer_norm(z, g_ref[...], b_ref[...])


def _ffn(x2, seq, mods, w_up, conv_w, conv_b, w_down, ln_g, ln_b, layer, mod_row0, rows_per_mod, tm, alpha):
    m_rows, d = x2.shape
    dff = w_down.shape[1]
    tf = 512 if dff % 512 == 0 else (256 if dff % 256 == 0 else LANES)
    nf = dff // tf
    hb = tm // HALO
    nhb = m_rows // HALO
    mod_idx = lambda i, f: (layer * MOD_ROWS + mod_row0 + (i * tm) // rows_per_mod, 0, 0)
    lay = lambda i, f: (layer, 0, 0)
    return pl.pallas_call(
        functools.partial(_ffn_kernel, tm=tm, seq=seq, alpha=alpha),
        grid=(m_rows // tm, nf),
        in_specs=[
            pl.BlockSpec((tm, d), lambda i, f: (i, 0)),
            pl.BlockSpec((HALO, d), lambda i, f: (jnp.maximum(i * hb - 1, 0), 0)),
            pl.BlockSpec((HALO, d), lambda i, f: (jnp.minimum((i + 1) * hb, nhb - 1), 0)),
            pl.BlockSpec((None, N_MOD, d), mod_idx),
            pl.BlockSpec((None, d, tf), lambda i, f: (layer, 0, f)),
            pl.BlockSpec((None, d, tf), lambda i, f: (layer, 0, nf + f)),
            pl.BlockSpec((None, 3, tf), lambda i, f: (layer, 0, f)),
            pl.BlockSpec((None, 1, tf), lambda i, f: (layer, 0, f)),
            pl.BlockSpec((None, tf, d), lambda i, f: (layer, f, 0)),
            pl.BlockSpec((None, 1, d), lay),
            pl.BlockSpec((None, 1, d), lay),
        ],
        out_specs=pl.BlockSpec((tm, d), lambda i, f: (i, 0)),
        out_shape=jax.ShapeDtypeStruct((m_rows, d), F32),
        scratch_shapes=[
            pltpu.VMEM((tm + 2 * HALO, d), BF16),
            pltpu.VMEM((tm + 2 * HALO, tf), F32),
        ],
        name="conv_ffn",
        compiler_params=_cparams("parallel", "arbitrary"),
    )(x2, x2, x2, mods, w_up, w_up, conv_w, conv_b, w_down, ln_g, ln_b)


def _rope_tables(n_tokens, dim):
    rows = n_tokens // GRID_W
    r = np.repeat(np.arange(rows, dtype=np.float32), GRID_W)
    cc = np.tile(np.arange(GRID_W, dtype=np.float32), rows)
    n_freq = dim // 4
    inv = jnp.asarray(ROPE_THETA, F32) ** (-jnp.arange(n_freq, dtype=F32) / n_freq)
    ang_r = jnp.asarray(r)[:, None] * inv
    ang_c = jnp.asarray(cc)[:, None] * inv
    ang = jnp.concatenate([ang_r, ang_r, ang_c, ang_c], axis=-1)
    cos, sin = jnp.cos(ang), jnp.sin(ang)
    lane = np.arange(dim)
    lo = jnp.asarray((lane % (2 * n_freq)) < n_freq)
    sin_lo = jnp.where(lo, -sin, 0.0)
    sin_hi = jnp.where(lo, 0.0, sin)
    pad = LANES - dim
    if pad:
        cos = jnp.pad(cos, ((0, 0), (0, pad)), constant_values=1.0)
        sin_lo = jnp.pad(sin_lo, ((0, 0), (0, pad)))
        sin_hi = jnp.pad(sin_hi, ((0, 0), (0, pad)))
    return cos, sin_lo, sin_hi


def _identity_tables(n_tokens):
    return (jnp.ones((n_tokens, LANES), F32), jnp.zeros((n_tokens, LANES), F32),
            jnp.zeros((n_tokens, LANES), F32))


def _lane_bcast(p):
    return jnp.broadcast_to(p.astype(F32)[..., None, None], p.shape + (SUBLANES, LANES))


def kernel(x, c, ctx, c_ctx, ada_w, ada_b, w_in, ret_decay_fwd, ret_decay_bwd, swa_sink, mla_q_norm, mla_w_uq,
           mla_kv_norm, mla_w_ukv, w_o, ln1_g, ln1_b, ffn_w_up, ffn_conv_w, ffn_conv_b, ffn_w_down, ln2_g, ln2_b):
    bsz, seq, d = x.shape
    t = ctx.shape[1]
    depth = w_in.shape[0]
    assert bsz + 1 <= MOD_ROWS and seq % RET_CHUNK == 0 and t % RET_CHUNK == 0
    assert seq % GRID_W == 0 and d % LANES == 0
    alpha = (2 * depth) ** 0.25

    w_in_b = jnp.pad(w_in, ((0, 0), (0, 0), (0, IN_WIDTH_PAD - IN_WIDTH))).astype(BF16)
    w_o_b = w_o.astype(BF16)
    w_up_b = ffn_w_up.astype(BF16)
    w_down_b = ffn_w_down.astype(BF16)
    hq = MLA_NOPE_DIM + MLA_ROPE_DIM
    wq = mla_w_uq.reshape(depth, MLA_Q_RANK, MLA_HEADS, hq)
    wq = jnp.pad(wq, ((0, 0), (0, 0), (0, 0), (0, MLA_QK_PAD - hq)))
    wq = wq.reshape(depth, MLA_Q_RANK, MLA_HEADS * MLA_QK_PAD).astype(BF16)
    wkv = mla_w_ukv.reshape(depth, MLA_KV_RANK, MLA_HEADS, MLA_NOPE_DIM + MLA_V_DIM)
    wk = wkv[..., :MLA_NOPE_DIM].reshape(depth, MLA_KV_RANK, MLA_HEADS * MLA_NOPE_DIM).astype(BF16)
    wv = wkv[..., MLA_NOPE_DIM:].reshape(depth, MLA_KV_RANK, MLA_HEADS * MLA_V_DIM).astype(BF16)
    qn = mla_q_norm.reshape(depth, 1, MLA_Q_RANK)
    kvn = mla_kv_norm.reshape(depth, 1, MLA_KV_RANK)
    g1, b1 = ln1_g.reshape(depth, 1, d), ln1_b.reshape(depth, 1, d)
    g2, b2 = ln2_g.reshape(depth, 1, d), ln2_b.reshape(depth, 1, d)
    conv_b = ffn_conv_b.reshape(depth, 1, -1)
    dec = _lane_bcast(jnp.stack([ret_decay_fwd, ret_decay_bwd], axis=1))
    sink = _lane_bcast(swa_sink)

    tabs_h = _rope_tables(seq, HEAD_DIM)
    tabs_m = _rope_tables(seq, MLA_ROPE_DIM)
    tabs_id = _identity_tables(t)

    cond = jnp.zeros((MOD_ROWS, d), F32).at[:bsz].set(c).at[bsz].set(c_ctx)
    mods = _ada_mod(cond, ada_w, ada_b).reshape(depth * MOD_ROWS, N_MOD, d)

    tm_x = 512 if seq % 512 == 0 else RET_CHUNK
    tm_f = tm_x
    tm_p = 1024 if seq % 1024 == 0 else tm_x
    tm_c = t if t <= 512 else RET_CHUNK
    zeros_state = jnp.zeros((bsz, RET_HEADS, HEAD_DIM, HEAD_DIM), F32)

    x2 = x.reshape(bsz * seq, d)
    xc2 = ctx.reshape(bsz * t, d)
    for l in range(depth):
        last = l == depth - 1
        ret_c, swa_c, mla_c = _inproj(xc2, t, mods, w_in_b, tabs_id, l, bsz, bsz * t, tm_c)
        swa_c3 = swa_c.reshape(bsz, t, SWA_COLS)
        y_ret_c, st_f, st_b = _retention(ret_c.reshape(-1, bsz, t, HEAD_DIM), dec, l, zeros_state, zeros_state)
        q_c, k_c, v_c = _mla_project(mla_c, t, qn, kvn, wq, wk, wv, tabs_id, l, tm_c)
        k_c3 = k_c.reshape(bsz, t, -1)
        v_c3 = v_c.reshape(bsz, t, -1)

        ret_x, swa_x, mla_x = _inproj(x2, seq, mods, w_in_b, tabs_h, l, 0, seq, tm_x)
        y_ret, _, _ = _retention(ret_x.reshape(-1, bsz, seq, HEAD_DIM), dec, l, st_f, st_b)
        y_swa = _swa_latent(swa_x.reshape(bsz, seq, SWA_COLS), swa_c3, sink, l)
        q_x, k_x, v_x = _mla_project(mla_x, seq, qn, kvn, wq, wk, wv, tabs_m, l, tm_p)
        y_mla = _attention(q_x.reshape(bsz, seq, -1), [k_x.reshape(bsz, seq, -1), k_c3],
                           [v_x.reshape(bsz, seq, -1), v_c3], n_heads=MLA_HEADS, group=1,
                           dq=MLA_QK_PAD, dv=MLA_V_DIM, q_col0=0, k_col0=0, v_col0=0, q_scale=1.0)
        x_a = _outproj_ln(y_ret.reshape(bsz * seq, -1), y_swa.reshape(bsz * seq, -1),
                          y_mla.reshape(bsz * seq, -1), x2, mods, w_o_b, g1, b1, l, 0, seq, tm_x, alpha)
        x_new = _ffn(x_a, seq, mods, w_up_b, ffn_conv_w, conv_b, w_down_b, g2, b2, l, 0, seq, tm_f, alpha)

        if not last:
            y_swa_c = _attention(swa_c3, [swa_c3], [swa_c3], n_heads=SWA_HEADS, group=SWA_GROUP,
                                 dq=HEAD_DIM, dv=HEAD_DIM, q_col0=0, k_col0=SWA_HEADS,
                                 v_col0=SWA_HEADS + SWA_KV_HEADS, q_scale=1.0, sink=sink, layer=l)
            y_mla_c = _attention(q_c.reshape(bsz, t, -1), [k_c3], [v_c3], n_heads=MLA_HEADS, group=1,
                                 dq=MLA_QK_PAD, dv=MLA_V_DIM, q_col0=0, k_col0=0, v_col0=0, q_scale=1.0)
            xc_a = _outproj_ln(y_ret_c.reshape(bsz * t, -1), y_swa_c.reshape(bsz * t, -1),
                               y_mla_c.reshape(bsz * t, -1), xc2, mods, w_o_b, g1, b1, l, bsz, bsz * t,
                               tm_c, alpha)
            xc2 = _ffn(xc_a, t, mods, w_up_b, ffn_conv_w, conv_b, w_down_b, g2, b2, l, bsz, bsz * t,
                       tm_c, alpha)
        x2 = x_new
    return x2.reshape(bsz, seq, d)
```

```python
import functools

import jax
import jax.numpy as jnp
import numpy as np
from jax import lax
from jax.experimental import pallas as pl
from jax.experimental.pallas import tpu as pltpu

GRID_W = 64
HEAD_DIM = 128
ROPE_THETA = 10000.0
RET_HEADS = 4
RET_DIM = RET_HEADS * HEAD_DIM
RET_CHUNK = 128
SWA_HEADS = 6
SWA_KV_HEADS = 2
SWA_GROUP = SWA_HEADS // SWA_KV_HEADS
SWA_WINDOW = 128
MLA_HEADS = 6
MLA_Q_RANK = 512
MLA_KV_RANK = 256
MLA_NOPE_DIM = 128
MLA_ROPE_DIM = 64
MLA_V_DIM = 128
MLA_SCALE = (MLA_NOPE_DIM + MLA_ROPE_DIM) ** -0.5
MLA_QK_PAD = 256
N_MOD = 6
LN_EPS = 1e-5
RMS_EPS = 1e-6
NEG_INF = -1e30
LOG2E = 1.4426950408889634

RET_COLS = 4 * RET_DIM
SWA_COLS = (SWA_HEADS + 2 * SWA_KV_HEADS) * HEAD_DIM
MLA_COLS = MLA_Q_RANK + MLA_KV_RANK + MLA_ROPE_DIM
MLA_COLS_PAD = 896
IN_WIDTH = RET_COLS + SWA_COLS + MLA_COLS
IN_WIDTH_PAD = RET_COLS + SWA_COLS + MLA_COLS_PAD
MIX_WIDTH = RET_DIM + SWA_HEADS * HEAD_DIM + MLA_HEADS * MLA_V_DIM

LANES = 128
SUBLANES = 8
MOD_ROWS = 8
VMEM_LIMIT = 56 * 1024 * 1024
HALO = 16

BF16 = jnp.bfloat16
F32 = jnp.float32


def _cparams(*sem):
    return pltpu.CompilerParams(dimension_semantics=sem, vmem_limit_bytes=VMEM_LIMIT)


def _resident(block, index_map):
    return pl.BlockSpec(block, index_map, pipeline_mode=pl.Buffered(1))


def _dot(a, b):
    return jnp.dot(a, b, preferred_element_type=F32)


def _dot_nt(a, b):
    return lax.dot_general(a, b, (((1,), (1,)), ((), ())), preferred_element_type=F32)


def _silu(x):
    return x * (1.0 / (1.0 + jnp.exp(-x)))


def _rope(x, cos, sin_lo, sin_hi, nq):
    w = x.shape[-1]
    return x * cos + pltpu.roll(x, w - nq, 1) * sin_lo + pltpu.roll(x, nq, 1) * sin_hi


def _layer_norm(z, g, b):
    mu = jnp.mean(z, axis=-1, keepdims=True)
    zc = z - mu
    var = jnp.mean(zc * zc, axis=-1, keepdims=True)
    return zc * lax.rsqrt(var + LN_EPS) * g + b


def _rms(x):
    return x * lax.rsqrt(jnp.mean(x * x, axis=-1, keepdims=True) + RMS_EPS)


def _log_sigmoid(x):
    return -(jnp.maximum(-x, 0.0) + jnp.log(1.0 + jnp.exp(-jnp.abs(x))))


def _ada_kernel(c_ref, w_ref, b_ref, o_ref):
    sc = _silu(c_ref[...]).astype(BF16)
    o_ref[...] = _dot(sc, w_ref[...].astype(BF16)) + b_ref[...]


def _ada_mod(cond, ada_w, ada_b):
    depth, d, n = ada_w.shape
    tn = next(c for c in (1024, 512, 256, LANES) if n % c == 0)
    return pl.pallas_call(
        _ada_kernel,
        grid=(depth, n // tn),
        in_specs=[
            pl.BlockSpec((MOD_ROWS, d), lambda l, j: (0, 0)),
            pl.BlockSpec((None, d, tn), lambda l, j: (l, 0, j)),
            pl.BlockSpec((None, 1, tn), lambda l, j: (l, 0, j)),
        ],
        out_specs=pl.BlockSpec((None, MOD_ROWS, tn), lambda l, j: (l, 0, j)),
        out_shape=jax.ShapeDtypeStruct((depth, MOD_ROWS, n), F32),
        name="ada_mod",
        compiler_params=_cparams("parallel", "parallel"),
    )(cond, ada_w, ada_b.reshape(depth, 1, n))


def _inproj_kernel(x_ref, mod_ref, w_ref, cos_ref, slo_ref, shi_ref, ret_ref, swa_ref, mla_ref):
    m = mod_ref[...]
    h = (x_ref[...] * (1.0 + m[1:2, :]) + m[0:1, :]).astype(BF16)
    cos, slo, shi = cos_ref[...], slo_ref[...], shi_ref[...]
    nq = HEAD_DIM // 4
    k_scale = HEAD_DIM ** -0.5

    def rope(t, c, scale):
        r = _rope(t[:, c:c + HEAD_DIM], cos, slo, shi, nq)
        return r if scale is None else r * scale

    def rope_heads(t, out_ref, col0, n_heads, scale):
        for hh in range(n_heads):
            c = col0 + hh * HEAD_DIM
            out_ref[:, c:c + HEAD_DIM] = rope(t, c, scale).astype(out_ref.dtype)

    ret = _dot(h, w_ref[:, 0:RET_COLS])
    for j in range(RET_COLS // HEAD_DIM):
        c = j * HEAD_DIM
        if j < RET_HEADS:
            ret_ref[j] = rope(ret, c, None)
        elif j < 2 * RET_HEADS:
            ret_ref[j] = rope(ret, c, k_scale)
        else:
            ret_ref[j] = ret[:, c:c + HEAD_DIM]
    swa = _dot(h, w_ref[:, RET_COLS:RET_COLS + SWA_COLS])
    sq_cols = SWA_HEADS * HEAD_DIM
    sk_cols = SWA_KV_HEADS * HEAD_DIM
    rope_heads(swa, swa_ref, 0, SWA_HEADS, k_scale * LOG2E)
    rope_heads(swa, swa_ref, sq_cols, SWA_KV_HEADS, None)
    swa_ref[:, sq_cols + sk_cols:SWA_COLS] = swa[:, sq_cols + sk_cols:SWA_COLS].astype(swa_ref.dtype)
    mla_ref[...] = _dot(h, w_ref[:, RET_COLS + SWA_COLS:IN_WIDTH_PAD])


def _inproj(x2, seq, mods, w_in, tabs, layer, mod_row0, rows_per_mod, tm):
    m_rows, d = x2.shape
    mod_idx = lambda i: (layer * MOD_ROWS + mod_row0 + (i * tm) // rows_per_mod, 0, 0)
    tpb = seq // tm
    tab = pl.BlockSpec((tm, LANES), lambda i: (i % tpb, 0))
    return pl.pallas_call(
        _inproj_kernel,
        grid=(m_rows // tm,),
        in_specs=[
            pl.BlockSpec((tm, d), lambda i: (i, 0)),
            pl.BlockSpec((None, N_MOD, d), mod_idx),
            _resident((None, d, IN_WIDTH_PAD), lambda i: (layer, 0, 0)),
            tab, tab, tab,
        ],
        out_specs=[
            pl.BlockSpec((RET_COLS // HEAD_DIM, tm, HEAD_DIM), lambda i: (0, i, 0)),
            pl.BlockSpec((tm, SWA_COLS), lambda i: (i, 0)),
            pl.BlockSpec((tm, MLA_COLS_PAD), lambda i: (i, 0)),
        ],
        out_shape=[
            jax.ShapeDtypeStruct((RET_COLS // HEAD_DIM, m_rows, HEAD_DIM), F32),
            jax.ShapeDtypeStruct((m_rows, SWA_COLS), BF16),
            jax.ShapeDtypeStruct((m_rows, MLA_COLS_PAD), F32),
        ],
        name="inproj",
        compiler_params=_cparams("parallel"),
    )(x2, mods, w_in, *tabs)


def _decay_terms(dec_ref):
    lg = _log_sigmoid(dec_ref[...])
    return lg[0, 0:1, 0:1], lg[1, 0:1, 0:1]


def _ret_bwd_kernel(k_ref, v_ref, dec_ref, s0_ref, sin_ref, sfin_ref, state, *, cpb):
    n = pl.program_id(2)
    c = RET_CHUNK

    @pl.when(n == 0)
    def _():
        state[...] = s0_ref[...]

    _, lg_b = _decay_terms(dec_ref)
    pos = lax.broadcasted_iota(jnp.int32, (c, 1), 0).astype(F32)
    kdec = jnp.exp(lg_b * pos)
    cdec = jnp.exp(lg_b * float(c))
    kv = []
    for ci in range(cpb):
        rows = pl.ds(ci * c, c)
        kv.append(_dot((k_ref[rows, :] * kdec).T.astype(BF16), v_ref[rows, :].astype(BF16)))
    st = state[...]
    for ci in reversed(range(cpb)):
        sin_ref[ci] = st.astype(sin_ref.dtype)
        st = cdec * st + kv[ci]
    state[...] = st

    @pl.when(n == pl.num_programs(2) - 1)
    def _():
        sfin_ref[...] = st


def _ret_fwd_kernel(q_ref, k_ref, v_ref, g_ref, dec_ref, s0_ref, sb_ref, y_ref, sfin_ref, state, *, cpb):
    n = pl.program_id(2)
    c = RET_CHUNK

    @pl.when(n == 0)
    def _():
        state[...] = s0_ref[...]

    lg_f, lg_b = _decay_terms(dec_ref)
    pos = lax.broadcasted_iota(jnp.int32, (c, 1), 0).astype(F32)
    ri = lax.broadcasted_iota(jnp.int32, (c, c), 0)
    cj = lax.broadcasted_iota(jnp.int32, (c, c), 1)
    diff = (ri - cj).astype(F32)
    intra = (jnp.where(diff >= 0, jnp.exp(lg_f * jnp.maximum(diff, 0.0)), 0.0)
             + jnp.where(diff <= 0, jnp.exp(lg_b * jnp.maximum(-diff, 0.0)), 0.0))
    qdec_f = jnp.exp(lg_f * (pos + 1.0))
    qdec_b = jnp.exp(lg_b * (float(c) - pos))
    kdec_f = jnp.exp(lg_f * (float(c) - 1.0 - pos))
    cdec_f = jnp.exp(lg_f * float(c))
    lhs, vs, kv = [], [], []
    for ci in range(cpb):
        rows = pl.ds(ci * c, c)
        q, k = q_ref[rows, :], k_ref[rows, :]
        v = v_ref[rows, :].astype(BF16)
        scores = _dot_nt(q.astype(BF16), k.astype(BF16)) * intra
        lhs.append(jnp.concatenate([scores.astype(BF16), (q * qdec_f).astype(BF16),
                                    (q * qdec_b).astype(BF16)], axis=1))
        vs.append(v)
        kv.append(_dot((k * kdec_f).T.astype(BF16), v))
    st = state[...]
    for ci in range(cpb):
        rows = pl.ds(ci * c, c)
        rhs = jnp.concatenate([vs[ci], st.astype(BF16), sb_ref[ci].astype(BF16)], axis=0)
        out = _dot(lhs[ci], rhs)
        y_ref[rows, :] = (_silu(g_ref[rows, :]) * _rms(out)).astype(y_ref.dtype)
        st = cdec_f * st + kv[ci]
    state[...] = st

    @pl.when(n == pl.num_programs(2) - 1)
    def _():
        sfin_ref[...] = st


def _retention(ret3, dec, layer, s0_f, s0_b):
    _, bsz, seq, _ = ret3.shape
    c = RET_CHUNK
    cpb = next(n for n in (16, 8, 4, 2, 1) if seq % (n * c) == 0)
    rows = cpb * c
    nb = seq // rows
    h = RET_HEADS
    grid = (bsz, h, nb)
    state_spec = pl.BlockSpec((None, None, HEAD_DIM, HEAD_DIM), lambda b, hh, n: (b, hh, 0, 0))
    dec_spec = pl.BlockSpec((None, 2, None, SUBLANES, LANES), lambda b, hh, n: (layer, 0, hh, 0, 0))

    def col(j, rev):
        if rev:
            return pl.BlockSpec((None, None, rows, HEAD_DIM), lambda b, hh, n: (j * h + hh, b, nb - 1 - n, 0))
        return pl.BlockSpec((None, None, rows, HEAD_DIM), lambda b, hh, n: (j * h + hh, b, n, 0))

    sb_in, st_b = pl.pallas_call(
        functools.partial(_ret_bwd_kernel, cpb=cpb),
        grid=grid,
        in_specs=[col(1, True), col(2, True), dec_spec, state_spec],
        out_specs=[
            pl.BlockSpec((None, None, cpb, HEAD_DIM, HEAD_DIM), lambda b, hh, n: (b, hh, nb - 1 - n, 0, 0)),
            state_spec,
        ],
        out_shape=[
            jax.ShapeDtypeStruct((bsz, h, seq // c, HEAD_DIM, HEAD_DIM), BF16),
            jax.ShapeDtypeStruct((bsz, h, HEAD_DIM, HEAD_DIM), F32),
        ],
        scratch_shapes=[pltpu.VMEM((HEAD_DIM, HEAD_DIM), F32)],
        name="ret_bwd",
        compiler_params=_cparams("parallel", "parallel", "arbitrary"),
    )(ret3, ret3, dec, s0_b)

    y, st_f = pl.pallas_call(
        functools.partial(_ret_fwd_kernel, cpb=cpb),
        grid=grid,
        in_specs=[col(0, False), col(1, False), col(2, False), col(3, False), dec_spec, state_spec,
                  pl.BlockSpec((None, None, cpb, HEAD_DIM, HEAD_DIM), lambda b, hh, n: (b, hh, n, 0, 0))],
        out_specs=[
            pl.BlockSpec((None, rows, HEAD_DIM), lambda b, hh, n: (b, n, hh)),
            state_spec,
        ],
        out_shape=[
            jax.ShapeDtypeStruct((bsz, seq, RET_DIM), BF16),
            jax.ShapeDtypeStruct((bsz, h, HEAD_DIM, HEAD_DIM), F32),
        ],
        scratch_shapes=[pltpu.VMEM((HEAD_DIM, HEAD_DIM), F32)],
        name="ret_fwd",
        compiler_params=_cparams("parallel", "parallel", "arbitrary"),
    )(ret3, ret3, ret3, ret3, dec, s0_f, sb_in)
    return y, st_f, st_b


def _swa_kernel(q_ref, k_ref, v_ref, kp_ref, vp_ref, kn_ref, vn_ref, kc_ref, vc_ref, sink_ref, y_ref,
                *, tq, seq):
    i = pl.program_id(1)
    w = SWA_WINDOW
    nloc = tq + 2 * w
    q_pos = i * tq + lax.broadcasted_iota(jnp.int32, (tq, nloc), 0)
    k_pos = i * tq - w + lax.broadcasted_iota(jnp.int32, (tq, nloc), 1)
    valid = (jnp.abs(k_pos - q_pos) <= w) & (k_pos >= 0) & (k_pos < seq)
    nk = nloc + kc_ref.shape[0]
    ones = jnp.ones((nk, HEAD_DIM), BF16)
    for kv in range(SWA_KV_HEADS):
        ks = pl.ds(kv * HEAD_DIM, HEAD_DIM)
        kall = jnp.concatenate([kp_ref[:, ks], k_ref[:, ks], kn_ref[:, ks], kc_ref[:, ks]], axis=0)
        vall = jnp.concatenate(
            [jnp.concatenate([vp_ref[:, ks], v_ref[:, ks], vn_ref[:, ks], vc_ref[:, ks]], axis=0), ones], axis=1)
        for g in range(SWA_GROUP):
            hq = kv * SWA_GROUP + g
            hs = pl.ds(hq * HEAD_DIM, HEAD_DIM)
            s = _dot_nt(q_ref[:, hs], kall)
            s = jnp.concatenate([jnp.where(valid, s[:, 0:nloc], NEG_INF), s[:, nloc:]], axis=1)
            sink = sink_ref[hq, 0:1, 0:1] * LOG2E
            m = jnp.maximum(jnp.max(s, axis=-1, keepdims=True), sink)
            acc = _dot(jnp.exp2(s - m).astype(BF16), vall)
            den = acc[:, HEAD_DIM:2 * HEAD_DIM] + jnp.exp2(sink - m)
            y_ref[:, hs] = (acc[:, 0:HEAD_DIM] / den).astype(y_ref.dtype)


def _swa_latent(swa3, swa_ctx3, sink, layer):
    bsz, seq, _ = swa3.shape
    t = swa_ctx3.shape[1]
    w = SWA_WINDOW
    tq = 256 if seq % 256 == 0 else w
    r = tq // w
    nblk = seq // w
    kvw = SWA_KV_HEADS * HEAD_DIM
    qw = SWA_HEADS * HEAD_DIM
    k_col, v_col = qw // kvw, qw // kvw + 1
    prev = lambda i: jnp.maximum(i * r - 1, 0)
    nxt = lambda i: jnp.minimum((i + 1) * r, nblk - 1)
    return pl.pallas_call(
        functools.partial(_swa_kernel, tq=tq, seq=seq),
        grid=(bsz, seq // tq),
        in_specs=[
            pl.BlockSpec((None, tq, qw), lambda b, i: (b, i, 0)),
            pl.BlockSpec((None, tq, kvw), lambda b, i: (b, i, k_col)),
            pl.BlockSpec((None, tq, kvw), lambda b, i: (b, i, v_col)),
            pl.BlockSpec((None, w, kvw), lambda b, i: (b, prev(i), k_col)),
            pl.BlockSpec((None, w, kvw), lambda b, i: (b, prev(i), v_col)),
            pl.BlockSpec((None, w, kvw), lambda b, i: (b, nxt(i), k_col)),
            pl.BlockSpec((None, w, kvw), lambda b, i: (b, nxt(i), v_col)),
            pl.BlockSpec((None, t, kvw), lambda b, i: (b, 0, k_col)),
            pl.BlockSpec((None, t, kvw), lambda b, i: (b, 0, v_col)),
            pl.BlockSpec((None, SWA_HEADS, SUBLANES, LANES), lambda b, i: (layer, 0, 0, 0)),
        ],
        out_specs=pl.BlockSpec((None, tq, qw), lambda b, i: (b, i, 0)),
        out_shape=jax.ShapeDtypeStruct((bsz, seq, qw), BF16),
        name="swa_latent",
        compiler_params=_cparams("parallel", "parallel"),
    )(swa3, swa3, swa3, swa3, swa3, swa3, swa3, swa_ctx3, swa_ctx3, sink)


def _mla_proj_kernel(x_ref, qn_ref, kvn_ref, wq_ref, wk_ref, wv_ref, cos_ref, slo_ref, shi_ref,
                     q_ref, k_ref, v_ref):
    nq = MLA_ROPE_DIM // 4
    cos, slo, shi = cos_ref[...], slo_ref[...], shi_ref[...]
    cq = (_rms(x_ref[:, 0:MLA_Q_RANK]) * qn_ref[...]).astype(BF16)
    q = _dot(cq, wq_ref[...]) * (MLA_SCALE * LOG2E)
    ckv =(_rms(x_ref[:, MLA_Q_RANK:MLA_Q_RANK + MLA_KV_RANK]) * kvn_ref[...]).astype(BF16)
    kn = _dot(ckv, wk_ref[...])
    v_ref[...] = _dot(ckv, wv_ref[...]).astype(v_ref.dtype)
    kr = _rope(x_ref[:, MLA_Q_RANK + MLA_KV_RANK:MLA_COLS_PAD], cos, slo, shi, nq).astype(k_ref.dtype)
    for h in range(MLA_HEADS):
        a = h * MLA_QK_PAD
        q_ref[:, a:a + MLA_NOPE_DIM] = q[:, a:a + MLA_NOPE_DIM].astype(q_ref.dtype)
        q_ref[:, a + MLA_NOPE_DIM:a + MLA_QK_PAD] = _rope(
            q[:, a + MLA_NOPE_DIM:a + MLA_QK_PAD], cos, slo, shi, nq).astype(q_ref.dtype)
        k_ref[:, a:a + MLA_NOPE_DIM] = kn[:, h * MLA_NOPE_DIM:(h + 1) * MLA_NOPE_DIM].astype(k_ref.dtype)
        k_ref[:, a + MLA_NOPE_DIM:a + MLA_QK_PAD] = kr


def _mla_project(mla2, seq, qn, kvn, wq, wk, wv, tabs, layer, tm):
    m_rows = mla2.shape[0]
    cos, slo, shi = tabs
    tpb = seq // tm
    qkw = MLA_HEADS * MLA_QK_PAD
    vw = MLA_HEADS * MLA_V_DIM
    lay = lambda i: (layer, 0, 0)
    tab = pl.BlockSpec((tm, LANES), lambda i: (i % tpb, 0))
    return pl.pallas_call(
        _mla_proj_kernel,
        grid=(m_rows // tm,),
        in_specs=[
            pl.BlockSpec((tm, MLA_COLS_PAD), lambda i: (i, 0)),
            pl.BlockSpec((None, 1, MLA_Q_RANK), lay),
            pl.BlockSpec((None, 1, MLA_KV_RANK), lay),
            pl.BlockSpec((None, MLA_Q_RANK, qkw), lay),
            pl.BlockSpec((None, MLA_KV_RANK, vw), lay),
            pl.BlockSpec((None, MLA_KV_RANK, vw), lay),
            tab, tab, tab,
        ],
        out_specs=[
            pl.BlockSpec((tm, qkw), lambda i: (i, 0)),
            pl.BlockSpec((tm, qkw), lambda i: (i, 0)),
            pl.BlockSpec((tm, vw), lambda i: (i, 0)),
        ],
        out_shape=[
            jax.ShapeDtypeStruct((m_rows, qkw), BF16),
            jax.ShapeDtypeStruct((m_rows, qkw), BF16),
            jax.ShapeDtypeStruct((m_rows, vw), BF16),
        ],
        name="mla_project",
        compiler_params=_cparams("parallel"),
    )(mla2, qn, kvn, wq, wk, wv, cos, slo, shi)


def _attn_kernel(*refs, n_src, q_scale, has_sink, nq):
    q_ref = refs[0]
    k_refs = refs[1:1 + n_src]
    v_refs = refs[1 + n_src:1 + 2 * n_src]
    pos = 1 + 2 * n_src
    sink_ref = refs[pos] if has_sink else None
    sink_b_ref = refs[pos + 1] if has_sink else None
    o_ref = refs[pos + (2 if has_sink else 0)]
    scratch = refs[pos + (3 if has_sink else 1):]
    s_refs = scratch[0:n_src]
    va_refs = scratch[n_src:2 * n_src]
    mb_ref = scratch[2 * n_src]
    t = pl.program_id(0)
    tq = q_ref.shape[0]
    dv = o_ref.shape[-1]
    tks = [s_ref.shape[-1] for s_ref in s_refs]

    @pl.when(t == 0)
    def _():
        for s_ref in s_refs:
            s_ref[...] = jnp.zeros(s_ref.shape, F32)
        mb_ref[...] = jnp.zeros(mb_ref.shape, F32)

    @pl.when(jnp.maximum(t - 1, 0) % nq == 0)
    def _():
        for v_ref, va in zip(v_refs, va_refs):
            va[:, 0:dv] = v_ref[...].astype(BF16)
            va[:, dv:2 * dv] = jnp.ones((va.shape[0], dv), BF16)

    q = q_ref[...]
    if q_scale != 1.0:
        q = q.astype(F32) * q_scale
    q = q.astype(BF16)
    m_prev = mb_ref[...]

    def lane_max(m_vec, s):
        for c in range(s.shape[-1] // LANES):
            m_vec = jnp.maximum(m_vec, s[:, c * LANES:(c + 1) * LANES])
        return m_vec

    m_vec = jnp.full((tq, LANES), NEG_INF, F32)
    acc = jnp.zeros((tq, 2 * dv), F32)
    for k_ref, s_ref, va, tkk in zip(k_refs, s_refs, va_refs, tks):
        for j in range(s_ref.shape[0]):
            rows = pl.ds(j * tkk, tkk)
            p = jnp.exp2(s_ref[j] - jnp.concatenate([m_prev] * (tkk // LANES), axis=1))
            acc = acc + _dot(p.astype(BF16), va[rows, :])
            s = _dot_nt(q, k_ref[rows, :].astype(BF16))
            s_ref[j] = s
            m_vec = lane_max(m_vec, s)

    l = acc[:, dv:2 * dv]
    if has_sink:
        l = l + jnp.exp2(sink_b_ref[0:1, 0:1] * LOG2E - m_prev)
    o_ref[...] = (acc[:, 0:dv] / l).astype(o_ref.dtype)

    m_row = jnp.max(m_vec, axis=-1, keepdims=True)
    if has_sink:
        m_row = jnp.maximum(m_row, sink_ref[0:1, 0:1] * LOG2E)
    mb_ref[...] = jnp.broadcast_to(m_row, (tq, LANES))


def _attention(q3, ks, vs, *, n_heads, group, dq, dv, q_col0, k_col0, v_col0, q_scale, sink=None, layer=0,
               tq=512, tk=512):
    bsz, lq, _ = q3.shape
    tq = min(tq, lq)
    assert dv == LANES
    n_src = len(ks)
    tks = [min(tk, k.shape[1]) for k in ks]
    nq = lq // tq
    n_blocks = bsz * n_heads * nq

    def cur(t):
        t = jnp.minimum(t, n_blocks - 1)
        return t // (n_heads * nq), (t // nq) % n_heads, t % nq

    def prev(t):
        return cur(jnp.maximum(t - 1, 0))

    def at(fn, spec):
        def index_map(t):
            b, h, i = fn(t)
            return spec(b, h, i)
        return index_map

    in_specs = [pl.BlockSpec((None, tq, dq), at(cur, lambda b, h, i: (b, i, q_col0 + h)))]
    in_specs += [pl.BlockSpec((None, k.shape[1], dq), at(cur, lambda b, h, i: (b, 0, k_col0 + h // group)))
                 for k in ks]
    in_specs += [pl.BlockSpec((None, v.shape[1], dv), at(prev, lambda b, h, i: (b, 0, v_col0 + h // group)))
                 for v in vs]
    args = [q3] + list(ks) + list(vs)
    if sink is not None:
        for fn in (cur, prev):
            in_specs.append(pl.BlockSpec((None, None, SUBLANES, LANES),
                                         at(fn, lambda b, h, i: (layer, h, 0, 0))))
            args.append(sink)
    scratch = [pltpu.VMEM((k.shape[1] // tkk, tq, tkk), F32) for k, tkk in zip(ks, tks)]
    scratch += [pltpu.VMEM((v.shape[1], 2 * dv), BF16) for v in vs]
    scratch += [pltpu.VMEM((tq, LANES), F32)]
    return pl.pallas_call(
        functools.partial(_attn_kernel, n_src=n_src, q_scale=q_scale, has_sink=sink is not None, nq=nq),
        grid=(n_blocks + 1,),
        in_specs=in_specs,
        out_specs=pl.BlockSpec((None, tq, dv), at(prev, lambda b, h, i: (b, i, h))),
        out_shape=jax.ShapeDtypeStruct((bsz, lq, n_heads * dv), BF16),
        scratch_shapes=scratch,
        name="attention",
        compiler_params=_cparams("arbitrary"),
    )(*args)


def _outproj_kernel(yr_ref, ys_ref, ym_ref, x_ref, mod_ref, w_ref, g_ref, b_ref, o_ref, *, alpha):
    a, b = RET_DIM, RET_DIM + SWA_HEADS * HEAD_DIM
    mix = (_dot(yr_ref[...], w_ref[0:a, :]) + _dot(ys_ref[...], w_ref[a:b, :])
           + _dot(ym_ref[...], w_ref[b:MIX_WIDTH, :]))
    z = alpha * x_ref[...] + (1.0 + mod_ref[2:3, :]) * mix
    o_ref[...] = _layer_norm(z, g_ref[...], b_ref[...])


def _outproj_ln(y_ret, y_swa, y_mla, x2, mods, w_o, ln_g, ln_b, layer, mod_row0, rows_per_mod, tm, alpha):
    m_rows, d = x2.shape
    mod_idx = lambda i: (layer * MOD_ROWS + mod_row0 + (i * tm) // rows_per_mod, 0, 0)
    lay = lambda i: (layer, 0, 0)
    row = lambda w: pl.BlockSpec((tm, w), lambda i: (i, 0))
    return pl.pallas_call(
        functools.partial(_outproj_kernel, alpha=alpha),
        grid=(m_rows // tm,),
        in_specs=[
            row(RET_DIM), row(SWA_HEADS * HEAD_DIM), row(MLA_HEADS * MLA_V_DIM), row(d),
            pl.BlockSpec((None, N_MOD, d), mod_idx),
            _resident((None, MIX_WIDTH, d), lay),
            pl.BlockSpec((None, 1, d), lay),
            pl.BlockSpec((None, 1, d), lay),
        ],
        out_specs=row(d),
        out_shape=jax.ShapeDtypeStruct((m_rows, d), F32),
        name="outproj_ln",
        compiler_params=_cparams("parallel"),
    )(y_ret, y_swa, y_mla, x2, mods, w_o, ln_g, ln_b)


def _ffn_kernel(x_ref, xp_ref, xn_ref, mod_ref, wu_ref, wg_ref, cw_ref, cb_ref, wd_ref, g_ref, b_ref,
                o_ref, h_ext, g_ext, *, tm, seq, alpha):
    i = pl.program_id(0)
    f = pl.program_id(1)

    @pl.when(f == 0)
    def _():
        scale = 1.0 + mod_ref[4:5, :]
        shift = mod_ref[3:4, :]
        has_prev = jnp.where((i * tm) % seq != 0, 1.0, 0.0).astype(F32)
        has_next = jnp.where(((i + 1) * tm) % seq != 0, 1.0, 0.0).astype(F32)
        h_ext[0:HALO, :] = ((xp_ref[...] * scale + shift) * has_prev).astype(BF16)
        h_ext[HALO:HALO + tm, :] = (x_ref[...] * scale + shift).astype(BF16)
        h_ext[HALO + tm:tm + 2 * HALO, :] = ((xn_ref[...] * scale + shift) * has_next).astype(BF16)
        o_ref[...] = jnp.zeros_like(o_ref)

    g_ext[...] = _dot(h_ext[...], wg_ref[...])
    u = _dot(h_ext[HALO:HALO + tm, :], wu_ref[...])
    gc =(g_ext[HALO - 1:HALO - 1 + tm, :] * cw_ref[0:1, :] + g_ext[HALO:HALO + tm, :] * cw_ref[1:2, :]
          + g_ext[HALO + 1:HALO + 1 + tm, :] * cw_ref[2:3, :] + cb_ref[...])
    a = (_silu(gc) * u).astype(BF16)
    o_ref[...] += _dot(a, wd_ref[...])

    @pl.when(f == pl.num_programs(1) - 1)
    def _():
        xa = x_ref[...]
        z = alpha * xa + (1.0 + mod_ref[5:6, :]) * o_ref[...]
        o_ref[...] = _layer_norm(z, g_ref[...], b_ref[...])


def _ffn(x2, seq, mods, w_up, conv_w, conv_b, w_down, ln_g, ln_b, layer, mod_row0, rows_per_mod, tm, alpha):
    m_rows, d = x2.shape
    dff = w_down.shape[1]
    tf = 512 if dff % 512 == 0 else (256 if dff % 256 == 0 else LANES)
    nf = dff // tf
    hb = tm // HALO
    nhb = m_rows // HALO
    mod_idx = lambda i, f: (layer * MOD_ROWS + mod_row0 + (i * tm) // rows_per_mod, 0, 0)
    lay = lambda i, f: (layer, 0, 0)
    return pl.pallas_call(
        functools.partial(_ffn_kernel, tm=tm, seq=seq, alpha=alpha),
        grid=(m_rows // tm, nf),
        in_specs=[
            _resident((tm, d), lambda i, f: (i, 0)),
            pl.BlockSpec((HALO, d), lambda i, f: (jnp.maximum(i * hb - 1, 0), 0)),
            pl.BlockSpec((HALO, d), lambda i, f: (jnp.minimum((i + 1) * hb, nhb - 1), 0)),
            pl.BlockSpec((None, N_MOD, d), mod_idx),
            pl.BlockSpec((None, d, tf), lambda i, f: (layer, 0, f)),
            pl.BlockSpec((None, d, tf), lambda i, f: (layer, 0, nf + f)),
            pl.BlockSpec((None, 3, tf), lambda i, f: (layer, 0, f)),
            pl.BlockSpec((None, 1, tf), lambda i, f: (layer, 0, f)),
            pl.BlockSpec((None, tf, d), lambda i, f: (layer, f, 0)),
            pl.BlockSpec((None, 1, d), lay),
            pl.BlockSpec((None, 1, d), lay),
        ],
        out_specs=pl.BlockSpec((tm, d), lambda i, f: (i, 0)),
        out_shape=jax.ShapeDtypeStruct((m_rows, d), F32),
        scratch_shapes=[
            pltpu.VMEM((tm + 2 * HALO, d), BF16),
            pltpu.VMEM((tm + 2 * HALO, tf), F32),
        ],
        name="conv_ffn",
        compiler_params=_cparams("parallel", "arbitrary"),
    )(x2, x2, x2, mods, w_up, w_up, conv_w, conv_b, w_down, ln_g, ln_b)


def _rope_tables(n_tokens, dim):
    rows = n_tokens // GRID_W
    r = np.repeat(np.arange(rows, dtype=np.float32), GRID_W)
    cc = np.tile(np.arange(GRID_W, dtype=np.float32), rows)
    n_freq = dim // 4
    inv = jnp.asarray(ROPE_THETA, F32) ** (-jnp.arange(n_freq, dtype=F32) / n_freq)
    ang_r = jnp.asarray(r)[:, None] * inv
    ang_c = jnp.asarray(cc)[:, None] * inv
    ang = jnp.concatenate([ang_r, ang_r, ang_c, ang_c], axis=-1)
    cos, sin = jnp.cos(ang), jnp.sin(ang)
    lane = np.arange(dim)
    lo = jnp.asarray((lane % (2 * n_freq)) < n_freq)
    sin_lo = jnp.where(lo, -sin, 0.0)
    sin_hi = jnp.where(lo, 0.0, sin)
    pad = LANES - dim
    if pad:
        cos = jnp.pad(cos, ((0, 0), (0, pad)), constant_values=1.0)
        sin_lo = jnp.pad(sin_lo, ((0, 0), (0, pad)))
        sin_hi = jnp.pad(sin_hi, ((0, 0), (0, pad)))
    return cos, sin_lo, sin_hi


def _identity_tables(n_tokens):
    return (jnp.ones((n_tokens, LANES), F32), jnp.zeros((n_tokens, LANES), F32),
            jnp.zeros((n_tokens, LANES), F32))


def _lane_bcast(p):
    return jnp.broadcast_to(p.astype(F32)[..., None, None], p.shape + (SUBLANES, LANES))


def kernel(x, c, ctx, c_ctx, ada_w, ada_b, w_in, ret_decay_fwd, ret_decay_bwd, swa_sink, mla_q_norm, mla_w_uq,
           mla_kv_norm, mla_w_ukv, w_o, ln1_g, ln1_b, ffn_w_up, ffn_conv_w, ffn_conv_b, ffn_w_down, ln2_g, ln2_b):
    bsz, seq, d = x.shape
    t = ctx.shape[1]
    depth = w_in.shape[0]
    assert bsz + 1 <= MOD_ROWS and seq % RET_CHUNK == 0 and t % RET_CHUNK == 0
    assert seq % GRID_W == 0 and d % LANES == 0
    alpha = (2 * depth) ** 0.25

    w_in_b = jnp.pad(w_in, ((0, 0), (0, 0), (0, IN_WIDTH_PAD - IN_WIDTH))).astype(BF16)
    w_o_b = w_o.astype(BF16)
    w_up_b = ffn_w_up.astype(BF16)
    w_down_b = ffn_w_down.astype(BF16)
    hq = MLA_NOPE_DIM + MLA_ROPE_DIM
    wq = mla_w_uq.reshape(depth, MLA_Q_RANK, MLA_HEADS, hq)
    wq = jnp.pad(wq, ((0, 0), (0, 0), (0, 0), (0, MLA_QK_PAD - hq)))
    wq = wq.reshape(depth, MLA_Q_RANK, MLA_HEADS * MLA_QK_PAD).astype(BF16)
    wkv = mla_w_ukv.reshape(depth, MLA_KV_RANK, MLA_HEADS, MLA_NOPE_DIM + MLA_V_DIM)
    wk = wkv[..., :MLA_NOPE_DIM].reshape(depth, MLA_KV_RANK, MLA_HEADS * MLA_NOPE_DIM).astype(BF16)
    wv = wkv[..., MLA_NOPE_DIM:].reshape(depth, MLA_KV_RANK, MLA_HEADS * MLA_V_DIM).astype(BF16)
    qn = mla_q_norm.reshape(depth, 1, MLA_Q_RANK)
    kvn = mla_kv_norm.reshape(depth, 1, MLA_KV_RANK)
    g1, b1 = ln1_g.reshape(depth, 1, d), ln1_b.reshape(depth, 1, d)
    g2, b2 = ln2_g.reshape(depth, 1, d), ln2_b.reshape(depth, 1, d)
    conv_b = ffn_conv_b.reshape(depth, 1, -1)
    dec = _lane_bcast(jnp.stack([ret_decay_fwd, ret_decay_bwd], axis=1))
    sink = _lane_bcast(swa_sink)

    tabs_h = _rope_tables(seq, HEAD_DIM)
    tabs_m = _rope_tables(seq, MLA_ROPE_DIM)
    tabs_id = _identity_tables(t)

    cond = jnp.zeros((MOD_ROWS, d), F32).at[:bsz].set(c).at[bsz].set(c_ctx)
    mods = _ada_mod(cond, ada_w, ada_b).reshape(depth * MOD_ROWS, N_MOD, d)

    tm_x = 512 if seq % 512 == 0 else RET_CHUNK
    tm_f = 1024 if seq % 1024 == 0 else tm_x
    tm_p = 1024 if seq % 1024 == 0 else tm_x
    tm_c = t if t <= 512 else RET_CHUNK
    zeros_state = jnp.zeros((bsz, RET_HEADS, HEAD_DIM, HEAD_DIM), F32)

    x2 = x.reshape(bsz * seq, d)
    xc2 = ctx.reshape(bsz * t, d)
    for l in range(depth):
        last = l == depth - 1
        ret_c, swa_c, mla_c = _inproj(xc2, t, mods, w_in_b, tabs_id, l, bsz, bsz * t, tm_c)
        swa_c3 = swa_c.reshape(bsz, t, SWA_COLS)
        y_ret_c, st_f, st_b = _retention(ret_c.reshape(-1, bsz, t, HEAD_DIM), dec, l, zeros_state, zeros_state)
        q_c, k_c, v_c = _mla_project(mla_c, t, qn, kvn, wq, wk, wv, tabs_id, l, tm_c)
        k_c3 = k_c.reshape(bsz, t, -1)
        v_c3 = v_c.reshape(bsz, t, -1)

        ret_x, swa_x, mla_x = _inproj(x2, seq, mods, w_in_b, tabs_h, l, 0, seq, tm_x)
        y_ret, _, _ = _retention(ret_x.reshape(-1, bsz, seq, HEAD_DIM), dec, l, st_f, st_b)
        y_swa = _swa_latent(swa_x.reshape(bsz, seq, SWA_COLS), swa_c3, sink, l)
        q_x, k_x, v_x = _mla_project(mla_x, seq, qn, kvn, wq, wk, wv, tabs_m, l, tm_p)
        y_mla = _attention(q_x.reshape(bsz, seq, -1), [k_x.reshape(bsz, seq, -1), k_c3],
                           [v_x.reshape(bsz, seq, -1), v_c3], n_heads=MLA_HEADS, group=1,
                           dq=MLA_QK_PAD, dv=MLA_V_DIM, q_col0=0, k_col0=0, v_col0=0, q_scale=1.0)
        x_a = _outproj_ln(y_ret.reshape(bsz * seq, -1), y_swa.reshape(bsz * seq, -1),
                          y_mla.reshape(bsz * seq, -1), x2, mods, w_o_b, g1, b1, l, 0, seq, tm_x, alpha)
        x_new = _ffn(x_a, seq, mods, w_up_b, ffn_conv_w, conv_b, w_down_b, g2, b2, l, 0, seq, tm_f, alpha)

        if not last:
            y_swa_c = _attention(swa_c3, [swa_c3], [swa_c3], n_heads=SWA_HEADS, group=SWA_GROUP,
                                 dq=HEAD_DIM, dv=HEAD_DIM, q_col0=0, k_col0=SWA_HEADS,
                                 v_col0=SWA_HEADS + SWA_KV_HEADS, q_scale=1.0, sink=sink, layer=l)
            y_mla_c = _attention(q_c.reshape(bsz, t, -1), [k_c3], [v_c3], n_heads=MLA_HEADS, group=1,
                                 dq=MLA_QK_PAD, dv=MLA_V_DIM, q_col0=0, k_col0=0, v_col0=0, q_scale=1.0)
            xc_a = _outproj_ln(y_ret_c.reshape(bsz * t, -1), y_swa_c.reshape(bsz * t, -1),
                               y_mla_c.reshape(bsz * t, -1), xc2, mods, w_o_b, g1, b1, l, bsz, bsz * t,
                               tm_c, alpha)
            xc2 = _ffn(xc_a, t, mods, w_up_b, ffn_conv_w, conv_b, w_down_b, g2, b2, l, bsz, bsz * t,
                       tm_c, alpha)
        x2 = x_new
    return x2.reshape(bsz, seq, d)
```

```python
import functools

import jax
import jax.numpy as jnp
import numpy as np
from jax import lax
from jax.experimental import pallas as pl
from jax.experimental.pallas import tpu as pltpu

GRID_W = 64
HEAD_DIM = 128
ROPE_THETA = 10000.0
RET_HEADS = 4
RET_DIM = RET_HEADS * HEAD_DIM
RET_CHUNK = 128
SWA_HEADS = 6
SWA_KV_HEADS = 2
SWA_GROUP = SWA_HEADS // SWA_KV_HEADS
SWA_WINDOW = 128
MLA_HEADS = 6
MLA_Q_RANK = 512
MLA_KV_RANK = 256
MLA_NOPE_DIM = 128
MLA_ROPE_DIM = 64
MLA_V_DIM = 128
MLA_SCALE = (MLA_NOPE_DIM + MLA_ROPE_DIM) ** -0.5
MLA_QK_PAD = 256
N_MOD = 6
LN_EPS = 1e-5
RMS_EPS = 1e-6
NEG_INF = -1e30
LOG2E = 1.4426950408889634

RET_COLS = 4 * RET_DIM
SWA_COLS = (SWA_HEADS + 2 * SWA_KV_HEADS) * HEAD_DIM
MLA_COLS = MLA_Q_RANK + MLA_KV_RANK + MLA_ROPE_DIM
MLA_COLS_PAD = 896
IN_WIDTH = RET_COLS + SWA_COLS + MLA_COLS
IN_WIDTH_PAD = RET_COLS + SWA_COLS + MLA_COLS_PAD
MIX_WIDTH = RET_DIM + SWA_HEADS * HEAD_DIM + MLA_HEADS * MLA_V_DIM

LANES = 128
SUBLANES = 8
MOD_ROWS = 8
VMEM_LIMIT = 56 * 1024 * 1024
HALO = 16

BF16 = jnp.bfloat16
F32 = jnp.float32


def _cparams(*sem):
    return pltpu.CompilerParams(dimension_semantics=sem, vmem_limit_bytes=VMEM_LIMIT)


def _resident(block, index_map):
    return pl.BlockSpec(block, index_map, pipeline_mode=pl.Buffered(1))


def _dot(a, b):
    return jnp.dot(a, b, preferred_element_type=F32)


def _dot_nt(a, b):
    return lax.dot_general(a, b, (((1,), (1,)), ((), ())), preferred_element_type=F32)


def _silu(x):
    return x * (1.0 / (1.0 + jnp.exp(-x)))


def _rope(x, cos, sin_lo, sin_hi, nq):
    w = x.shape[-1]
    return x * cos + pltpu.roll(x, w - nq, 1) * sin_lo + pltpu.roll(x, nq, 1) * sin_hi


def _layer_norm(z, g, b):
    mu = jnp.mean(z, axis=-1, keepdims=True)
    zc = z - mu
    var = jnp.mean(zc * zc, axis=-1, keepdims=True)
    return zc * lax.rsqrt(var + LN_EPS) * g + b


def _rms(x):
    return x * lax.rsqrt(jnp.mean(x * x, axis=-1, keepdims=True) + RMS_EPS)


def _log_sigmoid(x):
    return -(jnp.maximum(-x, 0.0) + jnp.log(1.0 + jnp.exp(-jnp.abs(x))))


def _ada_kernel(c_ref, w_ref, b_ref, o_ref):
    sc = _silu(c_ref[...]).astype(BF16)
    o_ref[...] = _dot(sc, w_ref[...].astype(BF16)) + b_ref[...]


def _ada_mod(cond, ada_w, ada_b):
    depth, d, n = ada_w.shape
    tn = next(c for c in (1024, 512, 256, LANES) if n % c == 0)
    return pl.pallas_call(
        _ada_kernel,
        grid=(depth, n // tn),
        in_specs=[
            pl.BlockSpec((MOD_ROWS, d), lambda l, j: (0, 0)),
            pl.BlockSpec((None, d, tn), lambda l, j: (l, 0, j)),
            pl.BlockSpec((None, 1, tn), lambda l, j: (l, 0, j)),
        ],
        out_specs=pl.BlockSpec((None, MOD_ROWS, tn), lambda l, j: (l, 0, j)),
        out_shape=jax.ShapeDtypeStruct((depth, MOD_ROWS, n), F32),
        name="ada_mod",
        compiler_params=_cparams("parallel", "parallel"),
    )(cond, ada_w, ada_b.reshape(depth, 1, n))


def _inproj_kernel(x_ref, mod_ref, w_ref, cos_ref, slo_ref, shi_ref, ret_ref, swa_ref, mla_ref):
    m = mod_ref[...]
    h = (x_ref[...] * (1.0 + m[1:2, :]) + m[0:1, :]).astype(BF16)
    cos, slo, shi = cos_ref[...], slo_ref[...], shi_ref[...]
    nq = HEAD_DIM // 4
    k_scale = HEAD_DIM ** -0.5

    def rope(t, c, scale):
        r = _rope(t[:, c:c + HEAD_DIM], cos, slo, shi, nq)
        return r if scale is None else r * scale

    def rope_heads(t, out_ref, col0, n_heads, scale):
        for hh in range(n_heads):
            c = col0 + hh * HEAD_DIM
            out_ref[:, c:c + HEAD_DIM] = rope(t, c, scale).astype(out_ref.dtype)

    ret = _dot(h, w_ref[:, 0:RET_COLS])
    for j in range(RET_COLS // HEAD_DIM):
        c = j * HEAD_DIM
        if j < RET_HEADS:
            ret_ref[j] = rope(ret, c, None)
        elif j < 2 * RET_HEADS:
            ret_ref[j] = rope(ret, c, k_scale)
        else:
            ret_ref[j] = ret[:, c:c + HEAD_DIM]
    swa = _dot(h, w_ref[:, RET_COLS:RET_COLS + SWA_COLS])
    sq_cols = SWA_HEADS * HEAD_DIM
    sk_cols = SWA_KV_HEADS * HEAD_DIM
    rope_heads(swa, swa_ref, 0, SWA_HEADS, k_scale * LOG2E)
    rope_heads(swa, swa_ref, sq_cols, SWA_KV_HEADS, None)
    swa_ref[:, sq_cols + sk_cols:SWA_COLS] = swa[:, sq_cols + sk_cols:SWA_COLS].astype(swa_ref.dtype)
    mla_ref[...] = _dot(h, w_ref[:, RET_COLS + SWA_COLS:IN_WIDTH_PAD])


def _inproj(x2, seq, mods, w_in, tabs, layer, mod_row0, rows_per_mod, tm):
    m_rows, d = x2.shape
    mod_idx = lambda i: (layer * MOD_ROWS + mod_row0 + (i * tm) // rows_per_mod, 0, 0)
    tpb = seq // tm
    tab = pl.BlockSpec((tm, LANES), lambda i: (i % tpb, 0))
    return pl.pallas_call(
        _inproj_kernel,
        grid=(m_rows // tm,),
        in_specs=[
            pl.BlockSpec((tm, d), lambda i: (i, 0)),
            pl.BlockSpec((None, N_MOD, d), mod_idx),
            _resident((None, d, IN_WIDTH_PAD), lambda i: (layer, 0, 0)),
            tab, tab, tab,
        ],
        out_specs=[
            pl.BlockSpec((RET_COLS // HEAD_DIM, tm, HEAD_DIM), lambda i: (0, i, 0)),
            pl.BlockSpec((tm, SWA_COLS), lambda i: (i, 0)),
            pl.BlockSpec((tm, MLA_COLS_PAD), lambda i: (i, 0)),
        ],
        out_shape=[
            jax.ShapeDtypeStruct((RET_COLS // HEAD_DIM, m_rows, HEAD_DIM), F32),
            jax.ShapeDtypeStruct((m_rows, SWA_COLS), BF16),
            jax.ShapeDtypeStruct((m_rows, MLA_COLS_PAD), F32),
        ],
        name="inproj",
        compiler_params=_cparams("parallel"),
    )(x2, mods, w_in, *tabs)


def _decay_terms(dec_ref):
    lg = _log_sigmoid(dec_ref[...])
    return lg[0, 0:1, 0:1], lg[1, 0:1, 0:1]


def _ret_bwd_kernel(k_ref, v_ref, dec_ref, s0_ref, sin_ref, sfin_ref, state, *, cpb):
    n = pl.program_id(2)
    c = RET_CHUNK

    @pl.when(n == 0)
    def _():
        state[...] = s0_ref[...]

    _, lg_b = _decay_terms(dec_ref)
    pos = lax.broadcasted_iota(jnp.int32, (c, 1), 0).astype(F32)
    kdec = jnp.exp(lg_b * pos)
    cdec = jnp.exp(lg_b * float(c))
    kv = []
    for ci in range(cpb):
        rows = pl.ds(ci * c, c)
        kv.append(_dot((k_ref[rows, :] * kdec).T.astype(BF16), v_ref[rows, :].astype(BF16)))
    st = state[...]
    for ci in reversed(range(cpb)):
        sin_ref[ci] = st.astype(sin_ref.dtype)
        st = cdec * st + kv[ci]
    state[...] = st

    @pl.when(n == pl.num_programs(2) - 1)
    def _():
        sfin_ref[...] = st


def _ret_fwd_kernel(q_ref, k_ref, v_ref, g_ref, dec_ref, s0_ref, sb_ref, y_ref, sfin_ref, state, *, cpb):
    n = pl.program_id(2)
    c = RET_CHUNK

    @pl.when(n == 0)
    def _():
        state[...] = s0_ref[...]

    lg_f, lg_b = _decay_terms(dec_ref)
    pos = lax.broadcasted_iota(jnp.int32, (c, 1), 0).astype(F32)
    ri = lax.broadcasted_iota(jnp.int32, (c, c), 0)
    cj = lax.broadcasted_iota(jnp.int32, (c, c), 1)
    diff = (ri - cj).astype(F32)
    intra = (jnp.where(diff >= 0, jnp.exp(lg_f * jnp.maximum(diff, 0.0)), 0.0)
             + jnp.where(diff <= 0, jnp.exp(lg_b * jnp.maximum(-diff, 0.0)), 0.0))
    qdec_f = jnp.exp(lg_f * (pos + 1.0))
    qdec_b = jnp.exp(lg_b * (float(c) - pos))
    kdec_f = jnp.exp(lg_f * (float(c) - 1.0 - pos))
    cdec_f = jnp.exp(lg_f * float(c))
    lhs, vs, kv = [], [], []
    for ci in range(cpb):
        rows = pl.ds(ci * c, c)
        q, k = q_ref[rows, :], k_ref[rows, :]
        v = v_ref[rows, :].astype(BF16)
        scores = _dot_nt(q.astype(BF16), k.astype(BF16)) * intra
        lhs.append(jnp.concatenate([scores.astype(BF16), (q * qdec_f).astype(BF16),
                                    (q * qdec_b).astype(BF16)], axis=1))
        vs.append(v)
        kv.append(_dot((k * kdec_f).T.astype(BF16), v))
    st = state[...]
    for ci in range(cpb):
        rows = pl.ds(ci * c, c)
        rhs = jnp.concatenate([vs[ci], st.astype(BF16), sb_ref[ci].astype(BF16)], axis=0)
        out = _dot(lhs[ci], rhs)
        y_ref[rows, :] = (_silu(g_ref[rows, :]) * _rms(out)).astype(y_ref.dtype)
        st = cdec_f * st + kv[ci]
    state[...] = st

    @pl.when(n == pl.num_programs(2) - 1)
    def _():
        sfin_ref[...] = st


def _retention(ret3, dec, layer, s0_f, s0_b):
    _, bsz, seq, _ = ret3.shape
    c = RET_CHUNK
    cpb = next(n for n in (16, 8, 4, 2, 1) if seq % (n * c) == 0)
    rows = cpb * c
    nb = seq // rows
    h = RET_HEADS
    grid = (bsz, h, nb)
    state_spec = pl.BlockSpec((None, None, HEAD_DIM, HEAD_DIM), lambda b, hh, n: (b, hh, 0, 0))
    dec_spec = pl.BlockSpec((None, 2, None, SUBLANES, LANES), lambda b, hh, n: (layer, 0, hh, 0, 0))

    def col(j, rev):
        if rev:
            return pl.BlockSpec((None, None, rows, HEAD_DIM), lambda b, hh, n: (j * h + hh, b, nb - 1 - n, 0))
        return pl.BlockSpec((None, None, rows, HEAD_DIM), lambda b, hh, n: (j * h + hh, b, n, 0))

    sb_in, st_b = pl.pallas_call(
        functools.partial(_ret_bwd_kernel, cpb=cpb),
        grid=grid,
        in_specs=[col(1, True), col(2, True), dec_spec, state_spec],
        out_specs=[
            pl.BlockSpec((None, None, cpb, HEAD_DIM, HEAD_DIM), lambda b, hh, n: (b, hh, nb - 1 - n, 0, 0)),
            state_spec,
        ],
        out_shape=[
            jax.ShapeDtypeStruct((bsz, h, seq // c, HEAD_DIM, HEAD_DIM), BF16),
            jax.ShapeDtypeStruct((bsz, h, HEAD_DIM, HEAD_DIM), F32),
        ],
        scratch_shapes=[pltpu.VMEM((HEAD_DIM, HEAD_DIM), F32)],
        name="ret_bwd",
        compiler_params=_cparams("parallel", "parallel", "arbitrary"),
    )(ret3, ret3, dec, s0_b)

    y, st_f = pl.pallas_call(
        functools.partial(_ret_fwd_kernel, cpb=cpb),
        grid=grid,
        in_specs=[col(0, False), col(1, False), col(2, False), col(3, False), dec_spec, state_spec,
                  pl.BlockSpec((None, None, cpb, HEAD_DIM, HEAD_DIM), lambda b, hh, n: (b, hh, n, 0, 0))],
        out_specs=[
            pl.BlockSpec((None, rows, HEAD_DIM), lambda b, hh, n: (b, n, hh)),
            state_spec,
        ],
        out_shape=[
            jax.ShapeDtypeStruct((bsz, seq, RET_DIM), BF16),
            jax.ShapeDtypeStruct((bsz, h, HEAD_DIM, HEAD_DIM), F32),
        ],
        scratch_shapes=[pltpu.VMEM((HEAD_DIM, HEAD_DIM), F32)],
        name="ret_fwd",
        compiler_params=_cparams("parallel", "parallel", "arbitrary"),
    )(ret3, ret3, ret3, ret3, dec, s0_f, sb_in)
    return y, st_f, st_b


def _swa_kernel(q_ref, k_ref, v_ref, kp_ref, vp_ref, kn_ref, vn_ref, kc_ref, vc_ref, sink_ref, y_ref,
                *, tq, seq):
    i = pl.program_id(1)
    w = SWA_WINDOW
    nloc = tq + 2 * w
    q_pos = i * tq + lax.broadcasted_iota(jnp.int32, (tq, nloc), 0)
    k_pos = i * tq - w + lax.broadcasted_iota(jnp.int32, (tq, nloc), 1)
    valid = (jnp.abs(k_pos - q_pos) <= w) & (k_pos >= 0) & (k_pos < seq)
    nk = nloc + kc_ref.shape[0]
    ones = jnp.ones((nk, HEAD_DIM), BF16)
    for kv in range(SWA_KV_HEADS):
        ks = pl.ds(kv * HEAD_DIM, HEAD_DIM)
        kall = jnp.concatenate([kp_ref[:, ks], k_ref[:, ks], kn_ref[:, ks], kc_ref[:, ks]], axis=0)
        vall = jnp.concatenate(
            [jnp.concatenate([vp_ref[:, ks], v_ref[:, ks], vn_ref[:, ks], vc_ref[:, ks]], axis=0), ones], axis=1)
        for g in range(SWA_GROUP):
            hq = kv * SWA_GROUP + g
            hs = pl.ds(hq * HEAD_DIM, HEAD_DIM)
            s = _dot_nt(q_ref[:, hs], kall)
            s = jnp.concatenate([jnp.where(valid, s[:, 0:nloc], NEG_INF), s[:, nloc:]], axis=1)
            sink = sink_ref[hq, 0:1, 0:1] * LOG2E
            m = jnp.maximum(jnp.max(s, axis=-1, keepdims=True), sink)
            acc = _dot(jnp.exp2(s - m).astype(BF16), vall)
            den = acc[:, HEAD_DIM:2 * HEAD_DIM] + jnp.exp2(sink - m)
            y_ref[:, hs] = (acc[:, 0:HEAD_DIM] / den).astype(y_ref.dtype)


def _swa_latent(swa3, swa_ctx3, sink, layer):
    bsz, seq, _ = swa3.shape
    t = swa_ctx3.shape[1]
    w = SWA_WINDOW
    tq = 256 if seq % 256 == 0 else w
    r = tq // w
    nblk = seq // w
    kvw = SWA_KV_HEADS * HEAD_DIM
    qw = SWA_HEADS * HEAD_DIM
    k_col, v_col = qw // kvw, qw // kvw + 1
    prev = lambda i: jnp.maximum(i * r - 1, 0)
    nxt = lambda i: jnp.minimum((i + 1) * r, nblk - 1)
    return pl.pallas_call(
        functools.partial(_swa_kernel, tq=tq, seq=seq),
        grid=(bsz, seq // tq),
        in_specs=[
            pl.BlockSpec((None, tq, qw), lambda b, i: (b, i, 0)),
            pl.BlockSpec((None, tq, kvw), lambda b, i: (b, i, k_col)),
            pl.BlockSpec((None, tq, kvw), lambda b, i: (b, i, v_col)),
            pl.BlockSpec((None, w, kvw), lambda b, i: (b, prev(i), k_col)),
            pl.BlockSpec((None, w, kvw), lambda b, i: (b, prev(i), v_col)),
            pl.BlockSpec((None, w, kvw), lambda b, i: (b, nxt(i), k_col)),
            pl.BlockSpec((None, w, kvw), lambda b, i: (b, nxt(i), v_col)),
            pl.BlockSpec((None, t, kvw), lambda b, i: (b, 0, k_col)),
            pl.BlockSpec((None, t, kvw), lambda b, i: (b, 0, v_col)),
            pl.BlockSpec((None, SWA_HEADS, SUBLANES, LANES), lambda b, i: (layer, 0, 0, 0)),
        ],
        out_specs=pl.BlockSpec((None, tq, qw), lambda b, i: (b, i, 0)),
        out_shape=jax.ShapeDtypeStruct((bsz, seq, qw), BF16),
        name="swa_latent",
        compiler_params=_cparams("parallel", "parallel"),
    )(swa3, swa3, swa3, swa3, swa3, swa3, swa3, swa_ctx3, swa_ctx3, sink)


def _mla_proj_kernel(x_ref, qn_ref, kvn_ref, wq_ref, wk_ref, wv_ref, cos_ref, slo_ref, shi_ref,
                     q_ref, k_ref, v_ref):
    nq = MLA_ROPE_DIM // 4
    cos, slo, shi = cos_ref[...], slo_ref[...], shi_ref[...]
    cq = (_rms(x_ref[:, 0:MLA_Q_RANK]) * qn_ref[...]).astype(BF16)
    q = _dot(cq, wq_ref[...]) * (MLA_SCALE * LOG2E)
    ckv =(_rms(x_ref[:, MLA_Q_RANK:MLA_Q_RANK + MLA_KV_RANK]) * kvn_ref[...]).astype(BF16)
    kn = _dot(ckv, wk_ref[...])
    v_ref[...] = _dot(ckv, wv_ref[...]).astype(v_ref.dtype)
    kr = _rope(x_ref[:, MLA_Q_RANK + MLA_KV_RANK:MLA_COLS_PAD], cos, slo, shi, nq).astype(k_ref.dtype)
    for h in range(MLA_HEADS):
        a = h * MLA_QK_PAD
        q_ref[:, a:a + MLA_NOPE_DIM] = q[:, a:a + MLA_NOPE_DIM].astype(q_ref.dtype)
        q_ref[:, a + MLA_NOPE_DIM:a + MLA_QK_PAD] = _rope(
            q[:, a + MLA_NOPE_DIM:a + MLA_QK_PAD], cos, slo, shi, nq).astype(q_ref.dtype)
        k_ref[:, a:a + MLA_NOPE_DIM] = kn[:, h * MLA_NOPE_DIM:(h + 1) * MLA_NOPE_DIM].astype(k_ref.dtype)
        k_ref[:, a + MLA_NOPE_DIM:a + MLA_QK_PAD] = kr


def _mla_project(mla2, seq, qn, kvn, wq, wk, wv, tabs, layer, tm):
    m_rows = mla2.shape[0]
    cos, slo, shi = tabs
    tpb = seq // tm
    qkw = MLA_HEADS * MLA_QK_PAD
    vw = MLA_HEADS * MLA_V_DIM
    lay = lambda i: (layer, 0, 0)
    tab = pl.BlockSpec((tm, LANES), lambda i: (i % tpb, 0))
    return pl.pallas_call(
        _mla_proj_kernel,
        grid=(m_rows // tm,),
        in_specs=[
            pl.BlockSpec((tm, MLA_COLS_PAD), lambda i: (i, 0)),
            pl.BlockSpec((None, 1, MLA_Q_RANK), lay),
            pl.BlockSpec((None, 1, MLA_KV_RANK), lay),
            pl.BlockSpec((None, MLA_Q_RANK, qkw), lay),
            pl.BlockSpec((None, MLA_KV_RANK, vw), lay),
            pl.BlockSpec((None, MLA_KV_RANK, vw), lay),
            tab, tab, tab,
        ],
        out_specs=[
            pl.BlockSpec((tm, qkw), lambda i: (i, 0)),
            pl.BlockSpec((tm, qkw), lambda i: (i, 0)),
            pl.BlockSpec((tm, vw), lambda i: (i, 0)),
        ],
        out_shape=[
            jax.ShapeDtypeStruct((m_rows, qkw), BF16),
            jax.ShapeDtypeStruct((m_rows, qkw), BF16),
            jax.ShapeDtypeStruct((m_rows, vw), BF16),
        ],
        name="mla_project",
        compiler_params=_cparams("parallel"),
    )(mla2, qn, kvn, wq, wk, wv, cos, slo, shi)


def _attn_kernel(*refs, n_src, q_scale, has_sink, nq):
    q_ref = refs[0]
    k_refs = refs[1:1 + n_src]
    v_refs = refs[1 + n_src:1 + 2 * n_src]
    pos = 1 + 2 * n_src
    sink_ref = refs[pos] if has_sink else None
    sink_b_ref = refs[pos + 1] if has_sink else None
    o_ref = refs[pos + (2 if has_sink else 0)]
    scratch = refs[pos + (3 if has_sink else 1):]
    s_refs = scratch[0:n_src]
    va_refs = scratch[n_src:2 * n_src]
    mb_ref = scratch[2 * n_src]
    t = pl.program_id(0)
    tq = q_ref.shape[0]
    dv = o_ref.shape[-1]
    tks = [s_ref.shape[-1] for s_ref in s_refs]

    @pl.when(t == 0)
    def _():
        for s_ref in s_refs:
            s_ref[...] = jnp.zeros(s_ref.shape, F32)
        mb_ref[...] = jnp.zeros(mb_ref.shape, F32)

    @pl.when(jnp.maximum(t - 1, 0) % nq == 0)
    def _():
        for v_ref, va in zip(v_refs, va_refs):
            va[:, 0:dv] = v_ref[...].astype(BF16)
            va[:, dv:2 * dv] = jnp.ones((va.shape[0], dv), BF16)

    q = q_ref[...]
    if q_scale != 1.0:
        q = q.astype(F32) * q_scale
    q = q.astype(BF16)
    m_prev = mb_ref[...]

    def lane_max(m_vec, s):
        for c in range(s.shape[-1] // LANES):
            m_vec = jnp.maximum(m_vec, s[:, c * LANES:(c + 1) * LANES])
        return m_vec

    m_vec = jnp.full((tq, LANES), NEG_INF, F32)
    acc = jnp.zeros((tq, 2 * dv), F32)
    for k_ref, s_ref, va, tkk in zip(k_refs, s_refs, va_refs, tks):
        for j in range(s_ref.shape[0]):
            rows = pl.ds(j * tkk, tkk)
            p = jnp.exp2(s_ref[j] - jnp.concatenate([m_prev] * (tkk // LANES), axis=1))
            acc = acc + _dot(p.astype(BF16), va[rows, :])
            s = _dot_nt(q, k_ref[rows, :].astype(BF16))
            s_ref[j] = s
            m_vec = lane_max(m_vec, s)

    l = acc[:, dv:2 * dv]
    if has_sink:
        l = l + jnp.exp2(sink_b_ref[0:1, 0:1] * LOG2E - m_prev)
    o_ref[...] = (acc[:, 0:dv] / l).astype(o_ref.dtype)

    m_row = jnp.max(m_vec, axis=-1, keepdims=True)
    if has_sink:
        m_row = jnp.maximum(m_row, sink_ref[0:1, 0:1] * LOG2E)
    mb_ref[...] = jnp.broadcast_to(m_row, (tq, LANES))


def _attention(q3, ks, vs, *, n_heads, group, dq, dv, q_col0, k_col0, v_col0, q_scale, sink=None, layer=0,
               tq=512, tk=512):
    bsz, lq, _ = q3.shape
    tq = min(tq, lq)
    assert dv == LANES
    n_src = len(ks)
    tks = [min(tk, k.shape[1]) for k in ks]
    nq = lq // tq
    n_blocks = bsz * n_heads * nq

    def cur(t):
        t = jnp.minimum(t, n_blocks - 1)
        return t // (n_heads * nq), (t // nq) % n_heads, t % nq

    def prev(t):
        return cur(jnp.maximum(t - 1, 0))

    def at(fn, spec):
        def index_map(t):
            b, h, i = fn(t)
            return spec(b, h, i)
        return index_map

    in_specs = [pl.BlockSpec((None, tq, dq), at(cur, lambda b, h, i: (b, i, q_col0 + h)))]
    in_specs += [pl.BlockSpec((None, k.shape[1], dq), at(cur, lambda b, h, i: (b, 0, k_col0 + h // group)))
                 for k in ks]
    in_specs += [pl.BlockSpec((None, v.shape[1], dv), at(prev, lambda b, h, i: (b, 0, v_col0 + h // group)))
                 for v in vs]
    args = [q3] + list(ks) + list(vs)
    if sink is not None:
        for fn in (cur, prev):
            in_specs.append(pl.BlockSpec((None, None, SUBLANES, LANES),
                                         at(fn, lambda b, h, i: (layer, h, 0, 0))))
            args.append(sink)
    scratch = [pltpu.VMEM((k.shape[1] // tkk, tq, tkk), F32) for k, tkk in zip(ks, tks)]
    scratch += [pltpu.VMEM((v.shape[1], 2 * dv), BF16) for v in vs]
    scratch += [pltpu.VMEM((tq, LANES), F32)]
    return pl.pallas_call(
        functools.partial(_attn_kernel, n_src=n_src, q_scale=q_scale, has_sink=sink is not None, nq=nq),
        grid=(n_blocks + 1,),
        in_specs=in_specs,
        out_specs=pl.BlockSpec((None, tq, dv), at(prev, lambda b, h, i: (b, i, h))),
        out_shape=jax.ShapeDtypeStruct((bsz, lq, n_heads * dv), BF16),
        scratch_shapes=scratch,
        name="attention",
        compiler_params=_cparams("arbitrary"),
    )(*args)


def _outproj_kernel(yr_ref, ys_ref, ym_ref, x_ref, mod_ref, w_ref, g_ref, b_ref, o_ref, *, alpha):
    a, b = RET_DIM, RET_DIM + SWA_HEADS * HEAD_DIM
    mix = (_dot(yr_ref[...], w_ref[0:a, :]) + _dot(ys_ref[...], w_ref[a:b, :])
           + _dot(ym_ref[...], w_ref[b:MIX_WIDTH, :]))
    z = alpha * x_ref[...] + (1.0 + mod_ref[2:3, :]) * mix
    o_ref[...] = _layer_norm(z, g_ref[...], b_ref[...])


def _outproj_ln(y_ret, y_swa, y_mla, x2, mods, w_o, ln_g, ln_b, layer, mod_row0, rows_per_mod, tm, alpha):
    m_rows, d = x2.shape
    mod_idx = lambda i: (layer * MOD_ROWS + mod_row0 + (i * tm) // rows_per_mod, 0, 0)
    lay = lambda i: (layer, 0, 0)
    row = lambda w: pl.BlockSpec((tm, w), lambda i: (i, 0))
    return pl.pallas_call(
        functools.partial(_outproj_kernel, alpha=alpha),
        grid=(m_rows // tm,),
        in_specs=[
            row(RET_DIM), row(SWA_HEADS * HEAD_DIM), row(MLA_HEADS * MLA_V_DIM), row(d),
            pl.BlockSpec((None, N_MOD, d), mod_idx),
            _resident((None, MIX_WIDTH, d), lay),
            pl.BlockSpec((None, 1, d), lay),
            pl.BlockSpec((None, 1, d), lay),
        ],
        out_specs=row(d),
        out_shape=jax.ShapeDtypeStruct((m_rows, d), F32),
        name="outproj_ln",
        compiler_params=_cparams("parallel"),
    )(y_ret, y_swa, y_mla, x2, mods, w_o, ln_g, ln_b)


def _ffn_kernel(x_ref, xp_ref, xn_ref, mod_ref, wu_ref, wg_ref, cw_ref, cb_ref, wd_ref, g_ref, b_ref,
                o_ref, h_ext, g_ext, *, tm, seq, alpha):
    i = pl.program_id(0)
    f = pl.program_id(1)

    @pl.when(f == 0)
    def _():
        scale = 1.0 + mod_ref[4:5, :]
        shift = mod_ref[3:4, :]
        has_prev = jnp.where((i * tm) % seq != 0, 1.0, 0.0).astype(F32)
        has_next = jnp.where(((i + 1) * tm) % seq != 0, 1.0, 0.0).astype(F32)
        h_ext[0:HALO, :] = ((xp_ref[...] * scale + shift) * has_prev).astype(BF16)
        h_ext[HALO:HALO + tm, :] = (x_ref[...] * scale + shift).astype(BF16)
        h_ext[HALO + tm:tm + 2 * HALO, :] = ((xn_ref[...] * scale + shift) * has_next).astype(BF16)
        o_ref[...] = jnp.zeros_like(o_ref)

    g_ext[...] = _dot(h_ext[...], wg_ref[...])
    u = _dot(h_ext[HALO:HALO + tm, :], wu_ref[...])
    gc =(g_ext[HALO - 1:HALO - 1 + tm, :] * cw_ref[0:1, :] + g_ext[HALO:HALO + tm, :] * cw_ref[1:2, :]
          + g_ext[HALO + 1:HALO + 1 + tm, :] * cw_ref[2:3, :] + cb_ref[...])
    a = (_silu(gc) * u).astype(BF16)
    o_ref[...] += _dot(a, wd_ref[...])

    @pl.when(f == pl.num_programs(1) - 1)
    def _():
        xa = x_ref[...]
        z = alpha * xa + (1.0 + mod_ref[5:6, :]) * o_ref[...]
        o_ref[...] = _layer_norm(z, g_ref[...], b_ref[...])


def _ffn_tile(dff):
    return 512 if dff % 512 == 0 else (256 if dff % 256 == 0 else LANES)


def _ffn(x2, seq, mods, w_up, conv_w, conv_b, w_down, ln_g, ln_b, layer, mod_row0, rows_per_mod, tm, alpha):
    m_rows, d = x2.shape
    dff = w_down.shape[1]
    tf = _ffn_tile(dff)
    nf = dff // tf
    hb = tm // HALO
    nhb = m_rows // HALO
    mod_idx = lambda i, f: (layer * MOD_ROWS + mod_row0 + (i * tm) // rows_per_mod, 0, 0)
    lay = lambda i, f: (layer, 0, 0)
    return pl.pallas_call(
        functools.partial(_ffn_kernel, tm=tm, seq=seq, alpha=alpha),
        grid=(m_rows // tm, nf),
        in_specs=[
            pl.BlockSpec((tm, d), lambda i, f: (i, 0)),
            pl.BlockSpec((HALO, d), lambda i, f: (jnp.maximum(i * hb - 1, 0), 0)),
            pl.BlockSpec((HALO, d), lambda i, f: (jnp.minimum((i + 1) * hb, nhb - 1), 0)),
            pl.BlockSpec((None, N_MOD, d), mod_idx),
            pl.BlockSpec((None, None, d, tf), lambda i, f: (layer, f, 0, 0)),
            pl.BlockSpec((None, None, d, tf), lambda i, f: (layer, nf + f, 0, 0)),
            pl.BlockSpec((None, 3, tf), lambda i, f: (layer, 0, f)),
            pl.BlockSpec((None, 1, tf), lambda i, f: (layer, 0, f)),
            pl.BlockSpec((None, tf, d), lambda i, f: (layer, f, 0)),
            pl.BlockSpec((None, 1, d), lay),
            pl.BlockSpec((None, 1, d), lay),
        ],
        out_specs=pl.BlockSpec((tm, d), lambda i, f: (i, 0)),
        out_shape=jax.ShapeDtypeStruct((m_rows, d), F32),
        scratch_shapes=[
            pltpu.VMEM((tm + 2 * HALO, d), BF16),
            pltpu.VMEM((tm + 2 * HALO, tf), F32),
        ],
        name="conv_ffn",
        compiler_params=_cparams("parallel", "arbitrary"),
    )(x2, x2, x2, mods, w_up, w_up, conv_w, conv_b, w_down, ln_g, ln_b)


def _rope_tables(n_tokens, dim):
    rows = n_tokens // GRID_W
    r = np.repeat(np.arange(rows, dtype=np.float32), GRID_W)
    cc = np.tile(np.arange(GRID_W, dtype=np.float32), rows)
    n_freq = dim // 4
    inv = jnp.asarray(ROPE_THETA, F32) ** (-jnp.arange(n_freq, dtype=F32) / n_freq)
    ang_r = jnp.asarray(r)[:, None] * inv
    ang_c = jnp.asarray(cc)[:, None] * inv
    ang = jnp.concatenate([ang_r, ang_r, ang_c, ang_c], axis=-1)
    cos, sin = jnp.cos(ang), jnp.sin(ang)
    lane = np.arange(dim)
    lo = jnp.asarray((lane % (2 * n_freq)) < n_freq)
    sin_lo = jnp.where(lo, -sin, 0.0)
    sin_hi = jnp.where(lo, 0.0, sin)
    pad = LANES - dim
    if pad:
        cos = jnp.pad(cos, ((0, 0), (0, pad)), constant_values=1.0)
        sin_lo = jnp.pad(sin_lo, ((0, 0), (0, pad)))
        sin_hi = jnp.pad(sin_hi, ((0, 0), (0, pad)))
    return cos, sin_lo, sin_hi


def _identity_tables(n_tokens):
    return (jnp.ones((n_tokens, LANES), F32), jnp.zeros((n_tokens, LANES), F32),
            jnp.zeros((n_tokens, LANES), F32))


def _lane_bcast(p):
    return jnp.broadcast_to(p.astype(F32)[..., None, None], p.shape + (SUBLANES, LANES))


def kernel(x, c, ctx, c_ctx, ada_w, ada_b, w_in, ret_decay_fwd, ret_decay_bwd, swa_sink, mla_q_norm, mla_w_uq,
           mla_kv_norm, mla_w_ukv, w_o, ln1_g, ln1_b, ffn_w_up, ffn_conv_w, ffn_conv_b, ffn_w_down, ln2_g, ln2_b):
    bsz, seq, d = x.shape
    t = ctx.shape[1]
    depth = w_in.shape[0]
    assert bsz + 1 <= MOD_ROWS and seq % RET_CHUNK == 0 and t % RET_CHUNK == 0
    assert seq % GRID_W == 0 and d % LANES == 0
    alpha = (2 * depth) ** 0.25

    w_in_b = jnp.pad(w_in, ((0, 0), (0, 0), (0, IN_WIDTH_PAD - IN_WIDTH))).astype(BF16)
    w_o_b = w_o.astype(BF16)
    tf = _ffn_tile(ffn_w_down.shape[1])
    w_up_b = ffn_w_up.astype(BF16).reshape(depth, d, -1, tf).transpose(0, 2, 1, 3)
    w_down_b = ffn_w_down.astype(BF16)
    hq = MLA_NOPE_DIM + MLA_ROPE_DIM
    wq = mla_w_uq.reshape(depth, MLA_Q_RANK, MLA_HEADS, hq)
    wq = jnp.pad(wq, ((0, 0), (0, 0), (0, 0), (0, MLA_QK_PAD - hq)))
    wq = wq.reshape(depth, MLA_Q_RANK, MLA_HEADS * MLA_QK_PAD).astype(BF16)
    wkv = mla_w_ukv.reshape(depth, MLA_KV_RANK, MLA_HEADS, MLA_NOPE_DIM + MLA_V_DIM)
    wk = wkv[..., :MLA_NOPE_DIM].reshape(depth, MLA_KV_RANK, MLA_HEADS * MLA_NOPE_DIM).astype(BF16)
    wv = wkv[..., MLA_NOPE_DIM:].reshape(depth, MLA_KV_RANK, MLA_HEADS * MLA_V_DIM).astype(BF16)
    qn = mla_q_norm.reshape(depth, 1, MLA_Q_RANK)
    kvn = mla_kv_norm.reshape(depth, 1, MLA_KV_RANK)
    g1, b1 = ln1_g.reshape(depth, 1, d), ln1_b.reshape(depth, 1, d)
    g2, b2 = ln2_g.reshape(depth, 1, d), ln2_b.reshape(depth, 1, d)
    conv_b = ffn_conv_b.reshape(depth, 1, -1)
    dec = _lane_bcast(jnp.stack([ret_decay_fwd, ret_decay_bwd], axis=1))
    sink = _lane_bcast(swa_sink)

    tabs_h = _rope_tables(seq, HEAD_DIM)
    tabs_m = _rope_tables(seq, MLA_ROPE_DIM)
    tabs_id = _identity_tables(t)

    cond = jnp.zeros((MOD_ROWS, d), F32).at[:bsz].set(c).at[bsz].set(c_ctx)
    mods = _ada_mod(cond, ada_w, ada_b).reshape(depth * MOD_ROWS, N_MOD, d)

    tm_x = 512 if seq % 512 == 0 else RET_CHUNK
    tm_f = tm_x
    tm_p = 1024 if seq % 1024 == 0 else tm_x
    tm_c = t if t <= 512 else RET_CHUNK
    zeros_state = jnp.zeros((bsz, RET_HEADS, HEAD_DIM, HEAD_DIM), F32)

    x2 = x.reshape(bsz * seq, d)
    xc2 = ctx.reshape(bsz * t, d)
    for l in range(depth):
        last = l == depth - 1
        ret_c, swa_c, mla_c = _inproj(xc2, t, mods, w_in_b, tabs_id, l, bsz, bsz * t, tm_c)
        swa_c3 = swa_c.reshape(bsz, t, SWA_COLS)
        y_ret_c, st_f, st_b = _retention(ret_c.reshape(-1, bsz, t, HEAD_DIM), dec, l, zeros_state, zeros_state)
        q_c, k_c, v_c = _mla_project(mla_c, t, qn, kvn, wq, wk, wv, tabs_id, l, tm_c)
        k_c3 = k_c.reshape(bsz, t, -1)
        v_c3 = v_c.reshape(bsz, t, -1)

        ret_x, swa_x, mla_x = _inproj(x2, seq, mods, w_in_b, tabs_h, l, 0, seq, tm_x)
        y_ret, _, _ = _retention(ret_x.reshape(-1, bsz, seq, HEAD_DIM), dec, l, st_f, st_b)
        y_swa = _swa_latent(swa_x.reshape(bsz, seq, SWA_COLS), swa_c3, sink, l)
        q_x, k_x, v_x = _mla_project(mla_x, seq, qn, kvn, wq, wk, wv, tabs_m, l, tm_p)
        y_mla = _attention(q_x.reshape(bsz, seq, -1), [k_x.reshape(bsz, seq, -1), k_c3],
                           [v_x.reshape(bsz, seq, -1), v_c3], n_heads=MLA_HEADS, group=1,
                           dq=MLA_QK_PAD, dv=MLA_V_DIM, q_col0=0, k_col0=0, v_col0=0, q_scale=1.0)
        x_a = _outproj_ln(y_ret.reshape(bsz * seq, -1), y_swa.reshape(bsz * seq, -1),
                          y_mla.reshape(bsz * seq, -1), x2, mods, w_o_b, g1, b1, l, 0, seq, tm_x, alpha)
        x_new = _ffn(x_a, seq, mods, w_up_b, ffn_conv_w, conv_b, w_down_b, g2, b2, l, 0, seq, tm_f, alpha)

        if not last:
            y_swa_c = _attention(swa_c3, [swa_c3], [swa_c3], n_heads=SWA_HEADS, group=SWA_GROUP,
                                 dq=HEAD_DIM, dv=HEAD_DIM, q_col0=0, k_col0=SWA_HEADS,
                                 v_col0=SWA_HEADS + SWA_KV_HEADS, q_scale=1.0, sink=sink, layer=l)
            y_mla_c = _attention(q_c.reshape(bsz, t, -1), [k_c3], [v_c3], n_heads=MLA_HEADS, group=1,
                                 dq=MLA_QK_PAD, dv=MLA_V_DIM, q_col0=0, k_col0=0, v_col0=0, q_scale=1.0)
            xc_a = _outproj_ln(y_ret_c.reshape(bsz * t, -1), y_swa_c.reshape(bsz * t, -1),
                               y_mla_c.reshape(bsz * t, -1), xc2, mods, w_o_b, g1, b1, l, bsz, bsz * t,
                               tm_c, alpha)
            xc2 = _ffn(xc_a, t, mods, w_up_b, ffn_conv_w, conv_b, w_down_b, g2, b2, l, bsz, bsz * t,
                       tm_c, alpha)
        x2 = x_new
    return x2.reshape(bsz, seq, d)
```

```python
import functools

import jax
import jax.numpy as jnp
import numpy as np
from jax import lax
from jax.experimental import pallas as pl
from jax.experimental.pallas import tpu as pltpu

GRID_W = 64
HEAD_DIM = 128
ROPE_THETA = 10000.0
RET_HEADS = 4
RET_DIM = RET_HEADS * HEAD_DIM
RET_CHUNK = 128
SWA_HEADS = 6
SWA_KV_HEADS = 2
SWA_GROUP = SWA_HEADS // SWA_KV_HEADS
SWA_WINDOW = 128
MLA_HEADS = 6
MLA_Q_RANK = 512
MLA_KV_RANK = 256
MLA_NOPE_DIM = 128
MLA_ROPE_DIM = 64
MLA_V_DIM = 128
MLA_SCALE = (MLA_NOPE_DIM + MLA_ROPE_DIM) ** -0.5
MLA_QK_PAD = 256
N_MOD = 6
LN_EPS = 1e-5
RMS_EPS = 1e-6
NEG_INF = -1e30
LOG2E = 1.4426950408889634

RET_COLS = 4 * RET_DIM
SWA_COLS = (SWA_HEADS + 2 * SWA_KV_HEADS) * HEAD_DIM
MLA_COLS = MLA_Q_RANK + MLA_KV_RANK + MLA_ROPE_DIM
MLA_COLS_PAD = 896
IN_WIDTH = RET_COLS + SWA_COLS + MLA_COLS
IN_WIDTH_PAD = RET_COLS + SWA_COLS + MLA_COLS_PAD
MIX_WIDTH = RET_DIM + SWA_HEADS * HEAD_DIM + MLA_HEADS * MLA_V_DIM

LANES = 128
SUBLANES = 8
MOD_ROWS = 8
VMEM_LIMIT = 56 * 1024 * 1024
HALO = 16

BF16 = jnp.bfloat16
F32 = jnp.float32


def _cparams(*sem):
    return pltpu.CompilerParams(dimension_semantics=sem, vmem_limit_bytes=VMEM_LIMIT)


def _resident(block, index_map):
    return pl.BlockSpec(block, index_map, pipeline_mode=pl.Buffered(1))


def _dot(a, b):
    return jnp.dot(a, b, preferred_element_type=F32)


def _dot_nt(a, b):
    return lax.dot_general(a, b, (((1,), (1,)), ((), ())), preferred_element_type=F32)


def _silu(x):
    return x * (1.0 / (1.0 + jnp.exp(-x)))


def _rope(x, cos, sin_lo, sin_hi, nq):
    w = x.shape[-1]
    return x * cos + pltpu.roll(x, w - nq, 1) * sin_lo + pltpu.roll(x, nq, 1) * sin_hi


def _layer_norm(z, g, b):
    mu = jnp.mean(z, axis=-1, keepdims=True)
    zc = z - mu
    var = jnp.mean(zc * zc, axis=-1, keepdims=True)
    return zc * lax.rsqrt(var + LN_EPS) * g + b


def _rms(x):
    return x * lax.rsqrt(jnp.mean(x * x, axis=-1, keepdims=True) + RMS_EPS)


def _log_sigmoid(x):
    return -(jnp.maximum(-x, 0.0) + jnp.log(1.0 + jnp.exp(-jnp.abs(x))))


def _ada_kernel(c_ref, w_ref, b_ref, o_ref):
    sc = _silu(c_ref[...]).astype(BF16)
    o_ref[...] = _dot(sc, w_ref[...].astype(BF16)) + b_ref[...]


def _ada_mod(cond, ada_w, ada_b):
    depth, d, n = ada_w.shape
    tn = next(c for c in (1024, 512, 256, LANES) if n % c == 0)
    return pl.pallas_call(
        _ada_kernel,
        grid=(depth, n // tn),
        in_specs=[
            pl.BlockSpec((MOD_ROWS, d), lambda l, j: (0, 0)),
            pl.BlockSpec((None, d, tn), lambda l, j: (l, 0, j)),
            pl.BlockSpec((None, 1, tn), lambda l, j: (l, 0, j)),
        ],
        out_specs=pl.BlockSpec((None, MOD_ROWS, tn), lambda l, j: (l, 0, j)),
        out_shape=jax.ShapeDtypeStruct((depth, MOD_ROWS, n), F32),
        name="ada_mod",
        compiler_params=_cparams("parallel", "parallel"),
    )(cond, ada_w, ada_b.reshape(depth, 1, n))


def _inproj_kernel(x_ref, mod_ref, w_ref, cos_ref, slo_ref, shi_ref, ret_ref, swa_ref, mla_ref):
    m = mod_ref[...]
    h = (x_ref[...] * (1.0 + m[1:2, :]) + m[0:1, :]).astype(BF16)
    cos, slo, shi = cos_ref[...], slo_ref[...], shi_ref[...]
    nq = HEAD_DIM // 4
    k_scale = HEAD_DIM ** -0.5

    def rope(t, c, scale):
        r = _rope(t[:, c:c + HEAD_DIM], cos, slo, shi, nq)
        return r if scale is None else r * scale

    def rope_heads(t, out_ref, col0, n_heads, scale):
        for hh in range(n_heads):
            c = col0 + hh * HEAD_DIM
            out_ref[:, c:c + HEAD_DIM] = rope(t, c, scale).astype(out_ref.dtype)

    ret = _dot(h, w_ref[:, 0:RET_COLS])
    for j in range(RET_COLS // HEAD_DIM):
        c = j * HEAD_DIM
        if j < RET_HEADS:
            ret_ref[j] = rope(ret, c, None)
        elif j < 2 * RET_HEADS:
            ret_ref[j] = rope(ret, c, k_scale)
        else:
            ret_ref[j] = ret[:, c:c + HEAD_DIM]
    swa = _dot(h, w_ref[:, RET_COLS:RET_COLS + SWA_COLS])
    sq_cols = SWA_HEADS * HEAD_DIM
    sk_cols = SWA_KV_HEADS * HEAD_DIM
    rope_heads(swa, swa_ref, 0, SWA_HEADS, k_scale * LOG2E)
    rope_heads(swa, swa_ref, sq_cols, SWA_KV_HEADS, None)
    swa_ref[:, sq_cols + sk_cols:SWA_COLS] = swa[:, sq_cols + sk_cols:SWA_COLS].astype(swa_ref.dtype)
    mla_ref[...] = _dot(h, w_ref[:, RET_COLS + SWA_COLS:IN_WIDTH_PAD])


def _inproj(x2, seq, mods, w_in, tabs, layer, mod_row0, rows_per_mod, tm):
    m_rows, d = x2.shape
    mod_idx = lambda i: (layer * MOD_ROWS + mod_row0 + (i * tm) // rows_per_mod, 0, 0)
    tpb = seq // tm
    tab = pl.BlockSpec((tm, LANES), lambda i: (i % tpb, 0))
    return pl.pallas_call(
        _inproj_kernel,
        grid=(m_rows // tm,),
        in_specs=[
            pl.BlockSpec((tm, d), lambda i: (i, 0)),
            pl.BlockSpec((None, N_MOD, d), mod_idx),
            _resident((None, d, IN_WIDTH_PAD), lambda i: (layer, 0, 0)),
            tab, tab, tab,
        ],
        out_specs=[
            pl.BlockSpec((RET_COLS // HEAD_DIM, tm, HEAD_DIM), lambda i: (0, i, 0)),
            pl.BlockSpec((tm, SWA_COLS), lambda i: (i, 0)),
            pl.BlockSpec((tm, MLA_COLS_PAD), lambda i: (i, 0)),
        ],
        out_shape=[
            jax.ShapeDtypeStruct((RET_COLS // HEAD_DIM, m_rows, HEAD_DIM), F32),
            jax.ShapeDtypeStruct((m_rows, SWA_COLS), BF16),
            jax.ShapeDtypeStruct((m_rows, MLA_COLS_PAD), F32),
        ],
        name="inproj",
        compiler_params=_cparams("parallel"),
    )(x2, mods, w_in, *tabs)


def _decay_terms(dec_ref):
    lg = _log_sigmoid(dec_ref[...])
    return lg[0, 0:1, 0:1], lg[1, 0:1, 0:1]


def _ret_bwd_kernel(k_ref, v_ref, dec_ref, s0_ref, sin_ref, sfin_ref, state, *, cpb):
    n = pl.program_id(2)
    c = RET_CHUNK

    @pl.when(n == 0)
    def _():
        state[...] = s0_ref[...]

    _, lg_b = _decay_terms(dec_ref)
    pos = lax.broadcasted_iota(jnp.int32, (c, 1), 0).astype(F32)
    kdec = jnp.exp(lg_b * pos)
    cdec = jnp.exp(lg_b * float(c))
    kv = []
    for ci in range(cpb):
        rows = pl.ds(ci * c, c)
        kv.append(_dot((k_ref[rows, :] * kdec).T.astype(BF16), v_ref[rows, :].astype(BF16)))
    st = state[...]
    for ci in reversed(range(cpb)):
        sin_ref[ci] = st.astype(sin_ref.dtype)
        st = cdec * st + kv[ci]
    state[...] = st

    @pl.when(n == pl.num_programs(2) - 1)
    def _():
        sfin_ref[...] = st


def _ret_fwd_kernel(q_ref, k_ref, v_ref, g_ref, dec_ref, s0_ref, sb_ref, y_ref, sfin_ref, state, *, cpb):
    n = pl.program_id(2)
    c = RET_CHUNK

    @pl.when(n == 0)
    def _():
        state[...] = s0_ref[...]

    lg_f, lg_b = _decay_terms(dec_ref)
    pos = lax.broadcasted_iota(jnp.int32, (c, 1), 0).astype(F32)
    ri = lax.broadcasted_iota(jnp.int32, (c, c), 0)
    cj = lax.broadcasted_iota(jnp.int32, (c, c), 1)
    diff = (ri - cj).astype(F32)
    intra = (jnp.where(diff >= 0, jnp.exp(lg_f * jnp.maximum(diff, 0.0)), 0.0)
             + jnp.where(diff <= 0, jnp.exp(lg_b * jnp.maximum(-diff, 0.0)), 0.0))
    qdec_f = jnp.exp(lg_f * (pos + 1.0))
    qdec_b = jnp.exp(lg_b * (float(c) - pos))
    kdec_f = jnp.exp(lg_f * (float(c) - 1.0 - pos))
    cdec_f = jnp.exp(lg_f * float(c))
    lhs, vs, kv = [], [], []
    for ci in range(cpb):
        rows = pl.ds(ci * c, c)
        q, k = q_ref[rows, :], k_ref[rows, :]
        v = v_ref[rows, :].astype(BF16)
        scores = _dot_nt(q.astype(BF16), k.astype(BF16)) * intra
        lhs.append(jnp.concatenate([scores.astype(BF16), (q * qdec_f).astype(BF16),
                                    (q * qdec_b).astype(BF16)], axis=1))
        vs.append(v)
        kv.append(_dot((k * kdec_f).T.astype(BF16), v))
    st = state[...]
    for ci in range(cpb):
        rows = pl.ds(ci * c, c)
        rhs = jnp.concatenate([vs[ci], st.astype(BF16), sb_ref[ci].astype(BF16)], axis=0)
        out = _dot(lhs[ci], rhs)
        y_ref[rows, :] = (_silu(g_ref[rows, :]) * _rms(out)).astype(y_ref.dtype)
        st = cdec_f * st + kv[ci]
    state[...] = st

    @pl.when(n == pl.num_programs(2) - 1)
    def _():
        sfin_ref[...] = st


def _retention(ret3, dec, layer, s0_f, s0_b):
    _, bsz, seq, _ = ret3.shape
    c = RET_CHUNK
    cpb = next(n for n in (16, 8, 4, 2, 1) if seq % (n * c) == 0)
    rows = cpb * c
    nb = seq // rows
    h = RET_HEADS
    grid = (bsz, h, nb)
    state_spec = pl.BlockSpec((None, None, HEAD_DIM, HEAD_DIM), lambda b, hh, n: (b, hh, 0, 0))
    dec_spec = pl.BlockSpec((None, 2, None, SUBLANES, LANES), lambda b, hh, n: (layer, 0, hh, 0, 0))

    def col(j, rev):
        if rev:
            return pl.BlockSpec((None, None, rows, HEAD_DIM), lambda b, hh, n: (j * h + hh, b, nb - 1 - n, 0))
        return pl.BlockSpec((None, None, rows, HEAD_DIM), lambda b, hh, n: (j * h + hh, b, n, 0))

    sb_in, st_b = pl.pallas_call(
        functools.partial(_ret_bwd_kernel, cpb=cpb),
        grid=grid,
        in_specs=[col(1, True), col(2, True), dec_spec, state_spec],
        out_specs=[
            pl.BlockSpec((None, None, cpb, HEAD_DIM, HEAD_DIM), lambda b, hh, n: (b, hh, nb - 1 - n, 0, 0)),
            state_spec,
        ],
        out_shape=[
            jax.ShapeDtypeStruct((bsz, h, seq // c, HEAD_DIM, HEAD_DIM), BF16),
            jax.ShapeDtypeStruct((bsz, h, HEAD_DIM, HEAD_DIM), F32),
        ],
        scratch_shapes=[pltpu.VMEM((HEAD_DIM, HEAD_DIM), F32)],
        name="ret_bwd",
        compiler_params=_cparams("parallel", "parallel", "arbitrary"),
    )(ret3, ret3, dec, s0_b)

    y, st_f = pl.pallas_call(
        functools.partial(_ret_fwd_kernel, cpb=cpb),
        grid=grid,
        in_specs=[col(0, False), col(1, False), col(2, False), col(3, False), dec_spec, state_spec,
                  pl.BlockSpec((None, None, cpb, HEAD_DIM, HEAD_DIM), lambda b, hh, n: (b, hh, n, 0, 0))],
        out_specs=[
            pl.BlockSpec((None, rows, HEAD_DIM), lambda b, hh, n: (b, n, hh)),
            state_spec,
        ],
        out_shape=[
            jax.ShapeDtypeStruct((bsz, seq, RET_DIM), BF16),
            jax.ShapeDtypeStruct((bsz, h, HEAD_DIM, HEAD_DIM), F32),
        ],
        scratch_shapes=[pltpu.VMEM((HEAD_DIM, HEAD_DIM), F32)],
        name="ret_fwd",
        compiler_params=_cparams("parallel", "parallel", "arbitrary"),
    )(ret3, ret3, ret3, ret3, dec, s0_f, sb_in)
    return y, st_f, st_b


def _swa_kernel(q_ref, k_ref, v_ref, kp_ref, vp_ref, kn_ref, vn_ref, kc_ref, vc_ref, sink_ref, y_ref,
                *, tq, seq):
    i = pl.program_id(1)
    w = SWA_WINDOW
    nloc = tq + 2 * w
    q_pos = i * tq + lax.broadcasted_iota(jnp.int32, (tq, nloc), 0)
    k_pos = i * tq - w + lax.broadcasted_iota(jnp.int32, (tq, nloc), 1)
    valid = (jnp.abs(k_pos - q_pos) <= w) & (k_pos >= 0) & (k_pos < seq)
    nk = nloc + kc_ref.shape[0]
    ones = jnp.ones((nk, HEAD_DIM), BF16)
    for kv in range(SWA_KV_HEADS):
        ks = pl.ds(kv * HEAD_DIM, HEAD_DIM)
        kall = jnp.concatenate([kp_ref[:, ks], k_ref[:, ks], kn_ref[:, ks], kc_ref[:, ks]], axis=0)
        vall = jnp.concatenate(
            [jnp.concatenate([vp_ref[:, ks], v_ref[:, ks], vn_ref[:, ks], vc_ref[:, ks]], axis=0), ones], axis=1)
        for g in range(SWA_GROUP):
            hq = kv * SWA_GROUP + g
            hs = pl.ds(hq * HEAD_DIM, HEAD_DIM)
            s = _dot_nt(q_ref[:, hs], kall)
            s = jnp.concatenate([jnp.where(valid, s[:, 0:nloc], NEG_INF), s[:, nloc:]], axis=1)
            sink = sink_ref[hq, 0:1, 0:1] * LOG2E
            m = jnp.maximum(jnp.max(s, axis=-1, keepdims=True), sink)
            acc = _dot(jnp.exp2(s - m).astype(BF16), vall)
            den = acc[:, HEAD_DIM:2 * HEAD_DIM] + jnp.exp2(sink - m)
            y_ref[:, hs] = (acc[:, 0:HEAD_DIM] / den).astype(y_ref.dtype)


def _swa_latent(swa3, swa_ctx3, sink, layer):
    bsz, seq, _ = swa3.shape
    t = swa_ctx3.shape[1]
    w = SWA_WINDOW
    tq = 256 if seq % 256 == 0 else w
    r = tq // w
    nblk = seq // w
    kvw = SWA_KV_HEADS * HEAD_DIM
    qw = SWA_HEADS * HEAD_DIM
    k_col, v_col = qw // kvw, qw // kvw + 1
    prev = lambda i: jnp.maximum(i * r - 1, 0)
    nxt = lambda i: jnp.minimum((i + 1) * r, nblk - 1)
    return pl.pallas_call(
        functools.partial(_swa_kernel, tq=tq, seq=seq),
        grid=(bsz, seq // tq),
        in_specs=[
            pl.BlockSpec((None, tq, qw), lambda b, i: (b, i, 0)),
            pl.BlockSpec((None, tq, kvw), lambda b, i: (b, i, k_col)),
            pl.BlockSpec((None, tq, kvw), lambda b, i: (b, i, v_col)),
            pl.BlockSpec((None, w, kvw), lambda b, i: (b, prev(i), k_col)),
            pl.BlockSpec((None, w, kvw), lambda b, i: (b, prev(i), v_col)),
            pl.BlockSpec((None, w, kvw), lambda b, i: (b, nxt(i), k_col)),
            pl.BlockSpec((None, w, kvw), lambda b, i: (b, nxt(i), v_col)),
            pl.BlockSpec((None, t, kvw), lambda b, i: (b, 0, k_col)),
            pl.BlockSpec((None, t, kvw), lambda b, i: (b, 0, v_col)),
            pl.BlockSpec((None, SWA_HEADS, SUBLANES, LANES), lambda b, i: (layer, 0, 0, 0)),
        ],
        out_specs=pl.BlockSpec((None, tq, qw), lambda b, i: (b, i, 0)),
        out_shape=jax.ShapeDtypeStruct((bsz, seq, qw), BF16),
        name="swa_latent",
        compiler_params=_cparams("parallel", "parallel"),
    )(swa3, swa3, swa3, swa3, swa3, swa3, swa3, swa_ctx3, swa_ctx3, sink)


def _mla_proj_kernel(x_ref, qn_ref, kvn_ref, wq_ref, wk_ref, wv_ref, cos_ref, slo_ref, shi_ref,
                     q_ref, k_ref, v_ref):
    nq = MLA_ROPE_DIM // 4
    cos, slo, shi = cos_ref[...], slo_ref[...], shi_ref[...]
    cq = (_rms(x_ref[:, 0:MLA_Q_RANK]) * qn_ref[...]).astype(BF16)
    q = _dot(cq, wq_ref[...]) * (MLA_SCALE * LOG2E)
    ckv =(_rms(x_ref[:, MLA_Q_RANK:MLA_Q_RANK + MLA_KV_RANK]) * kvn_ref[...]).astype(BF16)
    kn = _dot(ckv, wk_ref[...])
    v_ref[...] = _dot(ckv, wv_ref[...]).astype(v_ref.dtype)
    kr = _rope(x_ref[:, MLA_Q_RANK + MLA_KV_RANK:MLA_COLS_PAD], cos, slo, shi, nq).astype(k_ref.dtype)
    for h in range(MLA_HEADS):
        a = h * MLA_QK_PAD
        q_ref[:, a:a + MLA_NOPE_DIM] = q[:, a:a + MLA_NOPE_DIM].astype(q_ref.dtype)
        q_ref[:, a + MLA_NOPE_DIM:a + MLA_QK_PAD] = _rope(
            q[:, a + MLA_NOPE_DIM:a + MLA_QK_PAD], cos, slo, shi, nq).astype(q_ref.dtype)
        k_ref[:, a:a + MLA_NOPE_DIM] = kn[:, h * MLA_NOPE_DIM:(h + 1) * MLA_NOPE_DIM].astype(k_ref.dtype)
        k_ref[:, a + MLA_NOPE_DIM:a + MLA_QK_PAD] = kr


def _mla_project(mla2, seq, qn, kvn, wq, wk, wv, tabs, layer, tm):
    m_rows = mla2.shape[0]
    cos, slo, shi = tabs
    tpb = seq // tm
    qkw = MLA_HEADS * MLA_QK_PAD
    vw = MLA_HEADS * MLA_V_DIM
    lay = lambda i: (layer, 0, 0)
    tab = pl.BlockSpec((tm, LANES), lambda i: (i % tpb, 0))
    return pl.pallas_call(
        _mla_proj_kernel,
        grid=(m_rows // tm,),
        in_specs=[
            pl.BlockSpec((tm, MLA_COLS_PAD), lambda i: (i, 0)),
            pl.BlockSpec((None, 1, MLA_Q_RANK), lay),
            pl.BlockSpec((None, 1, MLA_KV_RANK), lay),
            pl.BlockSpec((None, MLA_Q_RANK, qkw), lay),
            pl.BlockSpec((None, MLA_KV_RANK, vw), lay),
            pl.BlockSpec((None, MLA_KV_RANK, vw), lay),
            tab, tab, tab,
        ],
        out_specs=[
            pl.BlockSpec((tm, qkw), lambda i: (i, 0)),
            pl.BlockSpec((tm, qkw), lambda i: (i, 0)),
            pl.BlockSpec((tm, vw), lambda i: (i, 0)),
        ],
        out_shape=[
            jax.ShapeDtypeStruct((m_rows, qkw), BF16),
            jax.ShapeDtypeStruct((m_rows, qkw), BF16),
            jax.ShapeDtypeStruct((m_rows, vw), BF16),
        ],
        name="mla_project",
        compiler_params=_cparams("parallel"),
    )(mla2, qn, kvn, wq, wk, wv, cos, slo, shi)


def _attn_kernel(*refs, n_src, q_scale, has_sink, nq):
    q_ref = refs[0]
    k_refs = refs[1:1 + n_src]
    v_refs = refs[1 + n_src:1 + 2 * n_src]
    pos = 1 + 2 * n_src
    sink_ref = refs[pos] if has_sink else None
    sink_b_ref = refs[pos + 1] if has_sink else None
    o_ref = refs[pos + (2 if has_sink else 0)]
    scratch = refs[pos + (3 if has_sink else 1):]
    s_refs = scratch[0:n_src]
    va_refs = scratch[n_src:2 * n_src]
    mb_ref = scratch[2 * n_src]
    t = pl.program_id(0)
    tq = q_ref.shape[0]
    dv = o_ref.shape[-1]
    tks = [s_ref.shape[-1] for s_ref in s_refs]

    @pl.when(t == 0)
    def _():
        for s_ref in s_refs:
            s_ref[...] = jnp.zeros(s_ref.shape, F32)
        mb_ref[...] = jnp.zeros(mb_ref.shape, F32)

    @pl.when(jnp.maximum(t - 1, 0) % nq == 0)
    def _():
        for v_ref, va in zip(v_refs, va_refs):
            va[:, 0:dv] = v_ref[...].astype(BF16)
            va[:, dv:2 * dv] = jnp.ones((va.shape[0], dv), BF16)

    q = q_ref[...]
    if q_scale != 1.0:
        q = q.astype(F32) * q_scale
    q = q.astype(BF16)
    m_prev = mb_ref[...]

    def lane_max(m_vec, s):
        for c in range(s.shape[-1] // LANES):
            m_vec = jnp.maximum(m_vec, s[:, c * LANES:(c + 1) * LANES])
        return m_vec

    m_vec = jnp.full((tq, LANES), NEG_INF, F32)
    acc = jnp.zeros((tq, 2 * dv), F32)
    for k_ref, s_ref, va, tkk in zip(k_refs, s_refs, va_refs, tks):
        for j in range(s_ref.shape[0]):
            rows = pl.ds(j * tkk, tkk)
            p = jnp.exp2(s_ref[j] - jnp.concatenate([m_prev] * (tkk // LANES), axis=1))
            acc = acc + _dot(p.astype(BF16), va[rows, :])
            s = _dot_nt(q, k_ref[rows, :].astype(BF16))
            s_ref[j] = s
            m_vec = lane_max(m_vec, s)

    l = acc[:, dv:2 * dv]
    if has_sink:
        l = l + jnp.exp2(sink_b_ref[0:1, 0:1] * LOG2E - m_prev)
    o_ref[...] = (acc[:, 0:dv] / l).astype(o_ref.dtype)

    m_row = jnp.max(m_vec, axis=-1, keepdims=True)
    if has_sink:
        m_row = jnp.maximum(m_row, sink_ref[0:1, 0:1] * LOG2E)
    mb_ref[...] = jnp.broadcast_to(m_row, (tq, LANES))


def _attention(q3, ks, vs, *, n_heads, group, dq, dv, q_col0, k_col0, v_col0, q_scale, sink=None, layer=0,
               tq=1024, tk=512):
    bsz, lq, _ = q3.shape
    tq = min(tq, lq)
    assert dv == LANES
    n_src = len(ks)
    tks = [min(tk, k.shape[1]) for k in ks]
    nq = lq // tq
    n_blocks = bsz * n_heads * nq

    def cur(t):
        t = jnp.minimum(t, n_blocks - 1)
        return t // (n_heads * nq), (t // nq) % n_heads, t % nq

    def prev(t):
        return cur(jnp.maximum(t - 1, 0))

    def at(fn, spec):
        def index_map(t):
            b, h, i = fn(t)
            return spec(b, h, i)
        return index_map

    in_specs = [pl.BlockSpec((None, tq, dq), at(cur, lambda b, h, i: (b, i, q_col0 + h)))]
    in_specs += [pl.BlockSpec((None, k.shape[1], dq), at(cur, lambda b, h, i: (b, 0, k_col0 + h // group)))
                 for k in ks]
    in_specs += [pl.BlockSpec((None, v.shape[1], dv), at(prev, lambda b, h, i: (b, 0, v_col0 + h // group)))
                 for v in vs]
    args = [q3] + list(ks) + list(vs)
    if sink is not None:
        for fn in (cur, prev):
            in_specs.append(pl.BlockSpec((None, None, SUBLANES, LANES),
                                         at(fn, lambda b, h, i: (layer, h, 0, 0))))
            args.append(sink)
    scratch = [pltpu.VMEM((k.shape[1] // tkk, tq, tkk), F32) for k, tkk in zip(ks, tks)]
    scratch += [pltpu.VMEM((v.shape[1], 2 * dv), BF16) for v in vs]
    scratch += [pltpu.VMEM((tq, LANES), F32)]
    return pl.pallas_call(
        functools.partial(_attn_kernel, n_src=n_src, q_scale=q_scale, has_sink=sink is not None, nq=nq),
        grid=(n_blocks + 1,),
        in_specs=in_specs,
        out_specs=pl.BlockSpec((None, tq, dv), at(prev, lambda b, h, i: (b, i, h))),
        out_shape=jax.ShapeDtypeStruct((bsz, lq, n_heads * dv), BF16),
        scratch_shapes=scratch,
        name="attention",
        compiler_params=_cparams("arbitrary"),
    )(*args)


def _outproj_kernel(yr_ref, ys_ref, ym_ref, x_ref, mod_ref, w_ref, g_ref, b_ref, o_ref, *, alpha):
    a, b = RET_DIM, RET_DIM + SWA_HEADS * HEAD_DIM
    mix = (_dot(yr_ref[...], w_ref[0:a, :]) + _dot(ys_ref[...], w_ref[a:b, :])
           + _dot(ym_ref[...], w_ref[b:MIX_WIDTH, :]))
    z = alpha * x_ref[...] + (1.0 + mod_ref[2:3, :]) * mix
    o_ref[...] = _layer_norm(z, g_ref[...], b_ref[...])


def _outproj_ln(y_ret, y_swa, y_mla, x2, mods, w_o, ln_g, ln_b, layer, mod_row0, rows_per_mod, tm, alpha):
    m_rows, d = x2.shape
    mod_idx = lambda i: (layer * MOD_ROWS + mod_row0 + (i * tm) // rows_per_mod, 0, 0)
    lay = lambda i: (layer, 0, 0)
    row = lambda w: pl.BlockSpec((tm, w), lambda i: (i, 0))
    return pl.pallas_call(
        functools.partial(_outproj_kernel, alpha=alpha),
        grid=(m_rows // tm,),
        in_specs=[
            row(RET_DIM), row(SWA_HEADS * HEAD_DIM), row(MLA_HEADS * MLA_V_DIM), row(d),
            pl.BlockSpec((None, N_MOD, d), mod_idx),
            _resident((None, MIX_WIDTH, d), lay),
            pl.BlockSpec((None, 1, d), lay),
            pl.BlockSpec((None, 1, d), lay),
        ],
        out_specs=row(d),
        out_shape=jax.ShapeDtypeStruct((m_rows, d), F32),
        name="outproj_ln",
        compiler_params=_cparams("parallel"),
    )(y_ret, y_swa, y_mla, x2, mods, w_o, ln_g, ln_b)


def _ffn_kernel(x_ref, xp_ref, xn_ref, mod_ref, wu_ref, wg_ref, cw_ref, cb_ref, wd_ref, g_ref, b_ref,
                o_ref, h_ext, g_ext, *, tm, seq, alpha):
    i = pl.program_id(0)
    f = pl.program_id(1)

    @pl.when(f == 0)
    def _():
        scale = 1.0 + mod_ref[4:5, :]
        shift = mod_ref[3:4, :]
        has_prev = jnp.where((i * tm) % seq != 0, 1.0, 0.0).astype(F32)
        has_next = jnp.where(((i + 1) * tm) % seq != 0, 1.0, 0.0).astype(F32)
        h_ext[0:HALO, :] = ((xp_ref[...] * scale + shift) * has_prev).astype(BF16)
        h_ext[HALO:HALO + tm, :] = (x_ref[...] * scale + shift).astype(BF16)
        h_ext[HALO + tm:tm + 2 * HALO, :] = ((xn_ref[...] * scale + shift) * has_next).astype(BF16)
        o_ref[...] = jnp.zeros_like(o_ref)

    g_ext[...] = _dot(h_ext[...], wg_ref[...])
    u = _dot(h_ext[HALO:HALO + tm, :], wu_ref[...])
    g_prev = g_ext[HALO - 1:HALO - 1 + tm, :]
    g_next = g_ext[HALO + 1:HALO + 1 + tm, :]
    if tm > seq:
        r = lax.broadcasted_iota(jnp.int32, (tm, 1), 0) % seq
        g_prev = jnp.where(r != 0, g_prev, 0.0)
        g_next = jnp.where(r != seq - 1, g_next, 0.0)
    gc = (g_prev * cw_ref[0:1, :] + g_ext[HALO:HALO + tm, :] * cw_ref[1:2, :]
          + g_next * cw_ref[2:3, :] + cb_ref[...])
    a = (_silu(gc) * u).astype(BF16)
    o_ref[...] += _dot(a, wd_ref[...])

    @pl.when(f == pl.num_programs(1) - 1)
    def _():
        xa = x_ref[...]
        z = alpha * xa + (1.0 + mod_ref[5:6, :]) * o_ref[...]
        o_ref[...] = _layer_norm(z, g_ref[...], b_ref[...])


def _ffn(x2, seq, mods, w_up, conv_w, conv_b, w_down, ln_g, ln_b, layer, mod_row0, rows_per_mod, tm, alpha):
    m_rows, d = x2.shape
    dff = w_down.shape[1]
    tf = 512 if dff % 512 == 0 else (256 if dff % 256 == 0 else LANES)
    nf = dff // tf
    hb = tm // HALO
    nhb = m_rows // HALO
    mod_idx = lambda i, f: (layer * MOD_ROWS + mod_row0 + (i * tm) // rows_per_mod, 0, 0)
    lay = lambda i, f: (layer, 0, 0)
    return pl.pallas_call(
        functools.partial(_ffn_kernel, tm=tm, seq=seq, alpha=alpha),
        grid=(m_rows // tm, nf),
        in_specs=[
            (_resident if tm > 512 else pl.BlockSpec)((tm, d), lambda i, f: (i, 0)),
            pl.BlockSpec((HALO, d), lambda i, f: (jnp.maximum(i * hb - 1, 0), 0)),
            pl.BlockSpec((HALO, d), lambda i, f: (jnp.minimum((i + 1) * hb, nhb - 1), 0)),
            pl.BlockSpec((None, N_MOD, d), mod_idx),
            pl.BlockSpec((None, d, tf), lambda i, f: (layer, 0, f)),
            pl.BlockSpec((None, d, tf), lambda i, f: (layer, 0, nf + f)),
            pl.BlockSpec((None, 3, tf), lambda i, f: (layer, 0, f)),
            pl.BlockSpec((None, 1, tf), lambda i, f: (layer, 0, f)),
            pl.BlockSpec((None, tf, d), lambda i, f: (layer, f, 0)),
            pl.BlockSpec((None, 1, d), lay),
            pl.BlockSpec((None, 1, d), lay),
        ],
        out_specs=pl.BlockSpec((tm, d), lambda i, f: (i, 0)),
        out_shape=jax.ShapeDtypeStruct((m_rows, d), F32),
        scratch_shapes=[
            pltpu.VMEM((tm + 2 * HALO, d), BF16),
            pltpu.VMEM((tm + 2 * HALO, tf), F32),
        ],
        name="conv_ffn",
        compiler_params=_cparams("parallel", "arbitrary"),
    )(x2, x2, x2, mods, w_up, w_up, conv_w, conv_b, w_down, ln_g, ln_b)


def _rope_tables(n_tokens, dim):
    rows = n_tokens // GRID_W
    r = np.repeat(np.arange(rows, dtype=np.float32), GRID_W)
    cc = np.tile(np.arange(GRID_W, dtype=np.float32), rows)
    n_freq = dim // 4
    inv = jnp.asarray(ROPE_THETA, F32) ** (-jnp.arange(n_freq, dtype=F32) / n_freq)
    ang_r = jnp.asarray(r)[:, None] * inv
    ang_c = jnp.asarray(cc)[:, None] * inv
    ang = jnp.concatenate([ang_r, ang_r, ang_c, ang_c], axis=-1)
    cos, sin = jnp.cos(ang), jnp.sin(ang)
    lane = np.arange(dim)
    lo = jnp.asarray((lane % (2 * n_freq)) < n_freq)
    sin_lo = jnp.where(lo, -sin, 0.0)
    sin_hi = jnp.where(lo, 0.0, sin)
    pad = LANES - dim
    if pad:
        cos = jnp.pad(cos, ((0, 0), (0, pad)), constant_values=1.0)
        sin_lo = jnp.pad(sin_lo, ((0, 0), (0, pad)))
        sin_hi = jnp.pad(sin_hi, ((0, 0), (0, pad)))
    return cos, sin_lo, sin_hi


def _identity_tables(n_tokens):
    return (jnp.ones((n_tokens, LANES), F32), jnp.zeros((n_tokens, LANES), F32),
            jnp.zeros((n_tokens, LANES), F32))


def _lane_bcast(p):
    return jnp.broadcast_to(p.astype(F32)[..., None, None], p.shape + (SUBLANES, LANES))


def kernel(x, c, ctx, c_ctx, ada_w, ada_b, w_in, ret_decay_fwd, ret_decay_bwd, swa_sink, mla_q_norm, mla_w_uq,
           mla_kv_norm, mla_w_ukv, w_o, ln1_g, ln1_b, ffn_w_up, ffn_conv_w, ffn_conv_b, ffn_w_down, ln2_g, ln2_b):
    bsz, seq, d = x.shape
    t = ctx.shape[1]
    depth = w_in.shape[0]
    assert bsz + 1 <= MOD_ROWS and seq % RET_CHUNK == 0 and t % RET_CHUNK == 0
    assert seq % GRID_W == 0 and d % LANES == 0
    alpha = (2 * depth) ** 0.25

    w_in_b = jnp.pad(w_in, ((0, 0), (0, 0), (0, IN_WIDTH_PAD - IN_WIDTH))).astype(BF16)
    w_o_b = w_o.astype(BF16)
    w_up_b = ffn_w_up.astype(BF16)
    w_down_b = ffn_w_down.astype(BF16)
    hq = MLA_NOPE_DIM + MLA_ROPE_DIM
    wq = mla_w_uq.reshape(depth, MLA_Q_RANK, MLA_HEADS, hq)
    wq = jnp.pad(wq, ((0, 0), (0, 0), (0, 0), (0, MLA_QK_PAD - hq)))
    wq = wq.reshape(depth, MLA_Q_RANK, MLA_HEADS * MLA_QK_PAD).astype(BF16)
    wkv = mla_w_ukv.reshape(depth, MLA_KV_RANK, MLA_HEADS, MLA_NOPE_DIM + MLA_V_DIM)
    wk = wkv[..., :MLA_NOPE_DIM].reshape(depth, MLA_KV_RANK, MLA_HEADS * MLA_NOPE_DIM).astype(BF16)
    wv = wkv[..., MLA_NOPE_DIM:].reshape(depth, MLA_KV_RANK, MLA_HEADS * MLA_V_DIM).astype(BF16)
    qn = mla_q_norm.reshape(depth, 1, MLA_Q_RANK)
    kvn = mla_kv_norm.reshape(depth, 1, MLA_KV_RANK)
    g1, b1 = ln1_g.reshape(depth, 1, d), ln1_b.reshape(depth, 1, d)
    g2, b2 = ln2_g.reshape(depth, 1, d), ln2_b.reshape(depth, 1, d)
    conv_b = ffn_conv_b.reshape(depth, 1, -1)
    dec = _lane_bcast(jnp.stack([ret_decay_fwd, ret_decay_bwd], axis=1))
    sink = _lane_bcast(swa_sink)

    tabs_h = _rope_tables(seq, HEAD_DIM)
    tabs_m = _rope_tables(seq, MLA_ROPE_DIM)
    tabs_id = _identity_tables(t)

    cond = jnp.zeros((MOD_ROWS, d), F32).at[:bsz].set(c).at[bsz].set(c_ctx)
    mods = _ada_mod(cond, ada_w, ada_b).reshape(depth * MOD_ROWS, N_MOD, d)

    tm_x = 512 if seq % 512 == 0 else RET_CHUNK
    tm_f = tm_x
    tm_p = 1024 if seq % 1024 == 0 else tm_x
    tm_c = t if t <= 512 else RET_CHUNK
    seqs = max([j for j in range(1, bsz + 1) if bsz % j == 0 and t * j <= 1024], default=0)
    tm_cf = t * seqs if seqs else tm_c
    zeros_state = jnp.zeros((bsz, RET_HEADS, HEAD_DIM, HEAD_DIM), F32)

    x2 = x.reshape(bsz * seq, d)
    xc2 = ctx.reshape(bsz * t, d)
    for l in range(depth):
        last = l == depth - 1
        ret_c, swa_c, mla_c = _inproj(xc2, t, mods, w_in_b, tabs_id, l, bsz, bsz * t, tm_c)
        swa_c3 = swa_c.reshape(bsz, t, SWA_COLS)
        y_ret_c, st_f, st_b = _retention(ret_c.reshape(-1, bsz, t, HEAD_DIM), dec, l, zeros_state, zeros_state)
        q_c, k_c, v_c = _mla_project(mla_c, t, qn, kvn, wq, wk, wv, tabs_id, l, tm_c)
        k_c3 = k_c.reshape(bsz, t, -1)
        v_c3 = v_c.reshape(bsz, t, -1)

        ret_x, swa_x, mla_x = _inproj(x2, seq, mods, w_in_b, tabs_h, l, 0, seq, tm_x)
        y_ret, _, _ = _retention(ret_x.reshape(-1, bsz, seq, HEAD_DIM), dec, l, st_f, st_b)
        y_swa = _swa_latent(swa_x.reshape(bsz, seq, SWA_COLS), swa_c3, sink, l)
        q_x, k_x, v_x = _mla_project(mla_x, seq, qn, kvn, wq, wk, wv, tabs_m, l, tm_p)
        y_mla = _attention(q_x.reshape(bsz, seq, -1), [k_x.reshape(bsz, seq, -1), k_c3],
                           [v_x.reshape(bsz, seq, -1), v_c3], n_heads=MLA_HEADS, group=1,
                           dq=MLA_QK_PAD, dv=MLA_V_DIM, q_col0=0, k_col0=0, v_col0=0, q_scale=1.0)
        x_a = _outproj_ln(y_ret.reshape(bsz * seq, -1), y_swa.reshape(bsz * seq, -1),
                          y_mla.reshape(bsz * seq, -1), x2, mods, w_o_b, g1, b1, l, 0, seq, tm_x, alpha)
        x_new = _ffn(x_a, seq, mods, w_up_b, ffn_conv_w, conv_b, w_down_b, g2, b2, l, 0, seq, tm_f, alpha)

        if not last:
            y_swa_c = _attention(swa_c3, [swa_c3], [swa_c3], n_heads=SWA_HEADS, group=SWA_GROUP,
                                 dq=HEAD_DIM, dv=HEAD_DIM, q_col0=0, k_col0=SWA_HEADS,
                                 v_col0=SWA_HEADS + SWA_KV_HEADS, q_scale=1.0, sink=sink, layer=l)
            y_mla_c = _attention(q_c.reshape(bsz, t, -1), [k_c3], [v_c3], n_heads=MLA_HEADS, group=1,
                                 dq=MLA_QK_PAD, dv=MLA_V_DIM, q_col0=0, k_col0=0, v_col0=0, q_scale=1.0)
            xc_a = _outproj_ln(y_ret_c.reshape(bsz * t, -1), y_swa_c.reshape(bsz * t, -1),
                               y_mla_c.reshape(bsz * t, -1), xc2, mods, w_o_b, g1, b1, l, bsz, bsz * t,
                               tm_c, alpha)
            xc2 = _ffn(xc_a, t, mods, w_up_b, ffn_conv_w, conv_b, w_down_b, g2, b2, l, bsz, bsz * t,
                       tm_cf, alpha)
        x2 = x_new
    return x2.reshape(bsz, seq, d)
```

```python
import functools

import jax
import jax.numpy as jnp
import numpy as np
from jax import lax
from jax.experimental import pallas as pl
from jax.experimental.pallas import tpu as pltpu

GRID_W = 64
HEAD_DIM = 128
ROPE_THETA = 10000.0
RET_HEADS = 4
RET_DIM = RET_HEADS * HEAD_DIM
RET_CHUNK = 128
SWA_HEADS = 6
SWA_KV_HEADS = 2
SWA_GROUP = SWA_HEADS // SWA_KV_HEADS
SWA_WINDOW = 128
MLA_HEADS = 6
MLA_Q_RANK = 512
MLA_KV_RANK = 256
MLA_NOPE_DIM = 128
MLA_ROPE_DIM = 64
MLA_V_DIM = 128
MLA_SCALE = (MLA_NOPE_DIM + MLA_ROPE_DIM) ** -0.5
MLA_QK_PAD = 256
N_MOD = 6
LN_EPS = 1e-5
RMS_EPS = 1e-6
NEG_INF = -1e30
LOG2E = 1.4426950408889634

RET_COLS = 4 * RET_DIM
SWA_COLS = (SWA_HEADS + 2 * SWA_KV_HEADS) * HEAD_DIM
MLA_COLS = MLA_Q_RANK + MLA_KV_RANK + MLA_ROPE_DIM
MLA_COLS_PAD = 896
IN_WIDTH = RET_COLS + SWA_COLS + MLA_COLS
IN_WIDTH_PAD = RET_COLS + SWA_COLS + MLA_COLS_PAD
MIX_WIDTH = RET_DIM + SWA_HEADS * HEAD_DIM + MLA_HEADS * MLA_V_DIM

LANES = 128
SUBLANES = 8
MOD_ROWS = 8
VMEM_LIMIT = 56 * 1024 * 1024
HALO = 16

BF16 = jnp.bfloat16
F32 = jnp.float32


def _cparams(*sem):
    return pltpu.CompilerParams(dimension_semantics=sem, vmem_limit_bytes=VMEM_LIMIT)


def _resident(block, index_map):
    return pl.BlockSpec(block, index_map, pipeline_mode=pl.Buffered(1))


def _dot(a, b):
    return jnp.dot(a, b, preferred_element_type=F32)


def _dot_nt(a, b):
    return lax.dot_general(a, b, (((1,), (1,)), ((), ())), preferred_element_type=F32)


def _silu(x):
    return x * (1.0 / (1.0 + jnp.exp(-x)))


def _rope(x, cos, sin_lo, sin_hi, nq):
    w = x.shape[-1]
    return x * cos + pltpu.roll(x, w - nq, 1) * sin_lo + pltpu.roll(x, nq, 1) * sin_hi


def _layer_norm(z, g, b):
    mu = jnp.mean(z, axis=-1, keepdims=True)
    zc = z - mu
    var = jnp.mean(zc * zc, axis=-1, keepdims=True)
    return zc * lax.rsqrt(var + LN_EPS) * g + b


def _rms(x):
    return x * lax.rsqrt(jnp.mean(x * x, axis=-1, keepdims=True) + RMS_EPS)


def _log_sigmoid(x):
    return -(jnp.maximum(-x, 0.0) + jnp.log(1.0 + jnp.exp(-jnp.abs(x))))


def _ada_kernel(c_ref, w_ref, b_ref, o_ref):
    sc = _silu(c_ref[...]).astype(BF16)
    o_ref[...] = _dot(sc, w_ref[...].astype(BF16)) + b_ref[...]


def _ada_mod(cond, ada_w, ada_b):
    depth, d, n = ada_w.shape
    tn = next(c for c in (1024, 512, 256, LANES) if n % c == 0)
    return pl.pallas_call(
        _ada_kernel,
        grid=(depth, n // tn),
        in_specs=[
            pl.BlockSpec((MOD_ROWS, d), lambda l, j: (0, 0)),
            pl.BlockSpec((None, d, tn), lambda l, j: (l, 0, j)),
            pl.BlockSpec((None, 1, tn), lambda l, j: (l, 0, j)),
        ],
        out_specs=pl.BlockSpec((None, MOD_ROWS, tn), lambda l, j: (l, 0, j)),
        out_shape=jax.ShapeDtypeStruct((depth, MOD_ROWS, n), F32),
        name="ada_mod",
        compiler_params=_cparams("parallel", "parallel"),
    )(cond, ada_w, ada_b.reshape(depth, 1, n))


def _inproj_kernel(x_ref, mod_ref, w_ref, cos_ref, slo_ref, shi_ref, ret_ref, swa_ref, mla_ref):
    m = mod_ref[...]
    h = (x_ref[...] * (1.0 + m[1:2, :]) + m[0:1, :]).astype(BF16)
    cos, slo, shi = cos_ref[...], slo_ref[...], shi_ref[...]
    nq = HEAD_DIM // 4
    k_scale = HEAD_DIM ** -0.5

    def rope(t, c, scale):
        r = _rope(t[:, c:c + HEAD_DIM], cos, slo, shi, nq)
        return r if scale is None else r * scale

    def rope_heads(t, out_ref, col0, n_heads, scale):
        for hh in range(n_heads):
            c = col0 + hh * HEAD_DIM
            out_ref[:, c:c + HEAD_DIM] = rope(t, c, scale).astype(out_ref.dtype)

    ret = _dot(h, w_ref[:, 0:RET_COLS])
    for j in range(RET_COLS // HEAD_DIM):
        c = j * HEAD_DIM
        if j < RET_HEADS:
            ret_ref[j] = rope(ret, c, None)
        elif j < 2 * RET_HEADS:
            ret_ref[j] = rope(ret, c, k_scale)
        else:
            ret_ref[j] = ret[:, c:c + HEAD_DIM]
    swa = _dot(h, w_ref[:, RET_COLS:RET_COLS + SWA_COLS])
    sq_cols = SWA_HEADS * HEAD_DIM
    sk_cols = SWA_KV_HEADS * HEAD_DIM
    rope_heads(swa, swa_ref, 0, SWA_HEADS, k_scale * LOG2E)
    rope_heads(swa, swa_ref, sq_cols, SWA_KV_HEADS, None)
    swa_ref[:, sq_cols + sk_cols:SWA_COLS] = swa[:, sq_cols + sk_cols:SWA_COLS].astype(swa_ref.dtype)
    mla_ref[...] = _dot(h, w_ref[:, RET_COLS + SWA_COLS:IN_WIDTH_PAD])


def _inproj(x2, seq, mods, w_in, tabs, layer, mod_row0, rows_per_mod, tm):
    m_rows, d = x2.shape
    mod_idx = lambda i: (layer * MOD_ROWS + mod_row0 + (i * tm) // rows_per_mod, 0, 0)
    tpb = seq // tm
    tab = pl.BlockSpec((tm, LANES), lambda i: (i % tpb, 0))
    return pl.pallas_call(
        _inproj_kernel,
        grid=(m_rows // tm,),
        in_specs=[
            pl.BlockSpec((tm, d), lambda i: (i, 0)),
            pl.BlockSpec((None, N_MOD, d), mod_idx),
            _resident((None, d, IN_WIDTH_PAD), lambda i: (layer, 0, 0)),
            tab, tab, tab,
        ],
        out_specs=[
            pl.BlockSpec((RET_COLS // HEAD_DIM, tm, HEAD_DIM), lambda i: (0, i, 0)),
            pl.BlockSpec((tm, SWA_COLS), lambda i: (i, 0)),
            pl.BlockSpec((tm, MLA_COLS_PAD), lambda i: (i, 0)),
        ],
        out_shape=[
            jax.ShapeDtypeStruct((RET_COLS // HEAD_DIM, m_rows, HEAD_DIM), F32),
            jax.ShapeDtypeStruct((m_rows, SWA_COLS), BF16),
            jax.ShapeDtypeStruct((m_rows, MLA_COLS_PAD), F32),
        ],
        name="inproj",
        compiler_params=_cparams("parallel"),
    )(x2, mods, w_in, *tabs)


def _decay_terms(dec_ref):
    lg = _log_sigmoid(dec_ref[...])
    return lg[0, 0:1, 0:1], lg[1, 0:1, 0:1]


def _ret_bwd_kernel(k_ref, v_ref, dec_ref, s0_ref, sin_ref, sfin_ref, state, *, cpb):
    n = pl.program_id(2)
    c = RET_CHUNK

    @pl.when(n == 0)
    def _():
        state[...] = s0_ref[...]

    _, lg_b = _decay_terms(dec_ref)
    pos = lax.broadcasted_iota(jnp.int32, (c, 1), 0).astype(F32)
    kdec = jnp.exp(lg_b * pos)
    cdec = jnp.exp(lg_b * float(c))
    kv = []
    for ci in range(cpb):
        rows = pl.ds(ci * c, c)
        kv.append(_dot((k_ref[rows, :] * kdec).T.astype(BF16), v_ref[rows, :].astype(BF16)))
    st = state[...]
    for ci in reversed(range(cpb)):
        sin_ref[ci] = st.astype(sin_ref.dtype)
        st = cdec * st + kv[ci]
    state[...] = st

    @pl.when(n == pl.num_programs(2) - 1)
    def _():
        sfin_ref[...] = st


def _ret_fwd_kernel(q_ref, k_ref, v_ref, g_ref, dec_ref, s0_ref, sb_ref, y_ref, sfin_ref, state, *, cpb):
    n = pl.program_id(2)
    c = RET_CHUNK

    @pl.when(n == 0)
    def _():
        state[...] = s0_ref[...]

    lg_f, lg_b = _decay_terms(dec_ref)
    pos = lax.broadcasted_iota(jnp.int32, (c, 1), 0).astype(F32)
    ri = lax.broadcasted_iota(jnp.int32, (c, c), 0)
    cj = lax.broadcasted_iota(jnp.int32, (c, c), 1)
    diff = (ri - cj).astype(F32)
    intra = (jnp.where(diff >= 0, jnp.exp(lg_f * jnp.maximum(diff, 0.0)), 0.0)
             + jnp.where(diff <= 0, jnp.exp(lg_b * jnp.maximum(-diff, 0.0)), 0.0))
    qdec_f = jnp.exp(lg_f * (pos + 1.0))
    qdec_b = jnp.exp(lg_b * (float(c) - pos))
    kdec_f = jnp.exp(lg_f * (float(c) - 1.0 - pos))
    cdec_f = jnp.exp(lg_f * float(c))
    lhs, vs, kv = [], [], []
    for ci in range(cpb):
        rows = pl.ds(ci * c, c)
        q, k = q_ref[rows, :], k_ref[rows, :]
        v = v_ref[rows, :].astype(BF16)
        scores = _dot_nt(q.astype(BF16), k.astype(BF16)) * intra
        lhs.append(jnp.concatenate([scores.astype(BF16), (q * qdec_f).astype(BF16),
                                    (q * qdec_b).astype(BF16)], axis=1))
        vs.append(v)
        kv.append(_dot((k * kdec_f).T.astype(BF16), v))
    st = state[...]
    for ci in range(cpb):
        rows = pl.ds(ci * c, c)
        rhs = jnp.concatenate([vs[ci], st.astype(BF16), sb_ref[ci].astype(BF16)], axis=0)
        out = _dot(lhs[ci], rhs)
        y_ref[rows, :] = (_silu(g_ref[rows, :]) * _rms(out)).astype(y_ref.dtype)
        st = cdec_f * st + kv[ci]
    state[...] = st

    @pl.when(n == pl.num_programs(2) - 1)
    def _():
        sfin_ref[...] = st


def _retention(ret3, dec, layer, s0_f, s0_b):
    _, bsz, seq, _ = ret3.shape
    c = RET_CHUNK
    cpb = next(n for n in (16, 8, 4, 2, 1) if seq % (n * c) == 0)
    rows = cpb * c
    nb = seq // rows
    h = RET_HEADS
    grid = (bsz, h, nb)
    state_spec = pl.BlockSpec((None, None, HEAD_DIM, HEAD_DIM), lambda b, hh, n: (b, hh, 0, 0))
    dec_spec = pl.BlockSpec((None, 2, None, SUBLANES, LANES), lambda b, hh, n: (layer, 0, hh, 0, 0))

    def col(j, rev):
        if rev:
            return pl.BlockSpec((None, None, rows, HEAD_DIM), lambda b, hh, n: (j * h + hh, b, nb - 1 - n, 0))
        return pl.BlockSpec((None, None, rows, HEAD_DIM), lambda b, hh, n: (j * h + hh, b, n, 0))

    sb_in, st_b = pl.pallas_call(
        functools.partial(_ret_bwd_kernel, cpb=cpb),
        grid=grid,
        in_specs=[col(1, True), col(2, True), dec_spec, state_spec],
        out_specs=[
            pl.BlockSpec((None, None, cpb, HEAD_DIM, HEAD_DIM), lambda b, hh, n: (b, hh, nb - 1 - n, 0, 0)),
            state_spec,
        ],
        out_shape=[
            jax.ShapeDtypeStruct((bsz, h, seq // c, HEAD_DIM, HEAD_DIM), BF16),
            jax.ShapeDtypeStruct((bsz, h, HEAD_DIM, HEAD_DIM), F32),
        ],
        scratch_shapes=[pltpu.VMEM((HEAD_DIM, HEAD_DIM), F32)],
        name="ret_bwd",
        compiler_params=_cparams("parallel", "parallel", "arbitrary"),
    )(ret3, ret3, dec, s0_b)

    y, st_f = pl.pallas_call(
        functools.partial(_ret_fwd_kernel, cpb=cpb),
        grid=grid,
        in_specs=[col(0, False), col(1, False), col(2, False), col(3, False), dec_spec, state_spec,
                  pl.BlockSpec((None, None, cpb, HEAD_DIM, HEAD_DIM), lambda b, hh, n: (b, hh, n, 0, 0))],
        out_specs=[
            pl.BlockSpec((None, rows, HEAD_DIM), lambda b, hh, n: (b, n, hh)),
            state_spec,
        ],
        out_shape=[
            jax.ShapeDtypeStruct((bsz, seq, RET_DIM), BF16),
            jax.ShapeDtypeStruct((bsz, h, HEAD_DIM, HEAD_DIM), F32),
        ],
        scratch_shapes=[pltpu.VMEM((HEAD_DIM, HEAD_DIM), F32)],
        name="ret_fwd",
        compiler_params=_cparams("parallel", "parallel", "arbitrary"),
    )(ret3, ret3, ret3, ret3, dec, s0_f, sb_in)
    return y, st_f, st_b


def _swa_kernel(q_ref, k_ref, v_ref, kp_ref, vp_ref, kn_ref, vn_ref, kc_ref, vc_ref, sink_ref, y_ref,
                *, tq, seq):
    i = pl.program_id(1)
    w = SWA_WINDOW
    nloc = tq + 2 * w
    q_pos = i * tq + lax.broadcasted_iota(jnp.int32, (tq, nloc), 0)
    k_pos = i * tq - w + lax.broadcasted_iota(jnp.int32, (tq, nloc), 1)
    valid = (jnp.abs(k_pos - q_pos) <= w) & (k_pos >= 0) & (k_pos < seq)
    nk = nloc + kc_ref.shape[0]
    ones = jnp.ones((nk, HEAD_DIM), BF16)
    for kv in range(SWA_KV_HEADS):
        ks = pl.ds(kv * HEAD_DIM, HEAD_DIM)
        kall = jnp.concatenate([kp_ref[:, ks], k_ref[:, ks], kn_ref[:, ks], kc_ref[:, ks]], axis=0)
        vall = jnp.concatenate(
            [jnp.concatenate([vp_ref[:, ks], v_ref[:, ks], vn_ref[:, ks], vc_ref[:, ks]], axis=0), ones], axis=1)
        for g in range(SWA_GROUP):
            hq = kv * SWA_GROUP + g
            hs = pl.ds(hq * HEAD_DIM, HEAD_DIM)
            s = _dot_nt(q_ref[:, hs], kall)
            s = jnp.concatenate([jnp.where(valid, s[:, 0:nloc], NEG_INF), s[:, nloc:]], axis=1)
            sink = sink_ref[hq, 0:1, 0:1] * LOG2E
            m = jnp.maximum(jnp.max(s, axis=-1, keepdims=True), sink)
            acc = _dot(jnp.exp2(s - m).astype(BF16), vall)
            den = acc[:, HEAD_DIM:2 * HEAD_DIM] + jnp.exp2(sink - m)
            y_ref[:, hs] = (acc[:, 0:HEAD_DIM] / den).astype(y_ref.dtype)


def _swa_latent(swa3, swa_ctx3, sink, layer):
    bsz, seq, _ = swa3.shape
    t = swa_ctx3.shape[1]
    w = SWA_WINDOW
    tq = 256 if seq % 256 == 0 else w
    r = tq // w
    nblk = seq // w
    kvw = SWA_KV_HEADS * HEAD_DIM
    qw = SWA_HEADS * HEAD_DIM
    k_col, v_col = qw // kvw, qw // kvw + 1
    prev = lambda i: jnp.maximum(i * r - 1, 0)
    nxt = lambda i: jnp.minimum((i + 1) * r, nblk - 1)
    return pl.pallas_call(
        functools.partial(_swa_kernel, tq=tq, seq=seq),
        grid=(bsz, seq // tq),
        in_specs=[
            pl.BlockSpec((None, tq, qw), lambda b, i: (b, i, 0)),
            pl.BlockSpec((None, tq, kvw), lambda b, i: (b, i, k_col)),
            pl.BlockSpec((None, tq, kvw), lambda b, i: (b, i, v_col)),
            pl.BlockSpec((None, w, kvw), lambda b, i: (b, prev(i), k_col)),
            pl.BlockSpec((None, w, kvw), lambda b, i: (b, prev(i), v_col)),
            pl.BlockSpec((None, w, kvw), lambda b, i: (b, nxt(i), k_col)),
            pl.BlockSpec((None, w, kvw), lambda b, i: (b, nxt(i), v_col)),
            pl.BlockSpec((None, t, kvw), lambda b, i: (b, 0, k_col)),
            pl.BlockSpec((None, t, kvw), lambda b, i: (b, 0, v_col)),
            pl.BlockSpec((None, SWA_HEADS, SUBLANES, LANES), lambda b, i: (layer, 0, 0, 0)),
        ],
        out_specs=pl.BlockSpec((None, tq, qw), lambda b, i: (b, i, 0)),
        out_shape=jax.ShapeDtypeStruct((bsz, seq, qw), BF16),
        name="swa_latent",
        compiler_params=_cparams("parallel", "parallel"),
    )(swa3, swa3, swa3, swa3, swa3, swa3, swa3, swa_ctx3, swa_ctx3, sink)


def _mla_proj_kernel(x_ref, qn_ref, kvn_ref, wq_ref, wk_ref, wv_ref, cos_ref, slo_ref, shi_ref,
                     q_ref, k_ref, v_ref):
    nq = MLA_ROPE_DIM // 4
    cos, slo, shi = cos_ref[...], slo_ref[...], shi_ref[...]
    cq = (_rms(x_ref[:, 0:MLA_Q_RANK]) * qn_ref[...]).astype(BF16)
    q = _dot(cq, wq_ref[...]) * (MLA_SCALE * LOG2E)
    ckv =(_rms(x_ref[:, MLA_Q_RANK:MLA_Q_RANK + MLA_KV_RANK]) * kvn_ref[...]).astype(BF16)
    kn = _dot(ckv, wk_ref[...])
    v_ref[...] = _dot(ckv, wv_ref[...]).astype(v_ref.dtype)
    kr = _rope(x_ref[:, MLA_Q_RANK + MLA_KV_RANK:MLA_COLS_PAD], cos, slo, shi, nq).astype(k_ref.dtype)
    for h in range(MLA_HEADS):
        a = h * MLA_QK_PAD
        q_ref[:, a:a + MLA_NOPE_DIM] = q[:, a:a + MLA_NOPE_DIM].astype(q_ref.dtype)
        q_ref[:, a + MLA_NOPE_DIM:a + MLA_QK_PAD] = _rope(
            q[:, a + MLA_NOPE_DIM:a + MLA_QK_PAD], cos, slo, shi, nq).astype(q_ref.dtype)
        k_ref[:, a:a + MLA_NOPE_DIM] = kn[:, h * MLA_NOPE_DIM:(h + 1) * MLA_NOPE_DIM].astype(k_ref.dtype)
        k_ref[:, a + MLA_NOPE_DIM:a + MLA_QK_PAD] = kr


def _mla_project(mla2, seq, qn, kvn, wq, wk, wv, tabs, layer, tm):
    m_rows = mla2.shape[0]
    cos, slo, shi = tabs
    tpb = seq // tm
    qkw = MLA_HEADS * MLA_QK_PAD
    vw = MLA_HEADS * MLA_V_DIM
    lay = lambda i: (layer, 0, 0)
    tab = pl.BlockSpec((tm, LANES), lambda i: (i % tpb, 0))
    return pl.pallas_call(
        _mla_proj_kernel,
        grid=(m_rows // tm,),
        in_specs=[
            pl.BlockSpec((tm, MLA_COLS_PAD), lambda i: (i, 0)),
            pl.BlockSpec((None, 1, MLA_Q_RANK), lay),
            pl.BlockSpec((None, 1, MLA_KV_RANK), lay),
            pl.BlockSpec((None, MLA_Q_RANK, qkw), lay),
            pl.BlockSpec((None, MLA_KV_RANK, vw), lay),
            pl.BlockSpec((None, MLA_KV_RANK, vw), lay),
            tab, tab, tab,
        ],
        out_specs=[
            pl.BlockSpec((tm, qkw), lambda i: (i, 0)),
            pl.BlockSpec((tm, qkw), lambda i: (i, 0)),
            pl.BlockSpec((tm, vw), lambda i: (i, 0)),
        ],
        out_shape=[
            jax.ShapeDtypeStruct((m_rows, qkw), BF16),
            jax.ShapeDtypeStruct((m_rows, qkw), BF16),
            jax.ShapeDtypeStruct((m_rows, vw), BF16),
        ],
        name="mla_project",
        compiler_params=_cparams("parallel"),
    )(mla2, qn, kvn, wq, wk, wv, cos, slo, shi)


def _attn_kernel(*refs, n_src, q_scale, has_sink, nq):
    q_ref = refs[0]
    k_refs = refs[1:1 + n_src]
    v_refs = refs[1 + n_src:1 + 2 * n_src]
    pos = 1 + 2 * n_src
    sink_ref = refs[pos] if has_sink else None
    sink_b_ref = refs[pos + 1] if has_sink else None
    o_ref = refs[pos + (2 if has_sink else 0)]
    scratch = refs[pos + (3 if has_sink else 1):]
    s_refs = scratch[0:n_src]
    va_refs = scratch[n_src:2 * n_src]
    mb_ref = scratch[2 * n_src]
    t = pl.program_id(0)
    tq = q_ref.shape[0]
    dv = o_ref.shape[-1]
    tks = [s_ref.shape[-1] for s_ref in s_refs]

    @pl.when(t == 0)
    def _():
        for s_ref in s_refs:
            s_ref[...] = jnp.zeros(s_ref.shape, F32)
        mb_ref[...] = jnp.zeros(mb_ref.shape, F32)

    @pl.when(jnp.maximum(t - 1, 0) % nq == 0)
    def _():
        for v_ref, va in zip(v_refs, va_refs):
            va[:, 0:dv] = v_ref[...].astype(BF16)
            va[:, dv:2 * dv] = jnp.ones((va.shape[0], dv), BF16)

    q = q_ref[...]
    if q_scale != 1.0:
        q = q.astype(F32) * q_scale
    q = q.astype(BF16)
    m_prev = mb_ref[...]

    def lane_max(m_vec, s):
        for c in range(s.shape[-1] // LANES):
            m_vec = jnp.maximum(m_vec, s[:, c * LANES:(c + 1) * LANES])
        return m_vec

    m_vec = jnp.full((tq, LANES), NEG_INF, F32)
    acc = jnp.zeros((tq, 2 * dv), F32)
    for k_ref, s_ref, va, tkk in zip(k_refs, s_refs, va_refs, tks):
        for j in range(s_ref.shape[0]):
            rows = pl.ds(j * tkk, tkk)
            p = jnp.exp2(s_ref[j] - jnp.concatenate([m_prev] * (tkk // LANES), axis=1))
            acc = acc + _dot(p.astype(BF16), va[rows, :])
            s = _dot_nt(q, k_ref[rows, :].astype(BF16))
            s_ref[j] = s
            m_vec = lane_max(m_vec, s)

    l = acc[:, dv:2 * dv]
    if has_sink:
        l = l + jnp.exp2(sink_b_ref[0:1, 0:1] * LOG2E - m_prev)
    o_ref[...] = (acc[:, 0:dv] / l).astype(o_ref.dtype)

    m_row = jnp.max(m_vec, axis=-1, keepdims=True)
    if has_sink:
        m_row = jnp.maximum(m_row, sink_ref[0:1, 0:1] * LOG2E)
    mb_ref[...] = jnp.broadcast_to(m_row, (tq, LANES))


def _attention(q3, ks, vs, *, n_heads, group, dq, dv, q_col0, k_col0, v_col0, q_scale, sink=None, layer=0,
               tq=1024, tk=512):
    bsz, lq, _ = q3.shape
    tq = min(tq, lq)
    assert dv == LANES
    n_src = len(ks)
    tks = [min(tk, k.shape[1]) for k in ks]
    nq = lq // tq
    n_blocks = bsz * n_heads * nq

    def cur(t):
        t = jnp.minimum(t, n_blocks - 1)
        return t // (n_heads * nq), (t // nq) % n_heads, t % nq

    def prev(t):
        return cur(jnp.maximum(t - 1, 0))

    def at(fn, spec):
        def index_map(t):
            b, h, i = fn(t)
            return spec(b, h, i)
        return index_map

    in_specs = [pl.BlockSpec((None, tq, dq), at(cur, lambda b, h, i: (b, i, q_col0 + h)))]
    in_specs += [pl.BlockSpec((None, k.shape[1], dq), at(cur, lambda b, h, i: (b, 0, k_col0 + h // group)))
                 for k in ks]
    in_specs += [pl.BlockSpec((None, v.shape[1], dv), at(prev, lambda b, h, i: (b, 0, v_col0 + h // group)))
                 for v in vs]
    args = [q3] + list(ks) + list(vs)
    if sink is not None:
        for fn in (cur, prev):
            in_specs.append(pl.BlockSpec((None, None, SUBLANES, LANES),
                                         at(fn, lambda b, h, i: (layer, h, 0, 0))))
            args.append(sink)
    scratch = [pltpu.VMEM((k.shape[1] // tkk, tq, tkk), F32) for k, tkk in zip(ks, tks)]
    scratch += [pltpu.VMEM((v.shape[1], 2 * dv), BF16) for v in vs]
    scratch += [pltpu.VMEM((tq, LANES), F32)]
    return pl.pallas_call(
        functools.partial(_attn_kernel, n_src=n_src, q_scale=q_scale, has_sink=sink is not None, nq=nq),
        grid=(n_blocks + 1,),
        in_specs=in_specs,
        out_specs=pl.BlockSpec((None, tq, dv), at(prev, lambda b, h, i: (b, i, h))),
        out_shape=jax.ShapeDtypeStruct((bsz, lq, n_heads * dv), BF16),
        scratch_shapes=scratch,
        name="attention",
        compiler_params=_cparams("arbitrary"),
    )(*args)


def _outproj_kernel(yr_ref, ys_ref, ym_ref, x_ref, mod_ref, w_ref, g_ref, b_ref, o_ref, *, alpha):
    a, b = RET_DIM, RET_DIM + SWA_HEADS * HEAD_DIM
    mix = (_dot(yr_ref[...], w_ref[0:a, :]) + _dot(ys_ref[...], w_ref[a:b, :])
           + _dot(ym_ref[...], w_ref[b:MIX_WIDTH, :]))
    z = alpha * x_ref[...] + (1.0 + mod_ref[2:3, :]) * mix
    o_ref[...] = _layer_norm(z, g_ref[...], b_ref[...])


def _outproj_ln(y_ret, y_swa, y_mla, x2, mods, w_o, ln_g, ln_b, layer, mod_row0, rows_per_mod, tm, alpha):
    m_rows, d = x2.shape
    mod_idx = lambda i: (layer * MOD_ROWS + mod_row0 + (i * tm) // rows_per_mod, 0, 0)
    lay = lambda i: (layer, 0, 0)
    row = lambda w: pl.BlockSpec((tm, w), lambda i: (i, 0))
    return pl.pallas_call(
        functools.partial(_outproj_kernel, alpha=alpha),
        grid=(m_rows // tm,),
        in_specs=[
            row(RET_DIM), row(SWA_HEADS * HEAD_DIM), row(MLA_HEADS * MLA_V_DIM), row(d),
            pl.BlockSpec((None, N_MOD, d), mod_idx),
            _resident((None, MIX_WIDTH, d), lay),
            pl.BlockSpec((None, 1, d), lay),
            pl.BlockSpec((None, 1, d), lay),
        ],
        out_specs=row(d),
        out_shape=jax.ShapeDtypeStruct((m_rows, d), F32),
        name="outproj_ln",
        compiler_params=_cparams("parallel"),
    )(y_ret, y_swa, y_mla, x2, mods, w_o, ln_g, ln_b)


def _ffn_kernel(x_hbm, xp_ref, xn_ref, mod_ref, wu_ref, wg_ref, cw_ref, cb_ref, wd_ref, g_ref, b_ref,
                o_ref, xbuf, h_ext, g_ext, sem, *, tm, seq, alpha):
    i = pl.program_id(0)
    f = pl.program_id(1)

    def x_copy(tile):
        rows = pl.ds(pl.multiple_of(tile * tm, tm), tm)
        return pltpu.make_async_copy(x_hbm.at[rows, :], xbuf, sem)

    @pl.when(f == 0)
    def _():
        @pl.when(i == 0)
        def _():
            x_copy(i).start()

        x_copy(i).wait()
        scale = 1.0 + mod_ref[4:5, :]
        shift = mod_ref[3:4, :]
        has_prev = jnp.where((i * tm) % seq != 0, 1.0, 0.0).astype(F32)
        has_next = jnp.where(((i + 1) * tm) % seq != 0, 1.0, 0.0).astype(F32)
        h_ext[0:HALO, :] = ((xp_ref[...] * scale + shift) * has_prev).astype(BF16)
        h_ext[HALO:HALO + tm, :] = (xbuf[...] * scale + shift).astype(BF16)
        h_ext[HALO + tm:tm + 2 * HALO, :] = ((xn_ref[...] * scale + shift) * has_next).astype(BF16)
        o_ref[...] = alpha * xbuf[...]

        @pl.when(i + 1 < pl.num_programs(0))
        def _():
            x_copy(i + 1).start()

    g_ext[...] = _dot(h_ext[...], wg_ref[...])
    u = _dot(h_ext[HALO:HALO + tm, :], wu_ref[...])
    g_prev = g_ext[HALO - 1:HALO - 1 + tm, :]
    g_next = g_ext[HALO + 1:HALO + 1 + tm, :]
    if tm > seq:
        r = lax.broadcasted_iota(jnp.int32, (tm, 1), 0) % seq
        g_prev = jnp.where(r != 0, g_prev, 0.0)
        g_next = jnp.where(r != seq - 1, g_next, 0.0)
    gc = (g_prev * cw_ref[0:1, :] + g_ext[HALO:HALO + tm, :] * cw_ref[1:2, :]
          + g_next * cw_ref[2:3, :] + cb_ref[...])
    a = (_silu(gc) * u).astype(BF16)
    o_ref[...] += (1.0 + mod_ref[5:6, :]) * _dot(a, wd_ref[...])

    @pl.when(f == pl.num_programs(1) - 1)
    def _():
        o_ref[...] = _layer_norm(o_ref[...], g_ref[...], b_ref[...])


def _ffn(x2, seq, mods, w_up, conv_w, conv_b, w_down, ln_g, ln_b, layer, mod_row0, rows_per_mod, tm, alpha):
    m_rows, d = x2.shape
    dff = w_down.shape[1]
    tf = 512 if dff % 512 == 0 else (256 if dff % 256 == 0 else LANES)
    nf = dff // tf
    hb = tm // HALO
    nhb = m_rows // HALO
    assert m_rows % tm == 0 and (tm % seq == 0 or seq % tm == 0)
    mod_idx = lambda i, f: (layer * MOD_ROWS + mod_row0 + (i * tm) // rows_per_mod, 0, 0)
    lay = lambda i, f: (layer, 0, 0)
    return pl.pallas_call(
        functools.partial(_ffn_kernel, tm=tm, seq=seq, alpha=alpha),
        grid=(m_rows // tm, nf),
        in_specs=[
            pl.BlockSpec(memory_space=pl.ANY),
            pl.BlockSpec((HALO, d), lambda i, f: (jnp.maximum(i * hb - 1, 0), 0)),
            pl.BlockSpec((HALO, d), lambda i, f: (jnp.minimum((i + 1) * hb, nhb - 1), 0)),
            pl.BlockSpec((None, N_MOD, d), mod_idx),
            pl.BlockSpec((None, d, tf), lambda i, f: (layer, 0, f)),
            pl.BlockSpec((None, d, tf), lambda i, f: (layer, 0, nf + f)),
            pl.BlockSpec((None, 3, tf), lambda i, f: (layer, 0, f)),
            pl.BlockSpec((None, 1, tf), lambda i, f: (layer, 0, f)),
            pl.BlockSpec((None, tf, d), lambda i, f: (layer, f, 0)),
            pl.BlockSpec((None, 1, d), lay),
            pl.BlockSpec((None, 1, d), lay),
        ],
        out_specs=pl.BlockSpec((tm, d), lambda i, f: (i, 0)),
        out_shape=jax.ShapeDtypeStruct((m_rows, d), F32),
        scratch_shapes=[
            pltpu.VMEM((tm, d), F32),
            pltpu.VMEM((tm + 2 * HALO, d), BF16),
            pltpu.VMEM((tm + 2 * HALO, tf), F32),
            pltpu.SemaphoreType.DMA(()),
        ],
        name="conv_ffn",
        compiler_params=_cparams("arbitrary", "arbitrary"),
    )(x2, x2, x2, mods, w_up, w_up, conv_w, conv_b, w_down, ln_g, ln_b)


def _rope_tables(n_tokens, dim):
    rows = n_tokens // GRID_W
    r = np.repeat(np.arange(rows, dtype=np.float32), GRID_W)
    cc = np.tile(np.arange(GRID_W, dtype=np.float32), rows)
    n_freq = dim // 4
    inv = jnp.asarray(ROPE_THETA, F32) ** (-jnp.arange(n_freq, dtype=F32) / n_freq)
    ang_r = jnp.asarray(r)[:, None] * inv
    ang_c = jnp.asarray(cc)[:, None] * inv
    ang = jnp.concatenate([ang_r, ang_r, ang_c, ang_c], axis=-1)
    cos, sin = jnp.cos(ang), jnp.sin(ang)
    lane = np.arange(dim)
    lo = jnp.asarray((lane % (2 * n_freq)) < n_freq)
    sin_lo = jnp.where(lo, -sin, 0.0)
    sin_hi = jnp.where(lo, 0.0, sin)
    pad = LANES - dim
    if pad:
        cos = jnp.pad(cos, ((0, 0), (0, pad)), constant_values=1.0)
        sin_lo = jnp.pad(sin_lo, ((0, 0), (0, pad)))
        sin_hi = jnp.pad(sin_hi, ((0, 0), (0, pad)))
    return cos, sin_lo, sin_hi


def _identity_tables(n_tokens):
    return (jnp.ones((n_tokens, LANES), F32), jnp.zeros((n_tokens, LANES), F32),
            jnp.zeros((n_tokens, LANES), F32))


def _lane_bcast(p):
    return jnp.broadcast_to(p.astype(F32)[..., None, None], p.shape + (SUBLANES, LANES))


def kernel(x, c, ctx, c_ctx, ada_w, ada_b, w_in, ret_decay_fwd, ret_decay_bwd, swa_sink, mla_q_norm, mla_w_uq,
           mla_kv_norm, mla_w_ukv, w_o, ln1_g, ln1_b, ffn_w_up, ffn_conv_w, ffn_conv_b, ffn_w_down, ln2_g, ln2_b):
    bsz, seq, d = x.shape
    t = ctx.shape[1]
    depth = w_in.shape[0]
    assert bsz + 1 <= MOD_ROWS and seq % RET_CHUNK == 0 and t % RET_CHUNK == 0
    assert seq % GRID_W == 0 and d % LANES == 0
    alpha = (2 * depth) ** 0.25

    w_in_b = jnp.pad(w_in, ((0, 0), (0, 0), (0, IN_WIDTH_PAD - IN_WIDTH))).astype(BF16)
    w_o_b = w_o.astype(BF16)
    w_up_b = ffn_w_up.astype(BF16)
    w_down_b = ffn_w_down.astype(BF16)
    hq = MLA_NOPE_DIM + MLA_ROPE_DIM
    wq = mla_w_uq.reshape(depth, MLA_Q_RANK, MLA_HEADS, hq)
    wq = jnp.pad(wq, ((0, 0), (0, 0), (0, 0), (0, MLA_QK_PAD - hq)))
    wq = wq.reshape(depth, MLA_Q_RANK, MLA_HEADS * MLA_QK_PAD).astype(BF16)
    wkv = mla_w_ukv.reshape(depth, MLA_KV_RANK, MLA_HEADS, MLA_NOPE_DIM + MLA_V_DIM)
    wk = wkv[..., :MLA_NOPE_DIM].reshape(depth, MLA_KV_RANK, MLA_HEADS * MLA_NOPE_DIM).astype(BF16)
    wv = wkv[..., MLA_NOPE_DIM:].reshape(depth, MLA_KV_RANK, MLA_HEADS * MLA_V_DIM).astype(BF16)
    qn = mla_q_norm.reshape(depth, 1, MLA_Q_RANK)
    kvn = mla_kv_norm.reshape(depth, 1, MLA_KV_RANK)
    g1, b1 = ln1_g.reshape(depth, 1, d), ln1_b.reshape(depth, 1, d)
    g2, b2 = ln2_g.reshape(depth, 1, d), ln2_b.reshape(depth, 1, d)
    conv_b = ffn_conv_b.reshape(depth, 1, -1)
    dec = _lane_bcast(jnp.stack([ret_decay_fwd, ret_decay_bwd], axis=1))
    sink = _lane_bcast(swa_sink)

    tabs_h = _rope_tables(seq, HEAD_DIM)
    tabs_m = _rope_tables(seq, MLA_ROPE_DIM)
    tabs_id = _identity_tables(t)

    cond = jnp.zeros((MOD_ROWS, d), F32).at[:bsz].set(c).at[bsz].set(c_ctx)
    mods = _ada_mod(cond, ada_w, ada_b).reshape(depth * MOD_ROWS, N_MOD, d)

    tm_x = 512 if seq % 512 == 0 else RET_CHUNK
    tm_f = 1024 if seq % 1024 == 0 else tm_x
    tm_p = 1024 if seq % 1024 == 0 else tm_x
    tm_c = t if t <= 512 else RET_CHUNK
    seqs = max([j for j in range(1, bsz + 1) if bsz % j == 0 and t * j <= 1024], default=0)
    tm_cf = t * seqs if seqs else tm_c
    zeros_state = jnp.zeros((bsz, RET_HEADS, HEAD_DIM, HEAD_DIM), F32)

    x2 = x.reshape(bsz * seq, d)
    xc2 = ctx.reshape(bsz * t, d)
    for l in range(depth):
        last = l == depth - 1
        ret_c, swa_c, mla_c = _inproj(xc2, t, mods, w_in_b, tabs_id, l, bsz, bsz * t, tm_c)
        swa_c3 = swa_c.reshape(bsz, t, SWA_COLS)
        y_ret_c, st_f, st_b = _retention(ret_c.reshape(-1, bsz, t, HEAD_DIM), dec, l, zeros_state, zeros_state)
        q_c, k_c, v_c = _mla_project(mla_c, t, qn, kvn, wq, wk, wv, tabs_id, l, tm_c)
        k_c3 = k_c.reshape(bsz, t, -1)
        v_c3 = v_c.reshape(bsz, t, -1)

        ret_x, swa_x, mla_x = _inproj(x2, seq, mods, w_in_b, tabs_h, l, 0, seq, tm_x)
        y_ret, _, _ = _retention(ret_x.reshape(-1, bsz, seq, HEAD_DIM), dec, l, st_f, st_b)
        y_swa = _swa_latent(swa_x.reshape(bsz, seq, SWA_COLS), swa_c3, sink, l)
        q_x, k_x, v_x = _mla_project(mla_x, seq, qn, kvn, wq, wk, wv, tabs_m, l, tm_p)
        y_mla = _attention(q_x.reshape(bsz, seq, -1), [k_x.reshape(bsz, seq, -1), k_c3],
                           [v_x.reshape(bsz, seq, -1), v_c3], n_heads=MLA_HEADS, group=1,
                           dq=MLA_QK_PAD, dv=MLA_V_DIM, q_col0=0, k_col0=0, v_col0=0, q_scale=1.0)
        x_a = _outproj_ln(y_ret.reshape(bsz * seq, -1), y_swa.reshape(bsz * seq, -1),
                          y_mla.reshape(bsz * seq, -1), x2, mods, w_o_b, g1, b1, l, 0, seq, tm_x, alpha)
        x_new = _ffn(x_a, seq, mods, w_up_b, ffn_conv_w, conv_b, w_down_b, g2, b2, l, 0, seq, tm_f, alpha)

        if not last:
            y_swa_c = _attention(swa_c3, [swa_c3], [swa_c3], n_heads=SWA_HEADS, group=SWA_GROUP,
                                 dq=HEAD_DIM, dv=HEAD_DIM, q_col0=0, k_col0=SWA_HEADS,
                                 v_col0=SWA_HEADS + SWA_KV_HEADS, q_scale=1.0, sink=sink, layer=l)
            y_mla_c = _attention(q_c.reshape(bsz, t, -1), [k_c3], [v_c3], n_heads=MLA_HEADS, group=1,
                                 dq=MLA_QK_PAD, dv=MLA_V_DIM, q_col0=0, k_col0=0, v_col0=0, q_scale=1.0)
            xc_a = _outproj_ln(y_ret_c.reshape(bsz * t, -1), y_swa_c.reshape(bsz * t, -1),
                               y_mla_c.reshape(bsz * t, -1), xc2, mods, w_o_b, g1, b1, l, bsz, bsz * t,
                               tm_c, alpha)
            xc2 = _ffn(xc_a, t, mods, w_up_b, ffn_conv_w, conv_b, w_down_b, g2, b2, l, bsz, bsz * t,
                       tm_cf, alpha)
        x2 = x_new
    return x2.reshape(bsz, seq, d)
```

```python
import functools

import jax
import jax.numpy as jnp
import numpy as np
from jax import lax
from jax.experimental import pallas as pl
from jax.experimental.pallas import tpu as pltpu

GRID_W = 64
HEAD_DIM = 128
ROPE_THETA = 10000.0
RET_HEADS = 4
RET_DIM = RET_HEADS * HEAD_DIM
RET_CHUNK = 128
SWA_HEADS = 6
SWA_KV_HEADS = 2
SWA_GROUP = SWA_HEADS // SWA_KV_HEADS
SWA_WINDOW = 128
MLA_HEADS = 6
MLA_Q_RANK = 512
MLA_KV_RANK = 256
MLA_NOPE_DIM = 128
MLA_ROPE_DIM = 64
MLA_V_DIM = 128
MLA_SCALE = (MLA_NOPE_DIM + MLA_ROPE_DIM) ** -0.5
MLA_QK_PAD = 256
N_MOD = 6
LN_EPS = 1e-5
RMS_EPS = 1e-6
NEG_INF = -1e30
LOG2E = 1.4426950408889634

RET_COLS = 4 * RET_DIM
SWA_COLS = (SWA_HEADS + 2 * SWA_KV_HEADS) * HEAD_DIM
MLA_COLS = MLA_Q_RANK + MLA_KV_RANK + MLA_ROPE_DIM
MLA_COLS_PAD = 896
IN_WIDTH = RET_COLS + SWA_COLS + MLA_COLS
IN_WIDTH_PAD = RET_COLS + SWA_COLS + MLA_COLS_PAD
MIX_WIDTH = RET_DIM + SWA_HEADS * HEAD_DIM + MLA_HEADS * MLA_V_DIM

LANES = 128
SUBLANES = 8
MOD_ROWS = 8
VMEM_LIMIT = 56 * 1024 * 1024
HALO = 16

BF16 = jnp.bfloat16
F32 = jnp.float32


def _cparams(*sem):
    return pltpu.CompilerParams(dimension_semantics=sem, vmem_limit_bytes=VMEM_LIMIT)


def _resident(block, index_map):
    return pl.BlockSpec(block, index_map, pipeline_mode=pl.Buffered(1))


def _dot(a, b):
    return jnp.dot(a, b, preferred_element_type=F32)


def _dot_nt(a, b):
    return lax.dot_general(a, b, (((1,), (1,)), ((), ())), preferred_element_type=F32)


def _silu(x):
    return x * (1.0 / (1.0 + jnp.exp(-x)))


def _rope(x, cos, sin_lo, sin_hi, nq):
    w = x.shape[-1]
    return x * cos + pltpu.roll(x, w - nq, 1) * sin_lo + pltpu.roll(x, nq, 1) * sin_hi


def _layer_norm(z, g, b):
    mu = jnp.mean(z, axis=-1, keepdims=True)
    zc = z - mu
    var = jnp.mean(zc * zc, axis=-1, keepdims=True)
    return zc * lax.rsqrt(var + LN_EPS) * g + b


def _rms(x):
    return x * lax.rsqrt(jnp.mean(x * x, axis=-1, keepdims=True) + RMS_EPS)


def _log_sigmoid(x):
    return -(jnp.maximum(-x, 0.0) + jnp.log(1.0 + jnp.exp(-jnp.abs(x))))


def _ada_kernel(c_ref, w_ref, b_ref, o_ref):
    sc = _silu(c_ref[...]).astype(BF16)
    o_ref[...] = _dot(sc, w_ref[...].astype(BF16)) + b_ref[...]


def _ada_mod(cond, ada_w, ada_b):
    depth, d, n = ada_w.shape
    tn = next(c for c in (1024, 512, 256, LANES) if n % c == 0)
    return pl.pallas_call(
        _ada_kernel,
        grid=(depth, n // tn),
        in_specs=[
            pl.BlockSpec((MOD_ROWS, d), lambda l, j: (0, 0)),
            pl.BlockSpec((None, d, tn), lambda l, j: (l, 0, j)),
            pl.BlockSpec((None, 1, tn), lambda l, j: (l, 0, j)),
        ],
        out_specs=pl.BlockSpec((None, MOD_ROWS, tn), lambda l, j: (l, 0, j)),
        out_shape=jax.ShapeDtypeStruct((depth, MOD_ROWS, n), F32),
        name="ada_mod",
        compiler_params=_cparams("parallel", "parallel"),
    )(cond, ada_w, ada_b.reshape(depth, 1, n))


def _inproj_kernel(x_ref, mod_ref, w_ref, cos_ref, slo_ref, shi_ref, ret_ref, swa_ref, mla_ref):
    m = mod_ref[...]
    h = (x_ref[...] * (1.0 + m[1:2, :]) + m[0:1, :]).astype(BF16)
    cos, slo, shi = cos_ref[...], slo_ref[...], shi_ref[...]
    nq = HEAD_DIM // 4
    k_scale = HEAD_DIM ** -0.5

    def rope(t, c, scale):
        r = _rope(t[:, c:c + HEAD_DIM], cos, slo, shi, nq)
        return r if scale is None else r * scale

    def rope_heads(t, out_ref, col0, n_heads, scale):
        for hh in range(n_heads):
            c = col0 + hh * HEAD_DIM
            out_ref[:, c:c + HEAD_DIM] = rope(t, c, scale).astype(out_ref.dtype)

    ret = _dot(h, w_ref[:, 0:RET_COLS])
    for j in range(RET_COLS // HEAD_DIM):
        c = j * HEAD_DIM
        if j < RET_HEADS:
            ret_ref[j] = rope(ret, c, None)
        elif j < 2 * RET_HEADS:
            ret_ref[j] = rope(ret, c, k_scale)
        else:
            ret_ref[j] = ret[:, c:c + HEAD_DIM]
    swa = _dot(h, w_ref[:, RET_COLS:RET_COLS + SWA_COLS])
    sq_cols = SWA_HEADS * HEAD_DIM
    sk_cols = SWA_KV_HEADS * HEAD_DIM
    rope_heads(swa, swa_ref, 0, SWA_HEADS, k_scale * LOG2E)
    rope_heads(swa, swa_ref, sq_cols, SWA_KV_HEADS, None)
    swa_ref[:, sq_cols + sk_cols:SWA_COLS] = swa[:, sq_cols + sk_cols:SWA_COLS].astype(swa_ref.dtype)
    mla_ref[...] = _dot(h, w_ref[:, RET_COLS + SWA_COLS:IN_WIDTH_PAD])


def _inproj(x2, seq, mods, w_in, tabs, layer, mod_row0, rows_per_mod, tm):
    m_rows, d = x2.shape
    mod_idx = lambda i: (layer * MOD_ROWS + mod_row0 + (i * tm) // rows_per_mod, 0, 0)
    tpb = seq // tm
    tab = pl.BlockSpec((tm, LANES), lambda i: (i % tpb, 0))
    return pl.pallas_call(
        _inproj_kernel,
        grid=(m_rows // tm,),
        in_specs=[
            pl.BlockSpec((tm, d), lambda i: (i, 0)),
            pl.BlockSpec((None, N_MOD, d), mod_idx),
            _resident((None, d, IN_WIDTH_PAD), lambda i: (layer, 0, 0)),
            tab, tab, tab,
        ],
        out_specs=[
            pl.BlockSpec((RET_COLS // HEAD_DIM, tm, HEAD_DIM), lambda i: (0, i, 0)),
            pl.BlockSpec((tm, SWA_COLS), lambda i: (i, 0)),
            pl.BlockSpec((tm, MLA_COLS_PAD), lambda i: (i, 0)),
        ],
        out_shape=[
            jax.ShapeDtypeStruct((RET_COLS // HEAD_DIM, m_rows, HEAD_DIM), F32),
            jax.ShapeDtypeStruct((m_rows, SWA_COLS), BF16),
            jax.ShapeDtypeStruct((m_rows, MLA_COLS_PAD), F32),
        ],
        name="inproj",
        compiler_params=_cparams("parallel"),
    )(x2, mods, w_in, *tabs)


def _decay_terms(dec_ref):
    lg = _log_sigmoid(dec_ref[...])
    return lg[0, 0:1, 0:1], lg[1, 0:1, 0:1]


def _ret_bwd_kernel(k_ref, v_ref, dec_ref, s0_ref, sin_ref, sfin_ref, state, *, cpb):
    n = pl.program_id(2)
    c = RET_CHUNK

    @pl.when(n == 0)
    def _():
        state[...] = s0_ref[...]

    _, lg_b = _decay_terms(dec_ref)
    pos = lax.broadcasted_iota(jnp.int32, (c, 1), 0).astype(F32)
    kdec = jnp.exp(lg_b * pos)
    cdec = jnp.exp(lg_b * float(c))
    kv = []
    for ci in range(cpb):
        rows = pl.ds(ci * c, c)
        kv.append(_dot((k_ref[rows, :] * kdec).T.astype(BF16), v_ref[rows, :].astype(BF16)))
    st = state[...]
    for ci in reversed(range(cpb)):
        sin_ref[ci] = st.astype(sin_ref.dtype)
        st = cdec * st + kv[ci]
    state[...] = st

    @pl.when(n == pl.num_programs(2) - 1)
    def _():
        sfin_ref[...] = st


def _ret_fwd_kernel(q_ref, k_ref, v_ref, g_ref, dec_ref, s0_ref, sb_ref, y_ref, sfin_ref, state, *, cpb):
    n = pl.program_id(2)
    c = RET_CHUNK

    @pl.when(n == 0)
    def _():
        state[...] = s0_ref[...]

    lg_f, lg_b = _decay_terms(dec_ref)
    pos = lax.broadcasted_iota(jnp.int32, (c, 1), 0).astype(F32)
    ri = lax.broadcasted_iota(jnp.int32, (c, c), 0)
    cj = lax.broadcasted_iota(jnp.int32, (c, c), 1)
    diff = (ri - cj).astype(F32)
    intra = (jnp.where(diff >= 0, jnp.exp(lg_f * jnp.maximum(diff, 0.0)), 0.0)
             + jnp.where(diff <= 0, jnp.exp(lg_b * jnp.maximum(-diff, 0.0)), 0.0))
    qdec_f = jnp.exp(lg_f * (pos + 1.0))
    qdec_b = jnp.exp(lg_b * (float(c) - pos))
    kdec_f = jnp.exp(lg_f * (float(c) - 1.0 - pos))
    cdec_f = jnp.exp(lg_f * float(c))
    lhs, vs, kv = [], [], []
    for ci in range(cpb):
        rows = pl.ds(ci * c, c)
        q, k = q_ref[rows, :], k_ref[rows, :]
        v = v_ref[rows, :].astype(BF16)
        scores = _dot_nt(q.astype(BF16), k.astype(BF16)) * intra
        lhs.append(jnp.concatenate([scores.astype(BF16), (q * qdec_f).astype(BF16),
                                    (q * qdec_b).astype(BF16)], axis=1))
        vs.append(v)
        kv.append(_dot((k * kdec_f).T.astype(BF16), v))
    st = state[...]
    for ci in range(cpb):
        rows = pl.ds(ci * c, c)
        rhs = jnp.concatenate([vs[ci], st.astype(BF16), sb_ref[ci].astype(BF16)], axis=0)
        out = _dot(lhs[ci], rhs)
        y_ref[rows, :] = (_silu(g_ref[rows, :]) * _rms(out)).astype(y_ref.dtype)
        st = cdec_f * st + kv[ci]
    state[...] = st

    @pl.when(n == pl.num_programs(2) - 1)
    def _():
        sfin_ref[...] = st


def _retention(ret3, dec, layer, s0_f, s0_b):
    _, bsz, seq, _ = ret3.shape
    c = RET_CHUNK
    cpb = next(n for n in (16, 8, 4, 2, 1) if seq % (n * c) == 0)
    rows = cpb * c
    nb = seq // rows
    h = RET_HEADS
    grid = (bsz, h, nb)
    state_spec = pl.BlockSpec((None, None, HEAD_DIM, HEAD_DIM), lambda b, hh, n: (b, hh, 0, 0))
    dec_spec = pl.BlockSpec((None, 2, None, SUBLANES, LANES), lambda b, hh, n: (layer, 0, hh, 0, 0))

    def col(j, rev):
        if rev:
            return pl.BlockSpec((None, None, rows, HEAD_DIM), lambda b, hh, n: (j * h + hh, b, nb - 1 - n, 0))
        return pl.BlockSpec((None, None, rows, HEAD_DIM), lambda b, hh, n: (j * h + hh, b, n, 0))

    sb_in, st_b = pl.pallas_call(
        functools.partial(_ret_bwd_kernel, cpb=cpb),
        grid=grid,
        in_specs=[col(1, True), col(2, True), dec_spec, state_spec],
        out_specs=[
            pl.BlockSpec((None, None, cpb, HEAD_DIM, HEAD_DIM), lambda b, hh, n: (b, hh, nb - 1 - n, 0, 0)),
            state_spec,
        ],
        out_shape=[
            jax.ShapeDtypeStruct((bsz, h, seq // c, HEAD_DIM, HEAD_DIM), BF16),
            jax.ShapeDtypeStruct((bsz, h, HEAD_DIM, HEAD_DIM), F32),
        ],
        scratch_shapes=[pltpu.VMEM((HEAD_DIM, HEAD_DIM), F32)],
        name="ret_bwd",
        compiler_params=_cparams("parallel", "parallel", "arbitrary"),
    )(ret3, ret3, dec, s0_b)

    y, st_f = pl.pallas_call(
        functools.partial(_ret_fwd_kernel, cpb=cpb),
        grid=grid,
        in_specs=[col(0, False), col(1, False), col(2, False), col(3, False), dec_spec, state_spec,
                  pl.BlockSpec((None, None, cpb, HEAD_DIM, HEAD_DIM), lambda b, hh, n: (b, hh, n, 0, 0))],
        out_specs=[
            pl.BlockSpec((None, rows, HEAD_DIM), lambda b, hh, n: (b, n, hh)),
            state_spec,
        ],
        out_shape=[
            jax.ShapeDtypeStruct((bsz, seq, RET_DIM), BF16),
            jax.ShapeDtypeStruct((bsz, h, HEAD_DIM, HEAD_DIM), F32),
        ],
        scratch_shapes=[pltpu.VMEM((HEAD_DIM, HEAD_DIM), F32)],
        name="ret_fwd",
        compiler_params=_cparams("parallel", "parallel", "arbitrary"),
    )(ret3, ret3, ret3, ret3, dec, s0_f, sb_in)
    return y, st_f, st_b


def _swa_kernel(q_ref, k_ref, kp_ref, kn_ref, kc_ref, v_ref, vp_ref, vn_ref, vc_ref, sink_ref, y_ref,
                s_ref, mb_ref, *, tq, seq, nq):
    t = pl.program_id(0)
    i = jnp.minimum(t, pl.num_programs(0) - 2) % nq
    w = SWA_WINDOW
    nloc = tq + 2 * w
    nk = nloc + kc_ref.shape[0]

    @pl.when(t == 0)
    def _():
        s_ref[...] = jnp.zeros(s_ref.shape, F32)
        mb_ref[...] = jnp.zeros(mb_ref.shape, F32)

    q_pos = i * tq + lax.broadcasted_iota(jnp.int32, (tq, nloc), 0)
    k_pos = i * tq - w + lax.broadcasted_iota(jnp.int32, (tq, nloc), 1)
    valid = (jnp.abs(k_pos - q_pos) <= w) & (k_pos >= 0) & (k_pos < seq)
    ones = jnp.ones((nk, HEAD_DIM), BF16)
    for kv in range(SWA_KV_HEADS):
        ks = pl.ds(kv * HEAD_DIM, HEAD_DIM)
        kall = jnp.concatenate([kp_ref[:, ks], k_ref[:, ks], kn_ref[:, ks], kc_ref[:, ks]], axis=0)
        vall = jnp.concatenate(
            [jnp.concatenate([vp_ref[:, ks], v_ref[:, ks], vn_ref[:, ks], vc_ref[:, ks]], axis=0), ones], axis=1)
        for g in range(SWA_GROUP):
            hq = kv * SWA_GROUP + g
            hs = pl.ds(hq * HEAD_DIM, HEAD_DIM)
            sink = sink_ref[hq, 0:1, 0:1] * LOG2E
            m_prev = mb_ref[hq]
            p = jnp.exp2(s_ref[hq] - jnp.concatenate([m_prev] * (nk // LANES), axis=1))
            acc = _dot(p.astype(BF16), vall)
            den = acc[:, HEAD_DIM:2 * HEAD_DIM] + jnp.exp2(sink - m_prev)
            y_ref[:, hs] = (acc[:, 0:HEAD_DIM] / den).astype(y_ref.dtype)
            s = _dot_nt(q_ref[:, hs], kall)
            s = jnp.concatenate([jnp.where(valid, s[:, 0:nloc], NEG_INF), s[:, nloc:]], axis=1)
            s_ref[hq] = s
            m = jnp.maximum(jnp.max(s, axis=-1, keepdims=True), sink)
            mb_ref[hq] = jnp.broadcast_to(m, (tq, LANES))


def _swa_latent(swa3, swa_ctx3, sink, layer):
    bsz, seq, _ = swa3.shape
    t_ctx = swa_ctx3.shape[1]
    w = SWA_WINDOW
    tq = 256 if seq % 256 == 0 else w
    r = tq // w
    nblk = seq // w
    nq = seq // tq
    n_blocks = bsz * nq
    kvw = SWA_KV_HEADS * HEAD_DIM
    qw = SWA_HEADS * HEAD_DIM
    k_col, v_col = qw // kvw, qw // kvw + 1
    nk = tq + 2 * w + t_ctx
    assert nk % LANES == 0

    def cur(t):
        t = jnp.minimum(t, n_blocks - 1)
        return t // nq, t % nq

    def prev(t):
        return cur(jnp.maximum(t - 1, 0))

    def at(fn, spec):
        def index_map(t):
            b, i = fn(t)
            return spec(b, i)
        return index_map

    before = lambda i: jnp.maximum(i * r - 1, 0)
    after = lambda i: jnp.minimum((i + 1) * r, nblk - 1)

    def kv_specs(fn, col):
        return [
            pl.BlockSpec((None, tq, kvw), at(fn, lambda b, i: (b, i, col))),
            pl.BlockSpec((None, w, kvw), at(fn, lambda b, i: (b, before(i), col))),
            pl.BlockSpec((None, w, kvw), at(fn, lambda b, i: (b, after(i), col))),
            pl.BlockSpec((None, t_ctx, kvw), at(fn, lambda b, i: (b, 0, col))),
        ]

    return pl.pallas_call(
        functools.partial(_swa_kernel, tq=tq, seq=seq, nq=nq),
        grid=(n_blocks + 1,),
        in_specs=[pl.BlockSpec((None, tq, qw), at(cur, lambda b, i: (b, i, 0)))]
        + kv_specs(cur, k_col) + kv_specs(prev, v_col)
        + [pl.BlockSpec((None, SWA_HEADS, SUBLANES, LANES), lambda t: (layer, 0, 0, 0))],
        out_specs=pl.BlockSpec((None, tq, qw), at(prev, lambda b, i: (b, i, 0))),
        out_shape=jax.ShapeDtypeStruct((bsz, seq, qw), BF16),
        scratch_shapes=[pltpu.VMEM((SWA_HEADS, tq, nk), F32), pltpu.VMEM((SWA_HEADS, tq, LANES), F32)],
        name="swa_latent",
        compiler_params=_cparams("arbitrary"),
    )(swa3, swa3, swa3, swa3, swa_ctx3, swa3, swa3, swa3, swa_ctx3, sink)


def _mla_proj_kernel(x_ref, qn_ref, kvn_ref, wq_ref, wk_ref, wv_ref, cos_ref, slo_ref, shi_ref,
                     q_ref, k_ref, v_ref):
    nq = MLA_ROPE_DIM // 4
    cos, slo, shi = cos_ref[...], slo_ref[...], shi_ref[...]
    cq = (_rms(x_ref[:, 0:MLA_Q_RANK]) * qn_ref[...]).astype(BF16)
    q = _dot(cq, wq_ref[...]) * (MLA_SCALE * LOG2E)
    ckv =(_rms(x_ref[:, MLA_Q_RANK:MLA_Q_RANK + MLA_KV_RANK]) * kvn_ref[...]).astype(BF16)
    kn = _dot(ckv, wk_ref[...])
    v_ref[...] = _dot(ckv, wv_ref[...]).astype(v_ref.dtype)
    kr = _rope(x_ref[:, MLA_Q_RANK + MLA_KV_RANK:MLA_COLS_PAD], cos, slo, shi, nq).astype(k_ref.dtype)
    for h in range(MLA_HEADS):
        a = h * MLA_QK_PAD
        q_ref[:, a:a + MLA_NOPE_DIM] = q[:, a:a + MLA_NOPE_DIM].astype(q_ref.dtype)
        q_ref[:, a + MLA_NOPE_DIM:a + MLA_QK_PAD] = _rope(
            q[:, a + MLA_NOPE_DIM:a + MLA_QK_PAD], cos, slo, shi, nq).astype(q_ref.dtype)
        k_ref[:, a:a + MLA_NOPE_DIM] = kn[:, h * MLA_NOPE_DIM:(h + 1) * MLA_NOPE_DIM].astype(k_ref.dtype)
        k_ref[:, a + MLA_NOPE_DIM:a + MLA_QK_PAD] = kr


def _mla_project(mla2, seq, qn, kvn, wq, wk, wv, tabs, layer, tm):
    m_rows = mla2.shape[0]
    cos, slo, shi = tabs
    tpb = seq // tm
    qkw = MLA_HEADS * MLA_QK_PAD
    vw = MLA_HEADS * MLA_V_DIM
    lay = lambda i: (layer, 0, 0)
    tab = pl.BlockSpec((tm, LANES), lambda i: (i % tpb, 0))
    return pl.pallas_call(
        _mla_proj_kernel,
        grid=(m_rows // tm,),
        in_specs=[
            pl.BlockSpec((tm, MLA_COLS_PAD), lambda i: (i, 0)),
            pl.BlockSpec((None, 1, MLA_Q_RANK), lay),
            pl.BlockSpec((None, 1, MLA_KV_RANK), lay),
            pl.BlockSpec((None, MLA_Q_RANK, qkw), lay),
            pl.BlockSpec((None, MLA_KV_RANK, vw), lay),
            pl.BlockSpec((None, MLA_KV_RANK, vw), lay),
            tab, tab, tab,
        ],
        out_specs=[
            pl.BlockSpec((tm, qkw), lambda i: (i, 0)),
            pl.BlockSpec((tm, qkw), lambda i: (i, 0)),
            pl.BlockSpec((tm, vw), lambda i: (i, 0)),
        ],
        out_shape=[
            jax.ShapeDtypeStruct((m_rows, qkw), BF16),
            jax.ShapeDtypeStruct((m_rows, qkw), BF16),
            jax.ShapeDtypeStruct((m_rows, vw), BF16),
        ],
        name="mla_project",
        compiler_params=_cparams("parallel"),
    )(mla2, qn, kvn, wq, wk, wv, cos, slo, shi)


def _attn_kernel(*refs, n_src, q_scale, has_sink, nq):
    q_ref = refs[0]
    k_refs = refs[1:1 + n_src]
    v_refs = refs[1 + n_src:1 + 2 * n_src]
    pos = 1 + 2 * n_src
    sink_ref = refs[pos] if has_sink else None
    sink_b_ref = refs[pos + 1] if has_sink else None
    o_ref = refs[pos + (2 if has_sink else 0)]
    scratch = refs[pos + (3 if has_sink else 1):]
    s_refs = scratch[0:n_src]
    va_refs = scratch[n_src:2 * n_src]
    mb_ref = scratch[2 * n_src]
    t = pl.program_id(0)
    tq = q_ref.shape[0]
    dv = o_ref.shape[-1]
    tks = [s_ref.shape[-1] for s_ref in s_refs]

    @pl.when(t == 0)
    def _():
        for s_ref in s_refs:
            s_ref[...] = jnp.zeros(s_ref.shape, F32)
        mb_ref[...] = jnp.zeros(mb_ref.shape, F32)

    @pl.when(jnp.maximum(t - 1, 0) % nq == 0)
    def _():
        for v_ref, va in zip(v_refs, va_refs):
            va[:, 0:dv] = v_ref[...].astype(BF16)
            va[:, dv:2 * dv] = jnp.ones((va.shape[0], dv), BF16)

    q = q_ref[...]
    if q_scale != 1.0:
        q = q.astype(F32) * q_scale
    q = q.astype(BF16)
    m_prev = mb_ref[...]

    def lane_max(m_vec, s):
        for c in range(s.shape[-1] // LANES):
            m_vec = jnp.maximum(m_vec, s[:, c * LANES:(c + 1) * LANES])
        return m_vec

    m_vec = jnp.full((tq, LANES), NEG_INF, F32)
    acc = jnp.zeros((tq, 2 * dv), F32)
    for k_ref, s_ref, va, tkk in zip(k_refs, s_refs, va_refs, tks):
        for j in range(s_ref.shape[0]):
            rows = pl.ds(j * tkk, tkk)
            p = jnp.exp2(s_ref[j] - jnp.concatenate([m_prev] * (tkk // LANES), axis=1))
            acc = acc + _dot(p.astype(BF16), va[rows, :])
            s = _dot_nt(q, k_ref[rows, :].astype(BF16))
            s_ref[j] = s
            m_vec = lane_max(m_vec, s)

    l = acc[:, dv:2 * dv]
    if has_sink:
        l = l + jnp.exp2(sink_b_ref[0:1, 0:1] * LOG2E - m_prev)
    o_ref[...] = (acc[:, 0:dv] / l).astype(o_ref.dtype)

    m_row = jnp.max(m_vec, axis=-1, keepdims=True)
    if has_sink:
        m_row = jnp.maximum(m_row, sink_ref[0:1, 0:1] * LOG2E)
    mb_ref[...] = jnp.broadcast_to(m_row, (tq, LANES))


def _attention(q3, ks, vs, *, n_heads, group, dq, dv, q_col0, k_col0, v_col0, q_scale, sink=None, layer=0,
               tq=1024, tk=512):
    bsz, lq, _ = q3.shape
    tq = min(tq, lq)
    assert dv == LANES
    n_src = len(ks)
    tks = [min(tk, k.shape[1]) for k in ks]
    nq = lq // tq
    n_blocks = bsz * n_heads * nq

    def cur(t):
        t = jnp.minimum(t, n_blocks - 1)
        return t // (n_heads * nq), (t // nq) % n_heads, t % nq

    def prev(t):
        return cur(jnp.maximum(t - 1, 0))

    def at(fn, spec):
        def index_map(t):
            b, h, i = fn(t)
            return spec(b, h, i)
        return index_map

    in_specs = [pl.BlockSpec((None, tq, dq), at(cur, lambda b, h, i: (b, i, q_col0 + h)))]
    in_specs += [pl.BlockSpec((None, k.shape[1], dq), at(cur, lambda b, h, i: (b, 0, k_col0 + h // group)))
                 for k in ks]
    in_specs += [pl.BlockSpec((None, v.shape[1], dv), at(prev, lambda b, h, i: (b, 0, v_col0 + h // group)))
                 for v in vs]
    args = [q3] + list(ks) + list(vs)
    if sink is not None:
        for fn in (cur, prev):
            in_specs.append(pl.BlockSpec((None, None, SUBLANES, LANES),
                                         at(fn, lambda b, h, i: (layer, h, 0, 0))))
            args.append(sink)
    scratch = [pltpu.VMEM((k.shape[1] // tkk, tq, tkk), F32) for k, tkk in zip(ks, tks)]
    scratch += [pltpu.VMEM((v.shape[1], 2 * dv), BF16) for v in vs]
    scratch += [pltpu.VMEM((tq, LANES), F32)]
    return pl.pallas_call(
        functools.partial(_attn_kernel, n_src=n_src, q_scale=q_scale, has_sink=sink is not None, nq=nq),
        grid=(n_blocks + 1,),
        in_specs=in_specs,
        out_specs=pl.BlockSpec((None, tq, dv), at(prev, lambda b, h, i: (b, i, h))),
        out_shape=jax.ShapeDtypeStruct((bsz, lq, n_heads * dv), BF16),
        scratch_shapes=scratch,
        name="attention",
        compiler_params=_cparams("arbitrary"),
    )(*args)


def _outproj_kernel(yr_ref, ys_ref, ym_ref, x_ref, mod_ref, w_ref, g_ref, b_ref, o_ref, *, alpha):
    a, b = RET_DIM, RET_DIM + SWA_HEADS * HEAD_DIM
    mix = (_dot(yr_ref[...], w_ref[0:a, :]) + _dot(ys_ref[...], w_ref[a:b, :])
           + _dot(ym_ref[...], w_ref[b:MIX_WIDTH, :]))
    z = alpha * x_ref[...] + (1.0 + mod_ref[2:3, :]) * mix
    o_ref[...] = _layer_norm(z, g_ref[...], b_ref[...])


def _outproj_ln(y_ret, y_swa, y_mla, x2, mods, w_o, ln_g, ln_b, layer, mod_row0, rows_per_mod, tm, alpha):
    m_rows, d = x2.shape
    mod_idx = lambda i: (layer * MOD_ROWS + mod_row0 + (i * tm) // rows_per_mod, 0, 0)
    lay = lambda i: (layer, 0, 0)
    row = lambda w: pl.BlockSpec((tm, w), lambda i: (i, 0))
    return pl.pallas_call(
        functools.partial(_outproj_kernel, alpha=alpha),
        grid=(m_rows // tm,),
        in_specs=[
            row(RET_DIM), row(SWA_HEADS * HEAD_DIM), row(MLA_HEADS * MLA_V_DIM), row(d),
            pl.BlockSpec((None, N_MOD, d), mod_idx),
            _resident((None, MIX_WIDTH, d), lay),
            pl.BlockSpec((None, 1, d), lay),
            pl.BlockSpec((None, 1, d), lay),
        ],
        out_specs=row(d),
        out_shape=jax.ShapeDtypeStruct((m_rows, d), F32),
        name="outproj_ln",
        compiler_params=_cparams("parallel"),
    )(y_ret, y_swa, y_mla, x2, mods, w_o, ln_g, ln_b)


def _ffn_kernel(x_hbm, xp_ref, xn_ref, mod_ref, wu_ref, wg_ref, cw_ref, cb_ref, wd_ref, g_ref, b_ref,
                o_ref, xbuf, h_ext, g_ext, sem, *, tm, seq, alpha):
    i = pl.program_id(0)
    f = pl.program_id(1)

    def x_copy(tile):
        rows = pl.ds(pl.multiple_of(tile * tm, tm), tm)
        return pltpu.make_async_copy(x_hbm.at[rows, :], xbuf, sem)

    @pl.when(f == 0)
    def _():
        @pl.when(i == 0)
        def _():
            x_copy(i).start()

        x_copy(i).wait()
        scale = 1.0 + mod_ref[4:5, :]
        shift = mod_ref[3:4, :]
        has_prev = jnp.where((i * tm) % seq != 0, 1.0, 0.0).astype(F32)
        has_next = jnp.where(((i + 1) * tm) % seq != 0, 1.0, 0.0).astype(F32)
        h_ext[0:HALO, :] = ((xp_ref[...] * scale + shift) * has_prev).astype(BF16)
        h_ext[HALO:HALO + tm, :] = (xbuf[...] * scale + shift).astype(BF16)
        h_ext[HALO + tm:tm + 2 * HALO, :] = ((xn_ref[...] * scale + shift) * has_next).astype(BF16)
        o_ref[...] = alpha * xbuf[...]

        @pl.when(i + 1 < pl.num_programs(0))
        def _():
            x_copy(i + 1).start()

    g_ext[...] = _dot(h_ext[...], wg_ref[...])
    u = _dot(h_ext[HALO:HALO + tm, :], wu_ref[...])
    g_prev = g_ext[HALO - 1:HALO - 1 + tm, :]
    g_next = g_ext[HALO + 1:HALO + 1 + tm, :]
    if tm > seq:
        r = lax.broadcasted_iota(jnp.int32, (tm, 1), 0) % seq
        g_prev = jnp.where(r != 0, g_prev, 0.0)
        g_next = jnp.where(r != seq - 1, g_next, 0.0)
    gc = (g_prev * cw_ref[0:1, :] + g_ext[HALO:HALO + tm, :] * cw_ref[1:2, :]
          + g_next * cw_ref[2:3, :] + cb_ref[...])
    a = (_silu(gc) * u).astype(BF16)
    o_ref[...] += (1.0 + mod_ref[5:6, :]) * _dot(a, wd_ref[...])

    @pl.when(f == pl.num_programs(1) - 1)
    def _():
        o_ref[...] = _layer_norm(o_ref[...], g_ref[...], b_ref[...])


def _ffn(x2, seq, mods, w_up, conv_w, conv_b, w_down, ln_g, ln_b, layer, mod_row0, rows_per_mod, tm, alpha):
    m_rows, d = x2.shape
    dff = w_down.shape[1]
    tf = 512 if dff % 512 == 0 else (256 if dff % 256 == 0 else LANES)
    nf = dff // tf
    hb = tm // HALO
    nhb = m_rows // HALO
    assert m_rows % tm == 0 and (tm % seq == 0 or seq % tm == 0)
    mod_idx = lambda i, f: (layer * MOD_ROWS + mod_row0 + (i * tm) // rows_per_mod, 0, 0)
    lay = lambda i, f: (layer, 0, 0)
    return pl.pallas_call(
        functools.partial(_ffn_kernel, tm=tm, seq=seq, alpha=alpha),
        grid=(m_rows // tm, nf),
        in_specs=[
            pl.BlockSpec(memory_space=pl.ANY),
            pl.BlockSpec((HALO, d), lambda i, f: (jnp.maximum(i * hb - 1, 0), 0)),
            pl.BlockSpec((HALO, d), lambda i, f: (jnp.minimum((i + 1) * hb, nhb - 1), 0)),
            pl.BlockSpec((None, N_MOD, d), mod_idx),
            pl.BlockSpec((None, d, tf), lambda i, f: (layer, 0, f)),
            pl.BlockSpec((None, d, tf), lambda i, f: (layer, 0, nf + f)),
            pl.BlockSpec((None, 3, tf), lambda i, f: (layer, 0, f)),
            pl.BlockSpec((None, 1, tf), lambda i, f: (layer, 0, f)),
            pl.BlockSpec((None, tf, d), lambda i, f: (layer, f, 0)),
            pl.BlockSpec((None, 1, d), lay),
            pl.BlockSpec((None, 1, d), lay),
        ],
        out_specs=pl.BlockSpec((tm, d), lambda i, f: (i, 0)),
        out_shape=jax.ShapeDtypeStruct((m_rows, d), F32),
        scratch_shapes=[
            pltpu.VMEM((tm, d), F32),
            pltpu.VMEM((tm + 2 * HALO, d), BF16),
            pltpu.VMEM((tm + 2 * HALO, tf), F32),
            pltpu.SemaphoreType.DMA(()),
        ],
        name="conv_ffn",
        compiler_params=_cparams("arbitrary", "arbitrary"),
    )(x2, x2, x2, mods, w_up, w_up, conv_w, conv_b, w_down, ln_g, ln_b)


def _rope_tables(n_tokens, dim):
    rows = n_tokens // GRID_W
    r = np.repeat(np.arange(rows, dtype=np.float32), GRID_W)
    cc = np.tile(np.arange(GRID_W, dtype=np.float32), rows)
    n_freq = dim // 4
    inv = jnp.asarray(ROPE_THETA, F32) ** (-jnp.arange(n_freq, dtype=F32) / n_freq)
    ang_r = jnp.asarray(r)[:, None] * inv
    ang_c = jnp.asarray(cc)[:, None] * inv
    ang = jnp.concatenate([ang_r, ang_r, ang_c, ang_c], axis=-1)
    cos, sin = jnp.cos(ang), jnp.sin(ang)
    lane = np.arange(dim)
    lo = jnp.asarray((lane % (2 * n_freq)) < n_freq)
    sin_lo = jnp.where(lo, -sin, 0.0)
    sin_hi = jnp.where(lo, 0.0, sin)
    pad = LANES - dim
    if pad:
        cos = jnp.pad(cos, ((0, 0), (0, pad)), constant_values=1.0)
        sin_lo = jnp.pad(sin_lo, ((0, 0), (0, pad)))
        sin_hi = jnp.pad(sin_hi, ((0, 0), (0, pad)))
    return cos, sin_lo, sin_hi


def _identity_tables(n_tokens):
    return (jnp.ones((n_tokens, LANES), F32), jnp.zeros((n_tokens, LANES), F32),
            jnp.zeros((n_tokens, LANES), F32))


def _lane_bcast(p):
    return jnp.broadcast_to(p.astype(F32)[..., None, None], p.shape + (SUBLANES, LANES))


def kernel(x, c, ctx, c_ctx, ada_w, ada_b, w_in, ret_decay_fwd, ret_decay_bwd, swa_sink, mla_q_norm, mla_w_uq,
           mla_kv_norm, mla_w_ukv, w_o, ln1_g, ln1_b, ffn_w_up, ffn_conv_w, ffn_conv_b, ffn_w_down, ln2_g, ln2_b):
    bsz, seq, d = x.shape
    t = ctx.shape[1]
    depth = w_in.shape[0]
    assert bsz + 1 <= MOD_ROWS and seq % RET_CHUNK == 0 and t % RET_CHUNK == 0
    assert seq % GRID_W == 0 and d % LANES == 0
    alpha = (2 * depth) ** 0.25

    w_in_b = jnp.pad(w_in, ((0, 0), (0, 0), (0, IN_WIDTH_PAD - IN_WIDTH))).astype(BF16)
    w_o_b = w_o.astype(BF16)
    w_up_b = ffn_w_up.astype(BF16)
    w_down_b = ffn_w_down.astype(BF16)
    hq = MLA_NOPE_DIM + MLA_ROPE_DIM
    wq = mla_w_uq.reshape(depth, MLA_Q_RANK, MLA_HEADS, hq)
    wq = jnp.pad(wq, ((0, 0), (0, 0), (0, 0), (0, MLA_QK_PAD - hq)))
    wq = wq.reshape(depth, MLA_Q_RANK, MLA_HEADS * MLA_QK_PAD).astype(BF16)
    wkv = mla_w_ukv.reshape(depth, MLA_KV_RANK, MLA_HEADS, MLA_NOPE_DIM + MLA_V_DIM)
    wk = wkv[..., :MLA_NOPE_DIM].reshape(depth, MLA_KV_RANK, MLA_HEADS * MLA_NOPE_DIM).astype(BF16)
    wv = wkv[..., MLA_NOPE_DIM:].reshape(depth, MLA_KV_RANK, MLA_HEADS * MLA_V_DIM).astype(BF16)
    qn = mla_q_norm.reshape(depth, 1, MLA_Q_RANK)
    kvn = mla_kv_norm.reshape(depth, 1, MLA_KV_RANK)
    g1, b1 = ln1_g.reshape(depth, 1, d), ln1_b.reshape(depth, 1, d)
    g2, b2 = ln2_g.reshape(depth, 1, d), ln2_b.reshape(depth, 1, d)
    conv_b = ffn_conv_b.reshape(depth, 1, -1)
    dec = _lane_bcast(jnp.stack([ret_decay_fwd, ret_decay_bwd], axis=1))
    sink = _lane_bcast(swa_sink)

    tabs_h = _rope_tables(seq, HEAD_DIM)
    tabs_m = _rope_tables(seq, MLA_ROPE_DIM)
    tabs_id = _identity_tables(t)

    cond = jnp.zeros((MOD_ROWS, d), F32).at[:bsz].set(c).at[bsz].set(c_ctx)
    mods = _ada_mod(cond, ada_w, ada_b).reshape(depth * MOD_ROWS, N_MOD, d)

    tm_x = 512 if seq % 512 == 0 else RET_CHUNK
    tm_f = 1024 if seq % 1024 == 0 else tm_x
    tm_p = 1024 if seq % 1024 == 0 else tm_x
    tm_c = t if t <= 512 else RET_CHUNK
    seqs = max([j for j in range(1, bsz + 1) if bsz % j == 0 and t * j <= 1024], default=0)
    tm_cf = t * seqs if seqs else tm_c
    zeros_state = jnp.zeros((bsz, RET_HEADS, HEAD_DIM, HEAD_DIM), F32)

    x2 = x.reshape(bsz * seq, d)
    xc2 = ctx.reshape(bsz * t, d)
    for l in range(depth):
        last = l == depth - 1
        ret_c, swa_c, mla_c = _inproj(xc2, t, mods, w_in_b, tabs_id, l, bsz, bsz * t, tm_c)
        swa_c3 = swa_c.reshape(bsz, t, SWA_COLS)
        y_ret_c, st_f, st_b = _retention(ret_c.reshape(-1, bsz, t, HEAD_DIM), dec, l, zeros_state, zeros_state)
        q_c, k_c, v_c = _mla_project(mla_c, t, qn, kvn, wq, wk, wv, tabs_id, l, tm_c)
        k_c3 = k_c.reshape(bsz, t, -1)
        v_c3 = v_c.reshape(bsz, t, -1)

        ret_x, swa_x, mla_x = _inproj(x2, seq, mods, w_in_b, tabs_h, l, 0, seq, tm_x)
        y_ret, _, _ = _retention(ret_x.reshape(-1, bsz, seq, HEAD_DIM), dec, l, st_f, st_b)
        y_swa = _swa_latent(swa_x.reshape(bsz, seq, SWA_COLS), swa_c3, sink, l)
        q_x, k_x, v_x = _mla_project(mla_x, seq, qn, kvn, wq, wk, wv, tabs_m, l, tm_p)
        y_mla = _attention(q_x.reshape(bsz, seq, -1), [k_x.reshape(bsz, seq, -1), k_c3],
                           [v_x.reshape(bsz, seq, -1), v_c3], n_heads=MLA_HEADS, group=1,
                           dq=MLA_QK_PAD, dv=MLA_V_DIM, q_col0=0, k_col0=0, v_col0=0, q_scale=1.0)
        x_a = _outproj_ln(y_ret.reshape(bsz * seq, -1), y_swa.reshape(bsz * seq, -1),
                          y_mla.reshape(bsz * seq, -1), x2, mods, w_o_b, g1, b1, l, 0, seq, tm_x, alpha)
        x_new = _ffn(x_a, seq, mods, w_up_b, ffn_conv_w, conv_b, w_down_b, g2, b2, l, 0, seq, tm_f, alpha)

        if not last:
            y_swa_c = _attention(swa_c3, [swa_c3], [swa_c3], n_heads=SWA_HEADS, group=SWA_GROUP,
                                 dq=HEAD_DIM, dv=HEAD_DIM, q_col0=0, k_col0=SWA_HEADS,
                                 v_col0=SWA_HEADS + SWA_KV_HEADS, q_scale=1.0, sink=sink, layer=l)
            y_mla_c = _attention(q_c.reshape(bsz, t, -1), [k_c3], [v_c3], n_heads=MLA_HEADS, group=1,
                                 dq=MLA_QK_PAD, dv=MLA_V_DIM, q_col0=0, k_col0=0, v_col0=0, q_scale=1.0)
            xc_a = _outproj_ln(y_ret_c.reshape(bsz * t, -1), y_swa_c.reshape(bsz * t, -1),
                               y_mla_c.reshape(bsz * t, -1), xc2, mods, w_o_b, g1, b1, l, bsz, bsz * t,
                               tm_c, alpha)
            xc2 = _ffn(xc_a, t, mods, w_up_b, ffn_conv_w, conv_b, w_down_b, g2, b2, l, bsz, bsz * t,
                       tm_cf, alpha)
        x2 = x_new
    return x2.reshape(bsz, seq, d)
```

```python
import functools

import jax
import jax.numpy as jnp
import numpy as np
from jax import lax
from jax.experimental import pallas as pl
from jax.experimental.pallas import tpu as pltpu

GRID_W = 64
HEAD_DIM = 128
ROPE_THETA = 10000.0
RET_HEADS = 4
RET_DIM = RET_HEADS * HEAD_DIM
RET_CHUNK = 128
SWA_HEADS = 6
SWA_KV_HEADS = 2
SWA_GROUP = SWA_HEADS // SWA_KV_HEADS
SWA_WINDOW = 128
MLA_HEADS = 6
MLA_Q_RANK = 512
MLA_KV_RANK = 256
MLA_NOPE_DIM = 128
MLA_ROPE_DIM = 64
MLA_V_DIM = 128
MLA_SCALE = (MLA_NOPE_DIM + MLA_ROPE_DIM) ** -0.5
MLA_QK_PAD = 256
N_MOD = 6
LN_EPS = 1e-5
RMS_EPS = 1e-6
NEG_INF = -1e30
LOG2E = 1.4426950408889634

RET_COLS = 4 * RET_DIM
SWA_COLS = (SWA_HEADS + 2 * SWA_KV_HEADS) * HEAD_DIM
MLA_COLS = MLA_Q_RANK + MLA_KV_RANK + MLA_ROPE_DIM
MLA_COLS_PAD = 896
IN_WIDTH = RET_COLS + SWA_COLS + MLA_COLS
IN_WIDTH_PAD = RET_COLS + SWA_COLS + MLA_COLS_PAD
MIX_WIDTH = RET_DIM + SWA_HEADS * HEAD_DIM + MLA_HEADS * MLA_V_DIM

LANES = 128
SUBLANES = 8
MOD_ROWS = 8
VMEM_LIMIT = 56 * 1024 * 1024
HALO = 16

BF16 = jnp.bfloat16
F32 = jnp.float32


def _cparams(*sem):
    return pltpu.CompilerParams(dimension_semantics=sem, vmem_limit_bytes=VMEM_LIMIT)


def _resident(block, index_map):
    return pl.BlockSpec(block, index_map, pipeline_mode=pl.Buffered(1))


def _dot(a, b):
    return jnp.dot(a, b, preferred_element_type=F32)


def _dot_nt(a, b):
    return lax.dot_general(a, b, (((1,), (1,)), ((), ())), preferred_element_type=F32)


def _silu(x):
    return x * (1.0 / (1.0 + jnp.exp(-x)))


def _rope(x, cos, sin_lo, sin_hi, nq):
    w = x.shape[-1]
    return x * cos + pltpu.roll(x, w - nq, 1) * sin_lo + pltpu.roll(x, nq, 1) * sin_hi


def _layer_norm(z, g, b):
    mu = jnp.mean(z, axis=-1, keepdims=True)
    zc = z - mu
    var = jnp.mean(zc * zc, axis=-1, keepdims=True)
    return zc * lax.rsqrt(var + LN_EPS) * g + b


def _rms(x):
    return x * lax.rsqrt(jnp.mean(x * x, axis=-1, keepdims=True) + RMS_EPS)


def _log_sigmoid(x):
    return -(jnp.maximum(-x, 0.0) + jnp.log(1.0 + jnp.exp(-jnp.abs(x))))


def _ada_kernel(c_ref, w_ref, b_ref, o_ref):
    sc = _silu(c_ref[...]).astype(BF16)
    o_ref[...] = _dot(sc, w_ref[...].astype(BF16)) + b_ref[...]


def _ada_mod(cond, ada_w, ada_b):
    depth, d, n = ada_w.shape
    tn = next(c for c in (2048, 1024, 512, 256, LANES) if n % c == 0)
    return pl.pallas_call(
        _ada_kernel,
        grid=(depth, n // tn),
        in_specs=[
            pl.BlockSpec((MOD_ROWS, d), lambda l, j: (0, 0)),
            pl.BlockSpec((None, d, tn), lambda l, j: (l, 0, j)),
            pl.BlockSpec((None, 1, tn), lambda l, j: (l, 0, j)),
        ],
        out_specs=pl.BlockSpec((None, MOD_ROWS, tn), lambda l, j: (l, 0, j)),
        out_shape=jax.ShapeDtypeStruct((depth, MOD_ROWS, n), F32),
        name="ada_mod",
        compiler_params=_cparams("parallel", "parallel"),
    )(cond, ada_w, ada_b.reshape(depth, 1, n))


def _inproj_kernel(x_ref, mod_ref, w_ref, cos_ref, slo_ref, shi_ref, ret_ref, swa_ref, mla_ref):
    m = mod_ref[...]
    h = (x_ref[...] * (1.0 + m[1:2, :]) + m[0:1, :]).astype(BF16)
    cos, slo, shi = cos_ref[...], slo_ref[...], shi_ref[...]
    nq = HEAD_DIM // 4
    k_scale = HEAD_DIM ** -0.5

    def rope(t, c, scale):
        r = _rope(t[:, c:c + HEAD_DIM], cos, slo, shi, nq)
        return r if scale is None else r * scale

    def rope_heads(t, out_ref, col0, n_heads, scale):
        for hh in range(n_heads):
            c = col0 + hh * HEAD_DIM
            out_ref[:, c:c + HEAD_DIM] = rope(t, c, scale).astype(out_ref.dtype)

    ret = _dot(h, w_ref[:, 0:RET_COLS])
    for j in range(RET_COLS // HEAD_DIM):
        c = j * HEAD_DIM
        if j < RET_HEADS:
            ret_ref[j] = rope(ret, c, None)
        elif j < 2 * RET_HEADS:
            ret_ref[j] = rope(ret, c, k_scale)
        else:
            ret_ref[j] = ret[:, c:c + HEAD_DIM]
    swa = _dot(h, w_ref[:, RET_COLS:RET_COLS + SWA_COLS])
    sq_cols = SWA_HEADS * HEAD_DIM
    sk_cols = SWA_KV_HEADS * HEAD_DIM
    rope_heads(swa, swa_ref, 0, SWA_HEADS, k_scale * LOG2E)
    rope_heads(swa, swa_ref, sq_cols, SWA_KV_HEADS, None)
    swa_ref[:, sq_cols + sk_cols:SWA_COLS] = swa[:, sq_cols + sk_cols:SWA_COLS].astype(swa_ref.dtype)
    mla_ref[...] = _dot(h, w_ref[:, RET_COLS + SWA_COLS:IN_WIDTH_PAD])


def _inproj(x2, seq, mods, w_in, tabs, layer, mod_row0, rows_per_mod, tm):
    m_rows, d = x2.shape
    mod_idx = lambda i: (layer * MOD_ROWS + mod_row0 + (i * tm) // rows_per_mod, 0, 0)
    tpb = seq // tm
    tab = pl.BlockSpec((tm, LANES), lambda i: (i % tpb, 0))
    return pl.pallas_call(
        _inproj_kernel,
        grid=(m_rows // tm,),
        in_specs=[
            pl.BlockSpec((tm, d), lambda i: (i, 0)),
            pl.BlockSpec((None, N_MOD, d), mod_idx),
            _resident((None, d, IN_WIDTH_PAD), lambda i: (layer, 0, 0)),
            tab, tab, tab,
        ],
        out_specs=[
            pl.BlockSpec((RET_COLS // HEAD_DIM, tm, HEAD_DIM), lambda i: (0, i, 0)),
            pl.BlockSpec((tm, SWA_COLS), lambda i: (i, 0)),
            pl.BlockSpec((tm, MLA_COLS_PAD), lambda i: (i, 0)),
        ],
        out_shape=[
            jax.ShapeDtypeStruct((RET_COLS // HEAD_DIM, m_rows, HEAD_DIM), F32),
            jax.ShapeDtypeStruct((m_rows, SWA_COLS), BF16),
            jax.ShapeDtypeStruct((m_rows, MLA_COLS_PAD), F32),
        ],
        name="inproj",
        compiler_params=_cparams("parallel"),
    )(x2, mods, w_in, *tabs)


def _decay_terms(dec_ref):
    lg = _log_sigmoid(dec_ref[...])
    return lg[0, 0:1, 0:1], lg[1, 0:1, 0:1]


def _ret_bwd_kernel(k_ref, v_ref, dec_ref, s0_ref, sin_ref, sfin_ref, state, *, cpb):
    n = pl.program_id(2)
    c = RET_CHUNK

    @pl.when(n == 0)
    def _():
        state[...] = s0_ref[...]

    _, lg_b = _decay_terms(dec_ref)
    pos = lax.broadcasted_iota(jnp.int32, (c, 1), 0).astype(F32)
    kdec = jnp.exp(lg_b * pos)
    cdec = jnp.exp(lg_b * float(c))
    kv = []
    for ci in range(cpb):
        rows = pl.ds(ci * c, c)
        kv.append(_dot((k_ref[rows, :] * kdec).T.astype(BF16), v_ref[rows, :].astype(BF16)))
    st = state[...]
    for ci in reversed(range(cpb)):
        sin_ref[ci] = st.astype(sin_ref.dtype)
        st = cdec * st + kv[ci]
    state[...] = st

    @pl.when(n == pl.num_programs(2) - 1)
    def _():
        sfin_ref[...] = st


def _ret_fwd_kernel(q_ref, k_ref, v_ref, g_ref, dec_ref, s0_ref, sb_ref, y_ref, sfin_ref, state, *, cpb):
    n = pl.program_id(2)
    c = RET_CHUNK

    @pl.when(n == 0)
    def _():
        state[...] = s0_ref[...]

    lg_f, lg_b = _decay_terms(dec_ref)
    pos = lax.broadcasted_iota(jnp.int32, (c, 1), 0).astype(F32)
    ri = lax.broadcasted_iota(jnp.int32, (c, c), 0)
    cj = lax.broadcasted_iota(jnp.int32, (c, c), 1)
    diff = (ri - cj).astype(F32)
    intra = (jnp.where(diff >= 0, jnp.exp(lg_f * jnp.maximum(diff, 0.0)), 0.0)
             + jnp.where(diff <= 0, jnp.exp(lg_b * jnp.maximum(-diff, 0.0)), 0.0))
    qdec_f = jnp.exp(lg_f * (pos + 1.0))
    qdec_b = jnp.exp(lg_b * (float(c) - pos))
    kdec_f = jnp.exp(lg_f * (float(c) - 1.0 - pos))
    cdec_f = jnp.exp(lg_f * float(c))
    lhs, vs, kv = [], [], []
    for ci in range(cpb):
        rows = pl.ds(ci * c, c)
        q, k = q_ref[rows, :], k_ref[rows, :]
        v = v_ref[rows, :].astype(BF16)
        scores = _dot_nt(q.astype(BF16), k.astype(BF16)) * intra
        lhs.append(jnp.concatenate([scores.astype(BF16), (q * qdec_f).astype(BF16),
                                    (q * qdec_b).astype(BF16)], axis=1))
        vs.append(v)
        kv.append(_dot((k * kdec_f).T.astype(BF16), v))
    st = state[...]
    for ci in range(cpb):
        rows = pl.ds(ci * c, c)
        rhs = jnp.concatenate([vs[ci], st.astype(BF16), sb_ref[ci].astype(BF16)], axis=0)
        out = _dot(lhs[ci], rhs)
        y_ref[rows, :] = (_silu(g_ref[rows, :]) * _rms(out)).astype(y_ref.dtype)
        st = cdec_f * st + kv[ci]
    state[...] = st

    @pl.when(n == pl.num_programs(2) - 1)
    def _():
        sfin_ref[...] = st


def _retention(ret3, dec, layer, s0_f, s0_b):
    _, bsz, seq, _ = ret3.shape
    c = RET_CHUNK
    cpb = next(n for n in (32, 16, 8, 4, 2, 1) if seq % (n * c) == 0)
    rows = cpb * c
    nb = seq // rows
    h = RET_HEADS
    grid = (bsz, h, nb)
    state_spec = pl.BlockSpec((None, None, HEAD_DIM, HEAD_DIM), lambda b, hh, n: (b, hh, 0, 0))
    dec_spec = pl.BlockSpec((None, 2, None, SUBLANES, LANES), lambda b, hh, n: (layer, 0, hh, 0, 0))

    def col(j, rev):
        if rev:
            return pl.BlockSpec((None, None, rows, HEAD_DIM), lambda b, hh, n: (j * h + hh, b, nb - 1 - n, 0))
        return pl.BlockSpec((None, None, rows, HEAD_DIM), lambda b, hh, n: (j * h + hh, b, n, 0))

    sb_in, st_b = pl.pallas_call(
        functools.partial(_ret_bwd_kernel, cpb=cpb),
        grid=grid,
        in_specs=[col(1, True), col(2, True), dec_spec, state_spec],
        out_specs=[
            pl.BlockSpec((None, None, cpb, HEAD_DIM, HEAD_DIM), lambda b, hh, n: (b, hh, nb - 1 - n, 0, 0)),
            state_spec,
        ],
        out_shape=[
            jax.ShapeDtypeStruct((bsz, h, seq // c, HEAD_DIM, HEAD_DIM), BF16),
            jax.ShapeDtypeStruct((bsz, h, HEAD_DIM, HEAD_DIM), F32),
        ],
        scratch_shapes=[pltpu.VMEM((HEAD_DIM, HEAD_DIM), F32)],
        name="ret_bwd",
        compiler_params=_cparams("parallel", "parallel", "arbitrary"),
    )(ret3, ret3, dec, s0_b)

    y, st_f = pl.pallas_call(
        functools.partial(_ret_fwd_kernel, cpb=cpb),
        grid=grid,
        in_specs=[col(0, False), col(1, False), col(2, False), col(3, False), dec_spec, state_spec,
                  pl.BlockSpec((None, None, cpb, HEAD_DIM, HEAD_DIM), lambda b, hh, n: (b, hh, n, 0, 0))],
        out_specs=[
            pl.BlockSpec((None, rows, HEAD_DIM), lambda b, hh, n: (b, n, hh)),
            state_spec,
        ],
        out_shape=[
            jax.ShapeDtypeStruct((bsz, seq, RET_DIM), BF16),
            jax.ShapeDtypeStruct((bsz, h, HEAD_DIM, HEAD_DIM), F32),
        ],
        scratch_shapes=[pltpu.VMEM((HEAD_DIM, HEAD_DIM), F32)],
        name="ret_fwd",
        compiler_params=_cparams("parallel", "parallel", "arbitrary"),
    )(ret3, ret3, ret3, ret3, dec, s0_f, sb_in)
    return y, st_f, st_b


def _swa_kernel(q_ref, k_ref, kp_ref, kn_ref, kc_ref, v_ref, vp_ref, vn_ref, vc_ref, sink_ref, y_ref,
                s_ref, mb_ref, *, tq, seq, nq):
    t = pl.program_id(0)
    i = jnp.minimum(t, pl.num_programs(0) - 2) % nq
    w = SWA_WINDOW
    nloc = tq + 2 * w
    nk = nloc + kc_ref.shape[0]

    @pl.when(t == 0)
    def _():
        s_ref[...] = jnp.zeros(s_ref.shape, F32)
        mb_ref[...] = jnp.zeros(mb_ref.shape, F32)

    q_pos = i * tq + lax.broadcasted_iota(jnp.int32, (tq, nloc), 0)
    k_pos = i * tq - w + lax.broadcasted_iota(jnp.int32, (tq, nloc), 1)
    valid = (jnp.abs(k_pos - q_pos) <= w) & (k_pos >= 0) & (k_pos < seq)
    ones = jnp.ones((nk, HEAD_DIM), BF16)
    for kv in range(SWA_KV_HEADS):
        ks = pl.ds(kv * HEAD_DIM, HEAD_DIM)
        kall = jnp.concatenate([kp_ref[:, ks], k_ref[:, ks], kn_ref[:, ks], kc_ref[:, ks]], axis=0)
        vall = jnp.concatenate(
            [jnp.concatenate([vp_ref[:, ks], v_ref[:, ks], vn_ref[:, ks], vc_ref[:, ks]], axis=0), ones], axis=1)
        for g in range(SWA_GROUP):
            hq = kv * SWA_GROUP + g
            hs = pl.ds(hq * HEAD_DIM, HEAD_DIM)
            sink = sink_ref[hq, 0:1, 0:1] * LOG2E
            m_prev = mb_ref[hq]
            p = jnp.exp2(s_ref[hq] - jnp.concatenate([m_prev] * (nk // LANES), axis=1))
            acc = _dot(p.astype(BF16), vall)
            den = acc[:, HEAD_DIM:2 * HEAD_DIM] + jnp.exp2(sink - m_prev)
            y_ref[:, hs] = (acc[:, 0:HEAD_DIM] / den).astype(y_ref.dtype)
            s = _dot_nt(q_ref[:, hs], kall)
            s = jnp.concatenate([jnp.where(valid, s[:, 0:nloc], NEG_INF), s[:, nloc:]], axis=1)
            s_ref[hq] = s
            m = jnp.maximum(jnp.max(s, axis=-1, keepdims=True), sink)
            mb_ref[hq] = jnp.broadcast_to(m, (tq, LANES))


def _swa_latent(swa3, swa_ctx3, sink, layer):
    bsz, seq, _ = swa3.shape
    t_ctx = swa_ctx3.shape[1]
    w = SWA_WINDOW
    tq = 256 if seq % 256 == 0 else w
    r = tq // w
    nblk = seq // w
    nq = seq // tq
    n_blocks = bsz * nq
    kvw = SWA_KV_HEADS * HEAD_DIM
    qw = SWA_HEADS * HEAD_DIM
    k_col, v_col = qw // kvw, qw // kvw + 1
    nk = tq + 2 * w + t_ctx
    assert nk % LANES == 0

    def cur(t):
        t = jnp.minimum(t, n_blocks - 1)
        return t // nq, t % nq

    def prev(t):
        return cur(jnp.maximum(t - 1, 0))

    def at(fn, spec):
        def index_map(t):
            b, i = fn(t)
            return spec(b, i)
        return index_map

    before = lambda i: jnp.maximum(i * r - 1, 0)
    after = lambda i: jnp.minimum((i + 1) * r, nblk - 1)

    def kv_specs(fn, col):
        return [
            pl.BlockSpec((None, tq, kvw), at(fn, lambda b, i: (b, i, col))),
            pl.BlockSpec((None, w, kvw), at(fn, lambda b, i: (b, before(i), col))),
            pl.BlockSpec((None, w, kvw), at(fn, lambda b, i: (b, after(i), col))),
            pl.BlockSpec((None, t_ctx, kvw), at(fn, lambda b, i: (b, 0, col))),
        ]

    return pl.pallas_call(
        functools.partial(_swa_kernel, tq=tq, seq=seq, nq=nq),
        grid=(n_blocks + 1,),
        in_specs=[pl.BlockSpec((None, tq, qw), at(cur, lambda b, i: (b, i, 0)))]
        + kv_specs(cur, k_col) + kv_specs(prev, v_col)
        + [pl.BlockSpec((None, SWA_HEADS, SUBLANES, LANES), lambda t: (layer, 0, 0, 0))],
        out_specs=pl.BlockSpec((None, tq, qw), at(prev, lambda b, i: (b, i, 0))),
        out_shape=jax.ShapeDtypeStruct((bsz, seq, qw), BF16),
        scratch_shapes=[pltpu.VMEM((SWA_HEADS, tq, nk), F32), pltpu.VMEM((SWA_HEADS, tq, LANES), F32)],
        name="swa_latent",
        compiler_params=_cparams("arbitrary"),
    )(swa3, swa3, swa3, swa3, swa_ctx3, swa3, swa3, swa3, swa_ctx3, sink)


def _mla_proj_kernel(x_ref, qn_ref, kvn_ref, wq_ref, wk_ref, wv_ref, cos_ref, slo_ref, shi_ref,
                     q_ref, k_ref, v_ref):
    nq = MLA_ROPE_DIM // 4
    cos, slo, shi = cos_ref[...], slo_ref[...], shi_ref[...]
    cq = (_rms(x_ref[:, 0:MLA_Q_RANK]) * qn_ref[...]).astype(BF16)
    q = _dot(cq, wq_ref[...]) * (MLA_SCALE * LOG2E)
    ckv =(_rms(x_ref[:, MLA_Q_RANK:MLA_Q_RANK + MLA_KV_RANK]) * kvn_ref[...]).astype(BF16)
    kn = _dot(ckv, wk_ref[...])
    v_ref[...] = _dot(ckv, wv_ref[...]).astype(v_ref.dtype)
    kr = _rope(x_ref[:, MLA_Q_RANK + MLA_KV_RANK:MLA_COLS_PAD], cos, slo, shi, nq).astype(k_ref.dtype)
    for h in range(MLA_HEADS):
        a = h * MLA_QK_PAD
        q_ref[:, a:a + MLA_NOPE_DIM] = q[:, a:a + MLA_NOPE_DIM].astype(q_ref.dtype)
        q_ref[:, a + MLA_NOPE_DIM:a + MLA_QK_PAD] = _rope(
            q[:, a + MLA_NOPE_DIM:a + MLA_QK_PAD], cos, slo, shi, nq).astype(q_ref.dtype)
        k_ref[:, a:a + MLA_NOPE_DIM] = kn[:, h * MLA_NOPE_DIM:(h + 1) * MLA_NOPE_DIM].astype(k_ref.dtype)
        k_ref[:, a + MLA_NOPE_DIM:a + MLA_QK_PAD] = kr


def _mla_project(mla2, seq, qn, kvn, wq, wk, wv, tabs, layer, tm):
    m_rows = mla2.shape[0]
    cos, slo, shi = tabs
    tpb = seq // tm
    qkw = MLA_HEADS * MLA_QK_PAD
    vw = MLA_HEADS * MLA_V_DIM
    lay = lambda i: (layer, 0, 0)
    tab = pl.BlockSpec((tm, LANES), lambda i: (i % tpb, 0))
    return pl.pallas_call(
        _mla_proj_kernel,
        grid=(m_rows // tm,),
        in_specs=[
            pl.BlockSpec((tm, MLA_COLS_PAD), lambda i: (i, 0)),
            pl.BlockSpec((None, 1, MLA_Q_RANK), lay),
            pl.BlockSpec((None, 1, MLA_KV_RANK), lay),
            pl.BlockSpec((None, MLA_Q_RANK, qkw), lay),
            pl.BlockSpec((None, MLA_KV_RANK, vw), lay),
            pl.BlockSpec((None, MLA_KV_RANK, vw), lay),
            tab, tab, tab,
        ],
        out_specs=[
            pl.BlockSpec((tm, qkw), lambda i: (i, 0)),
            pl.BlockSpec((tm, qkw), lambda i: (i, 0)),
            pl.BlockSpec((tm, vw), lambda i: (i, 0)),
        ],
        out_shape=[
            jax.ShapeDtypeStruct((m_rows, qkw), BF16),
            jax.ShapeDtypeStruct((m_rows, qkw), BF16),
            jax.ShapeDtypeStruct((m_rows, vw), BF16),
        ],
        name="mla_project",
        compiler_params=_cparams("parallel"),
    )(mla2, qn, kvn, wq, wk, wv, cos, slo, shi)


def _attn_kernel(*refs, n_src, q_scale, has_sink, nq):
    q_ref = refs[0]
    k_refs = refs[1:1 + n_src]
    v_refs = refs[1 + n_src:1 + 2 * n_src]
    pos = 1 + 2 * n_src
    sink_ref = refs[pos] if has_sink else None
    sink_b_ref = refs[pos + 1] if has_sink else None
    o_ref = refs[pos + (2 if has_sink else 0)]
    scratch = refs[pos + (3 if has_sink else 1):]
    s_refs = scratch[0:n_src]
    va_refs = scratch[n_src:2 * n_src]
    mb_ref = scratch[2 * n_src]
    t = pl.program_id(0)
    tq = q_ref.shape[0]
    dv = o_ref.shape[-1]
    tks = [s_ref.shape[-1] for s_ref in s_refs]

    @pl.when(t == 0)
    def _():
        for s_ref in s_refs:
            s_ref[...] = jnp.zeros(s_ref.shape, F32)
        mb_ref[...] = jnp.zeros(mb_ref.shape, F32)

    @pl.when(jnp.maximum(t - 1, 0) % nq == 0)
    def _():
        for v_ref, va in zip(v_refs, va_refs):
            va[:, 0:dv] = v_ref[...].astype(BF16)
            va[:, dv:2 * dv] = jnp.ones((va.shape[0], dv), BF16)

    q = q_ref[...]
    if q_scale != 1.0:
        q = q.astype(F32) * q_scale
    q = q.astype(BF16)
    m_prev = mb_ref[...]

    def lane_max(m_vec, s):
        for c in range(s.shape[-1] // LANES):
            m_vec = jnp.maximum(m_vec, s[:, c * LANES:(c + 1) * LANES])
        return m_vec

    m_vec = jnp.full((tq, LANES), NEG_INF, F32)
    acc = jnp.zeros((tq, 2 * dv), F32)
    for k_ref, s_ref, va, tkk in zip(k_refs, s_refs, va_refs, tks):
        for j in range(s_ref.shape[0]):
            rows = pl.ds(j * tkk, tkk)
            p = jnp.exp2(s_ref[j] - jnp.concatenate([m_prev] * (tkk // LANES), axis=1))
            acc = acc + _dot(p.astype(BF16), va[rows, :])
            s = _dot_nt(q, k_ref[rows, :].astype(BF16))
            s_ref[j] = s
            m_vec = lane_max(m_vec, s)

    l = acc[:, dv:2 * dv]
    if has_sink:
        l = l + jnp.exp2(sink_b_ref[0:1, 0:1] * LOG2E - m_prev)
    o_ref[...] = (acc[:, 0:dv] / l).astype(o_ref.dtype)

    m_row = jnp.max(m_vec, axis=-1, keepdims=True)
    if has_sink:
        m_row = jnp.maximum(m_row, sink_ref[0:1, 0:1] * LOG2E)
    mb_ref[...] = jnp.broadcast_to(m_row, (tq, LANES))


def _attention(q3, ks, vs, *, n_heads, group, dq, dv, q_col0, k_col0, v_col0, q_scale, sink=None, layer=0,
               tq=1024, tk=512):
    bsz, lq, _ = q3.shape
    tq = min(tq, lq)
    assert dv == LANES
    n_src = len(ks)
    tks = [min(tk, k.shape[1]) for k in ks]
    nq = lq // tq
    n_blocks = bsz * n_heads * nq

    def cur(t):
        t = jnp.minimum(t, n_blocks - 1)
        return t // (n_heads * nq), (t // nq) % n_heads, t % nq

    def prev(t):
        return cur(jnp.maximum(t - 1, 0))

    def at(fn, spec):
        def index_map(t):
            b, h, i = fn(t)
            return spec(b, h, i)
        return index_map

    in_specs = [pl.BlockSpec((None, tq, dq), at(cur, lambda b, h, i: (b, i, q_col0 + h)))]
    in_specs += [pl.BlockSpec((None, k.shape[1], dq), at(cur, lambda b, h, i: (b, 0, k_col0 + h // group)))
                 for k in ks]
    in_specs += [pl.BlockSpec((None, v.shape[1], dv), at(prev, lambda b, h, i: (b, 0, v_col0 + h // group)))
                 for v in vs]
    args = [q3] + list(ks) + list(vs)
    if sink is not None:
        for fn in (cur, prev):
            in_specs.append(pl.BlockSpec((None, None, SUBLANES, LANES),
                                         at(fn, lambda b, h, i: (layer, h, 0, 0))))
            args.append(sink)
    scratch = [pltpu.VMEM((k.shape[1] // tkk, tq, tkk), F32) for k, tkk in zip(ks, tks)]
    scratch += [pltpu.VMEM((v.shape[1], 2 * dv), BF16) for v in vs]
    scratch += [pltpu.VMEM((tq, LANES), F32)]
    return pl.pallas_call(
        functools.partial(_attn_kernel, n_src=n_src, q_scale=q_scale, has_sink=sink is not None, nq=nq),
        grid=(n_blocks + 1,),
        in_specs=in_specs,
        out_specs=pl.BlockSpec((None, tq, dv), at(prev, lambda b, h, i: (b, i, h))),
        out_shape=jax.ShapeDtypeStruct((bsz, lq, n_heads * dv), BF16),
        scratch_shapes=scratch,
        name="attention",
        compiler_params=_cparams("arbitrary"),
    )(*args)


def _outproj_kernel(yr_ref, ys_ref, ym_ref, x_ref, mod_ref, w_ref, g_ref, b_ref, o_ref, *, alpha):
    a, b = RET_DIM, RET_DIM + SWA_HEADS * HEAD_DIM
    mix = (_dot(yr_ref[...], w_ref[0:a, :]) + _dot(ys_ref[...], w_ref[a:b, :])
           + _dot(ym_ref[...], w_ref[b:MIX_WIDTH, :]))
    z = alpha * x_ref[...] + (1.0 + mod_ref[2:3, :]) * mix
    o_ref[...] = _layer_norm(z, g_ref[...], b_ref[...])


def _outproj_ln(y_ret, y_swa, y_mla, x2, mods, w_o, ln_g, ln_b, layer, mod_row0, rows_per_mod, tm, alpha):
    m_rows, d = x2.shape
    mod_idx = lambda i: (layer * MOD_ROWS + mod_row0 + (i * tm) // rows_per_mod, 0, 0)
    lay = lambda i: (layer, 0, 0)
    row = lambda w: pl.BlockSpec((tm, w), lambda i: (i, 0))
    return pl.pallas_call(
        functools.partial(_outproj_kernel, alpha=alpha),
        grid=(m_rows // tm,),
        in_specs=[
            row(RET_DIM), row(SWA_HEADS * HEAD_DIM), row(MLA_HEADS * MLA_V_DIM), row(d),
            pl.BlockSpec((None, N_MOD, d), mod_idx),
            _resident((None, MIX_WIDTH, d), lay),
            pl.BlockSpec((None, 1, d), lay),
            pl.BlockSpec((None, 1, d), lay),
        ],
        out_specs=row(d),
        out_shape=jax.ShapeDtypeStruct((m_rows, d), F32),
        name="outproj_ln",
        compiler_params=_cparams("parallel"),
    )(y_ret, y_swa, y_mla, x2, mods, w_o, ln_g, ln_b)


def _ffn_kernel(x_hbm, xp_ref, xn_ref, mod_ref, wu_ref, wg_ref, cw_ref, cb_ref, wd_ref, g_ref, b_ref,
                o_ref, xbuf, h_ext, g_ext, sem, *, tm, seq, alpha):
    i = pl.program_id(0)
    f = pl.program_id(1)

    def x_copy(tile):
        rows = pl.ds(pl.multiple_of(tile * tm, tm), tm)
        return pltpu.make_async_copy(x_hbm.at[rows, :], xbuf, sem)

    @pl.when(f == 0)
    def _():
        @pl.when(i == 0)
        def _():
            x_copy(i).start()

        x_copy(i).wait()
        scale = 1.0 + mod_ref[4:5, :]
        shift = mod_ref[3:4, :]
        has_prev = jnp.where((i * tm) % seq != 0, 1.0, 0.0).astype(F32)
        has_next = jnp.where(((i + 1) * tm) % seq != 0, 1.0, 0.0).astype(F32)
        h_ext[0:HALO, :] = ((xp_ref[...] * scale + shift) * has_prev).astype(BF16)
        h_ext[HALO:HALO + tm, :] = (xbuf[...] * scale + shift).astype(BF16)
        h_ext[HALO + tm:tm + 2 * HALO, :] = ((xn_ref[...] * scale + shift) * has_next).astype(BF16)
        o_ref[...] = alpha * xbuf[...]

        @pl.when(i + 1 < pl.num_programs(0))
        def _():
            x_copy(i + 1).start()

    g_ext[...] = _dot(h_ext[...], wg_ref[...])
    u = _dot(h_ext[HALO:HALO + tm, :], wu_ref[...])
    g_prev = g_ext[HALO - 1:HALO - 1 + tm, :]
    g_next = g_ext[HALO + 1:HALO + 1 + tm, :]
    if tm > seq:
        r = lax.broadcasted_iota(jnp.int32, (tm, 1), 0) % seq
        g_prev = jnp.where(r != 0, g_prev, 0.0)
        g_next = jnp.where(r != seq - 1, g_next, 0.0)
    gc = (g_prev * cw_ref[0:1, :] + g_ext[HALO:HALO + tm, :] * cw_ref[1:2, :]
          + g_next * cw_ref[2:3, :] + cb_ref[...])
    a = (_silu(gc) * u).astype(BF16)
    o_ref[...] += (1.0 + mod_ref[5:6, :]) * _dot(a, wd_ref[...])

    @pl.when(f == pl.num_programs(1) - 1)
    def _():
        o_ref[...] = _layer_norm(o_ref[...], g_ref[...], b_ref[...])


def _ffn(x2, seq, mods, w_up, conv_w, conv_b, w_down, ln_g, ln_b, layer, mod_row0, rows_per_mod, tm, alpha):
    m_rows, d = x2.shape
    dff = w_down.shape[1]
    tf = 512 if dff % 512 == 0 else (256 if dff % 256 == 0 else LANES)
    nf = dff // tf
    hb = tm // HALO
    nhb = m_rows // HALO
    assert m_rows % tm == 0 and (tm % seq == 0 or seq % tm == 0)
    mod_idx = lambda i, f: (layer * MOD_ROWS + mod_row0 + (i * tm) // rows_per_mod, 0, 0)
    lay = lambda i, f: (layer, 0, 0)
    return pl.pallas_call(
        functools.partial(_ffn_kernel, tm=tm, seq=seq, alpha=alpha),
        grid=(m_rows // tm, nf),
        in_specs=[
            pl.BlockSpec(memory_space=pl.ANY),
            pl.BlockSpec((HALO, d), lambda i, f: (jnp.maximum(i * hb - 1, 0), 0)),
            pl.BlockSpec((HALO, d), lambda i, f: (jnp.minimum((i + 1) * hb, nhb - 1), 0)),
            pl.BlockSpec((None, N_MOD, d), mod_idx),
            pl.BlockSpec((None, d, tf), lambda i, f: (layer, 0, f)),
            pl.BlockSpec((None, d, tf), lambda i, f: (layer, 0, nf + f)),
            pl.BlockSpec((None, 3, tf), lambda i, f: (layer, 0, f)),
            pl.BlockSpec((None, 1, tf), lambda i, f: (layer, 0, f)),
            pl.BlockSpec((None, tf, d), lambda i, f: (layer, f, 0)),
            pl.BlockSpec((None, 1, d), lay),
            pl.BlockSpec((None, 1, d), lay),
        ],
        out_specs=pl.BlockSpec((tm, d), lambda i, f: (i, 0)),
        out_shape=jax.ShapeDtypeStruct((m_rows, d), F32),
        scratch_shapes=[
            pltpu.VMEM((tm, d), F32),
            pltpu.VMEM((tm + 2 * HALO, d), BF16),
            pltpu.VMEM((tm + 2 * HALO, tf), F32),
            pltpu.SemaphoreType.DMA(()),
        ],
        name="conv_ffn",
        compiler_params=_cparams("arbitrary", "arbitrary"),
    )(x2, x2, x2, mods, w_up, w_up, conv_w, conv_b, w_down, ln_g, ln_b)


def _rope_tables(n_tokens, dim):
    rows = n_tokens // GRID_W
    r = np.repeat(np.arange(rows, dtype=np.float32), GRID_W)
    cc = np.tile(np.arange(GRID_W, dtype=np.float32), rows)
    n_freq = dim // 4
    inv = jnp.asarray(ROPE_THETA, F32) ** (-jnp.arange(n_freq, dtype=F32) / n_freq)
    ang_r = jnp.asarray(r)[:, None] * inv
    ang_c = jnp.asarray(cc)[:, None] * inv
    ang = jnp.concatenate([ang_r, ang_r, ang_c, ang_c], axis=-1)
    cos, sin = jnp.cos(ang), jnp.sin(ang)
    lane = np.arange(dim)
    lo = jnp.asarray((lane % (2 * n_freq)) < n_freq)
    sin_lo = jnp.where(lo, -sin, 0.0)
    sin_hi = jnp.where(lo, 0.0, sin)
    pad = LANES - dim
    if pad:
        cos = jnp.pad(cos, ((0, 0), (0, pad)), constant_values=1.0)
        sin_lo = jnp.pad(sin_lo, ((0, 0), (0, pad)))
        sin_hi = jnp.pad(sin_hi, ((0, 0), (0, pad)))
    return cos, sin_lo, sin_hi


def _identity_tables(n_tokens):
    return (jnp.ones((n_tokens, LANES), F32), jnp.zeros((n_tokens, LANES), F32),
            jnp.zeros((n_tokens, LANES), F32))


def _lane_bcast(p):
    return jnp.broadcast_to(p.astype(F32)[..., None, None], p.shape + (SUBLANES, LANES))


def kernel(x, c, ctx, c_ctx, ada_w, ada_b, w_in, ret_decay_fwd, ret_decay_bwd, swa_sink, mla_q_norm, mla_w_uq,
           mla_kv_norm, mla_w_ukv, w_o, ln1_g, ln1_b, ffn_w_up, ffn_conv_w, ffn_conv_b, ffn_w_down, ln2_g, ln2_b):
    bsz, seq, d = x.shape
    t = ctx.shape[1]
    depth = w_in.shape[0]
    assert bsz + 1 <= MOD_ROWS and seq % RET_CHUNK == 0 and t % RET_CHUNK == 0
    assert seq % GRID_W == 0 and d % LANES == 0
    alpha = (2 * depth) ** 0.25

    w_in_b = jnp.pad(w_in.astype(BF16), ((0, 0), (0, 0), (0, IN_WIDTH_PAD - IN_WIDTH)))
    w_o_b = w_o.astype(BF16)
    w_up_b = ffn_w_up.astype(BF16)
    w_down_b = ffn_w_down.astype(BF16)
    hq = MLA_NOPE_DIM + MLA_ROPE_DIM
    wq = mla_w_uq.reshape(depth, MLA_Q_RANK, MLA_HEADS, hq)
    wq = jnp.pad(wq, ((0, 0), (0, 0), (0, 0), (0, MLA_QK_PAD - hq)))
    wq = wq.reshape(depth, MLA_Q_RANK, MLA_HEADS * MLA_QK_PAD).astype(BF16)
    wkv = mla_w_ukv.reshape(depth, MLA_KV_RANK, MLA_HEADS, MLA_NOPE_DIM + MLA_V_DIM)
    wk = wkv[..., :MLA_NOPE_DIM].reshape(depth, MLA_KV_RANK, MLA_HEADS * MLA_NOPE_DIM).astype(BF16)
    wv = wkv[..., MLA_NOPE_DIM:].reshape(depth, MLA_KV_RANK, MLA_HEADS * MLA_V_DIM).astype(BF16)
    qn = mla_q_norm.reshape(depth, 1, MLA_Q_RANK)
    kvn = mla_kv_norm.reshape(depth, 1, MLA_KV_RANK)
    g1, b1 = ln1_g.reshape(depth, 1, d), ln1_b.reshape(depth, 1, d)
    g2, b2 = ln2_g.reshape(depth, 1, d), ln2_b.reshape(depth, 1, d)
    conv_b = ffn_conv_b.reshape(depth, 1, -1)
    dec = _lane_bcast(jnp.stack([ret_decay_fwd, ret_decay_bwd], axis=1))
    sink = _lane_bcast(swa_sink)

    tabs_h = _rope_tables(seq, HEAD_DIM)
    tabs_m = _rope_tables(seq, MLA_ROPE_DIM)
    tabs_id = _identity_tables(t)

    cond = jnp.zeros((MOD_ROWS, d), F32).at[:bsz].set(c).at[bsz].set(c_ctx)
    mods = _ada_mod(cond, ada_w, ada_b).reshape(depth * MOD_ROWS, N_MOD, d)

    tm_x = 512 if seq % 512 == 0 else RET_CHUNK
    tm_f = 1024 if seq % 1024 == 0 else tm_x
    tm_p = 1024 if seq % 1024 == 0 else tm_x
    tm_c = t if t <= 512 else RET_CHUNK
    seqs = max([j for j in range(1, bsz + 1) if bsz % j == 0 and t * j <= 1024], default=0)
    tm_cf = t * seqs if seqs else tm_c
    zeros_state = jnp.zeros((bsz, RET_HEADS, HEAD_DIM, HEAD_DIM), F32)

    x2 = x.reshape(bsz * seq, d)
    xc2 = ctx.reshape(bsz * t, d)
    for l in range(depth):
        last = l == depth - 1
        ret_c, swa_c, mla_c = _inproj(xc2, t, mods, w_in_b, tabs_id, l, bsz, bsz * t, tm_c)
        swa_c3 = swa_c.reshape(bsz, t, SWA_COLS)
        y_ret_c, st_f, st_b = _retention(ret_c.reshape(-1, bsz, t, HEAD_DIM), dec, l, zeros_state, zeros_state)
        q_c, k_c, v_c = _mla_project(mla_c, t, qn, kvn, wq, wk, wv, tabs_id, l, tm_c)
        k_c3 = k_c.reshape(bsz, t, -1)
        v_c3 = v_c.reshape(bsz, t, -1)

        ret_x, swa_x, mla_x = _inproj(x2, seq, mods, w_in_b, tabs_h, l, 0, seq, tm_x)
        y_ret, _, _ = _retention(ret_x.reshape(-1, bsz, seq, HEAD_DIM), dec, l, st_f, st_b)
        y_swa = _swa_latent(swa_x.reshape(bsz, seq, SWA_COLS), swa_c3, sink, l)
        q_x, k_x, v_x = _mla_project(mla_x, seq, qn, kvn, wq, wk, wv, tabs_m, l, tm_p)
        y_mla = _attention(q_x.reshape(bsz, seq, -1), [k_x.reshape(bsz, seq, -1), k_c3],
                           [v_x.reshape(bsz, seq, -1), v_c3], n_heads=MLA_HEADS, group=1,
                           dq=MLA_QK_PAD, dv=MLA_V_DIM, q_col0=0, k_col0=0, v_col0=0, q_scale=1.0)
        x_a = _outproj_ln(y_ret.reshape(bsz * seq, -1), y_swa.reshape(bsz * seq, -1),
                          y_mla.reshape(bsz * seq, -1), x2, mods, w_o_b, g1, b1, l, 0, seq, tm_x, alpha)
        x_new = _ffn(x_a, seq, mods, w_up_b, ffn_conv_w, conv_b, w_down_b, g2, b2, l, 0, seq, tm_f, alpha)

        if not last:
            y_swa_c = _attention(swa_c3, [swa_c3], [swa_c3], n_heads=SWA_HEADS, group=SWA_GROUP,
                                 dq=HEAD_DIM, dv=HEAD_DIM, q_col0=0, k_col0=SWA_HEADS,
                                 v_col0=SWA_HEADS + SWA_KV_HEADS, q_scale=1.0, sink=sink, layer=l)
            y_mla_c = _attention(q_c.reshape(bsz, t, -1), [k_c3], [v_c3], n_heads=MLA_HEADS, group=1,
                                 dq=MLA_QK_PAD, dv=MLA_V_DIM, q_col0=0, k_col0=0, v_col0=0, q_scale=1.0)
            xc_a = _outproj_ln(y_ret_c.reshape(bsz * t, -1), y_swa_c.reshape(bsz * t, -1),
                               y_mla_c.reshape(bsz * t, -1), xc2, mods, w_o_b, g1, b1, l, bsz, bsz * t,
                               tm_c, alpha)
            xc2 = _ffn(xc_a, t, mods, w_up_b, ffn_conv_w, conv_b, w_down_b, g2, b2, l, bsz, bsz * t,
                       tm_cf, alpha)
        x2 = x_new
    return x2.reshape(bsz, seq, d)
```

```python
import functools

import jax
import jax.numpy as jnp
import numpy as np
from jax import lax
from jax.experimental import pallas as pl
from jax.experimental.pallas import tpu as pltpu

GRID_W = 64
HEAD_DIM = 128
ROPE_THETA = 10000.0
RET_HEADS = 4
RET_DIM = RET_HEADS * HEAD_DIM
RET_CHUNK = 128
SWA_HEADS = 6
SWA_KV_HEADS = 2
SWA_GROUP = SWA_HEADS // SWA_KV_HEADS
SWA_WINDOW = 128
MLA_HEADS = 6
MLA_Q_RANK = 512
MLA_KV_RANK = 256
MLA_NOPE_DIM = 128
MLA_ROPE_DIM = 64
MLA_V_DIM = 128
MLA_SCALE = (MLA_NOPE_DIM + MLA_ROPE_DIM) ** -0.5
MLA_QK_PAD = 256
N_MOD = 6
LN_EPS = 1e-5
RMS_EPS = 1e-6
NEG_INF = -1e30
LOG2E = 1.4426950408889634

RET_COLS = 4 * RET_DIM
SWA_COLS = (SWA_HEADS + 2 * SWA_KV_HEADS) * HEAD_DIM
MLA_COLS = MLA_Q_RANK + MLA_KV_RANK + MLA_ROPE_DIM
MLA_COLS_PAD = 896
IN_WIDTH = RET_COLS + SWA_COLS + MLA_COLS
IN_WIDTH_PAD = RET_COLS + SWA_COLS + MLA_COLS_PAD
MIX_WIDTH = RET_DIM + SWA_HEADS * HEAD_DIM + MLA_HEADS * MLA_V_DIM

LANES = 128
SUBLANES = 8
MOD_ROWS = 8
VMEM_LIMIT = 56 * 1024 * 1024
HALO = 16

BF16 = jnp.bfloat16
F32 = jnp.float32


def _cparams(*sem):
    return pltpu.CompilerParams(dimension_semantics=sem, vmem_limit_bytes=VMEM_LIMIT)


def _resident(block, index_map):
    return pl.BlockSpec(block, index_map, pipeline_mode=pl.Buffered(1))


def _dot(a, b):
    return jnp.dot(a, b, preferred_element_type=F32)


def _dot_nt(a, b):
    return lax.dot_general(a, b, (((1,), (1,)), ((), ())), preferred_element_type=F32)


def _silu(x):
    return x * (1.0 / (1.0 + jnp.exp(-x)))


def _rope(x, cos, sin_lo, sin_hi, nq):
    w = x.shape[-1]
    return x * cos + pltpu.roll(x, w - nq, 1) * sin_lo + pltpu.roll(x, nq, 1) * sin_hi


def _layer_norm(z, g, b):
    mu = jnp.mean(z, axis=-1, keepdims=True)
    zc = z - mu
    var = jnp.mean(zc * zc, axis=-1, keepdims=True)
    return zc * lax.rsqrt(var + LN_EPS) * g + b


def _rms(x):
    return x * lax.rsqrt(jnp.mean(x * x, axis=-1, keepdims=True) + RMS_EPS)


def _log_sigmoid(x):
    return -(jnp.maximum(-x, 0.0) + jnp.log(1.0 + jnp.exp(-jnp.abs(x))))


def _ada_kernel(c_ref, w_ref, b_ref, o_ref):
    sc = _silu(c_ref[...]).astype(BF16)
    o_ref[...] = _dot(sc, w_ref[...].astype(BF16)) + b_ref[...]


def _ada_mod(cond, ada_w, ada_b):
    depth, d, n = ada_w.shape
    tn = next(c for c in (2048, 1024, 512, 256, LANES) if n % c == 0)
    return pl.pallas_call(
        _ada_kernel,
        grid=(depth, n // tn),
        in_specs=[
            pl.BlockSpec((MOD_ROWS, d), lambda l, j: (0, 0)),
            pl.BlockSpec((None, d, tn), lambda l, j: (l, 0, j)),
            pl.BlockSpec((None, 1, tn), lambda l, j: (l, 0, j)),
        ],
        out_specs=pl.BlockSpec((None, MOD_ROWS, tn), lambda l, j: (l, 0, j)),
        out_shape=jax.ShapeDtypeStruct((depth, MOD_ROWS, n), F32),
        name="ada_mod",
        compiler_params=_cparams("parallel", "parallel"),
    )(cond, ada_w, ada_b.reshape(depth, 1, n))


def _inproj_kernel(x_ref, mod_ref, w_ref, cos_ref, slo_ref, shi_ref, ret_ref, swa_ref, mla_ref):
    m = mod_ref[...]
    h = (x_ref[...] * (1.0 + m[1:2, :]) + m[0:1, :]).astype(BF16)
    cos, slo, shi = cos_ref[...], slo_ref[...], shi_ref[...]
    nq = HEAD_DIM // 4
    k_scale = HEAD_DIM ** -0.5

    def rope(t, c, scale):
        r = _rope(t[:, c:c + HEAD_DIM], cos, slo, shi, nq)
        return r if scale is None else r * scale

    def rope_heads(t, out_ref, col0, n_heads, scale):
        for hh in range(n_heads):
            c = col0 + hh * HEAD_DIM
            out_ref[:, c:c + HEAD_DIM] = rope(t, c, scale).astype(out_ref.dtype)

    ret = _dot(h, w_ref[:, 0:RET_COLS])
    for j in range(RET_COLS // HEAD_DIM):
        c = j * HEAD_DIM
        if j < RET_HEADS:
            ret_ref[j] = rope(ret, c, None)
        elif j < 2 * RET_HEADS:
            ret_ref[j] = rope(ret, c, k_scale)
        else:
            ret_ref[j] = ret[:, c:c + HEAD_DIM]
    swa = _dot(h, w_ref[:, RET_COLS:RET_COLS + SWA_COLS])
    sq_cols = SWA_HEADS * HEAD_DIM
    sk_cols = SWA_KV_HEADS * HEAD_DIM
    rope_heads(swa, swa_ref, 0, SWA_HEADS, k_scale * LOG2E)
    rope_heads(swa, swa_ref, sq_cols, SWA_KV_HEADS, None)
    swa_ref[:, sq_cols + sk_cols:SWA_COLS] = swa[:, sq_cols + sk_cols:SWA_COLS].astype(swa_ref.dtype)
    mla_ref[...] = _dot(h, w_ref[:, RET_COLS + SWA_COLS:IN_WIDTH_PAD])


def _inproj(x2, seq, mods, w_in, tabs, layer, mod_row0, rows_per_mod, tm):
    m_rows, d = x2.shape
    mod_idx = lambda i: (layer * MOD_ROWS + mod_row0 + (i * tm) // rows_per_mod, 0, 0)
    tpb = seq // tm
    tab = pl.BlockSpec((tm, LANES), lambda i: (i % tpb, 0))
    return pl.pallas_call(
        _inproj_kernel,
        grid=(m_rows // tm,),
        in_specs=[
            pl.BlockSpec((tm, d), lambda i: (i, 0)),
            pl.BlockSpec((None, N_MOD, d), mod_idx),
            _resident((None, d, IN_WIDTH_PAD), lambda i: (layer, 0, 0)),
            tab, tab, tab,
        ],
        out_specs=[
            pl.BlockSpec((RET_COLS // HEAD_DIM, tm, HEAD_DIM), lambda i: (0, i, 0)),
            pl.BlockSpec((tm, SWA_COLS), lambda i: (i, 0)),
            pl.BlockSpec((tm, MLA_COLS_PAD), lambda i: (i, 0)),
        ],
        out_shape=[
            jax.ShapeDtypeStruct((RET_COLS // HEAD_DIM, m_rows, HEAD_DIM), F32),
            jax.ShapeDtypeStruct((m_rows, SWA_COLS), BF16),
            jax.ShapeDtypeStruct((m_rows, MLA_COLS_PAD), F32),
        ],
        name="inproj",
        compiler_params=_cparams("parallel"),
    )(x2, mods, w_in, *tabs)


def _decay_terms(dec_ref):
    lg = _log_sigmoid(dec_ref[...])
    return lg[0, 0:1, 0:1], lg[1, 0:1, 0:1]


def _ret_bwd_kernel(k_ref, v_ref, dec_ref, s0_ref, sin_ref, sfin_ref, state, *, cpb):
    n = pl.program_id(2)
    c = RET_CHUNK

    @pl.when(n == 0)
    def _():
        state[...] = s0_ref[...]

    _, lg_b = _decay_terms(dec_ref)
    pos = lax.broadcasted_iota(jnp.int32, (c, 1), 0).astype(F32)
    kdec = jnp.exp(lg_b * pos)
    cdec = jnp.exp(lg_b * float(c))
    kv = []
    for ci in range(cpb):
        rows = pl.ds(ci * c, c)
        kv.append(_dot((k_ref[rows, :] * kdec).T.astype(BF16), v_ref[rows, :].astype(BF16)))
    st = state[...]
    for ci in reversed(range(cpb)):
        sin_ref[ci] = st.astype(sin_ref.dtype)
        st = cdec * st + kv[ci]
    state[...] = st

    @pl.when(n == pl.num_programs(2) - 1)
    def _():
        sfin_ref[...] = st


def _ret_fwd_kernel(q_ref, k_ref, v_ref, g_ref, dec_ref, s0_ref, sb_ref, y_ref, sfin_ref, state, *, cpb):
    n = pl.program_id(2)
    c = RET_CHUNK

    @pl.when(n == 0)
    def _():
        state[...] = s0_ref[...]

    lg_f, lg_b = _decay_terms(dec_ref)
    pos = lax.broadcasted_iota(jnp.int32, (c, 1), 0).astype(F32)
    ri = lax.broadcasted_iota(jnp.int32, (c, c), 0)
    cj = lax.broadcasted_iota(jnp.int32, (c, c), 1)
    diff = (ri - cj).astype(F32)
    intra = (jnp.where(diff >= 0, jnp.exp(lg_f * jnp.maximum(diff, 0.0)), 0.0)
             + jnp.where(diff <= 0, jnp.exp(lg_b * jnp.maximum(-diff, 0.0)), 0.0))
    qdec_f = jnp.exp(lg_f * (pos + 1.0))
    qdec_b = jnp.exp(lg_b * (float(c) - pos))
    kdec_f = jnp.exp(lg_f * (float(c) - 1.0 - pos))
    cdec_f = jnp.exp(lg_f * float(c))
    lhs, vs, kv = [], [], []
    for ci in range(cpb):
        rows = pl.ds(ci * c, c)
        q, k = q_ref[rows, :], k_ref[rows, :]
        v = v_ref[rows, :].astype(BF16)
        scores = _dot_nt(q.astype(BF16), k.astype(BF16)) * intra
        lhs.append(jnp.concatenate([scores.astype(BF16), (q * qdec_f).astype(BF16),
                                    (q * qdec_b).astype(BF16)], axis=1))
        vs.append(v)
        kv.append(_dot((k * kdec_f).T.astype(BF16), v))
    st = state[...]
    for ci in range(cpb):
        rows = pl.ds(ci * c, c)
        rhs = jnp.concatenate([vs[ci], st.astype(BF16), sb_ref[ci].astype(BF16)], axis=0)
        out = _dot(lhs[ci], rhs)
        y_ref[rows, :] = (_silu(g_ref[rows, :]) * _rms(out)).astype(y_ref.dtype)
        st = cdec_f * st + kv[ci]
    state[...] = st

    @pl.when(n == pl.num_programs(2) - 1)
    def _():
        sfin_ref[...] = st


def _retention(ret3, dec, layer, s0_f, s0_b):
    _, bsz, seq, _ = ret3.shape
    c = RET_CHUNK
    cpb = next(n for n in (32, 16, 8, 4, 2, 1) if seq % (n * c) == 0)
    rows = cpb * c
    nb = seq // rows
    h = RET_HEADS
    grid = (bsz, h, nb)
    state_spec = pl.BlockSpec((None, None, HEAD_DIM, HEAD_DIM), lambda b, hh, n: (b, hh, 0, 0))
    dec_spec = pl.BlockSpec((None, 2, None, SUBLANES, LANES), lambda b, hh, n: (layer, 0, hh, 0, 0))

    def col(j, rev):
        if rev:
            return pl.BlockSpec((None, None, rows, HEAD_DIM), lambda b, hh, n: (j * h + hh, b, nb - 1 - n, 0))
        return pl.BlockSpec((None, None, rows, HEAD_DIM), lambda b, hh, n: (j * h + hh, b, n, 0))

    sb_in, st_b = pl.pallas_call(
        functools.partial(_ret_bwd_kernel, cpb=cpb),
        grid=grid,
        in_specs=[col(1, True), col(2, True), dec_spec, state_spec],
        out_specs=[
            pl.BlockSpec((None, None, cpb, HEAD_DIM, HEAD_DIM), lambda b, hh, n: (b, hh, nb - 1 - n, 0, 0)),
            state_spec,
        ],
        out_shape=[
            jax.ShapeDtypeStruct((bsz, h, seq // c, HEAD_DIM, HEAD_DIM), BF16),
            jax.ShapeDtypeStruct((bsz, h, HEAD_DIM, HEAD_DIM), F32),
        ],
        scratch_shapes=[pltpu.VMEM((HEAD_DIM, HEAD_DIM), F32)],
        name="ret_bwd",
        compiler_params=_cparams("parallel", "parallel", "arbitrary"),
    )(ret3, ret3, dec, s0_b)

    y, st_f = pl.pallas_call(
        functools.partial(_ret_fwd_kernel, cpb=cpb),
        grid=grid,
        in_specs=[col(0, False), col(1, False), col(2, False), col(3, False), dec_spec, state_spec,
                  pl.BlockSpec((None, None, cpb, HEAD_DIM, HEAD_DIM), lambda b, hh, n: (b, hh, n, 0, 0))],
        out_specs=[
            pl.BlockSpec((None, rows, HEAD_DIM), lambda b, hh, n: (b, n, hh)),
            state_spec,
        ],
        out_shape=[
            jax.ShapeDtypeStruct((bsz, seq, RET_DIM), BF16),
            jax.ShapeDtypeStruct((bsz, h, HEAD_DIM, HEAD_DIM), F32),
        ],
        scratch_shapes=[pltpu.VMEM((HEAD_DIM, HEAD_DIM), F32)],
        name="ret_fwd",
        compiler_params=_cparams("parallel", "parallel", "arbitrary"),
    )(ret3, ret3, ret3, ret3, dec, s0_f, sb_in)
    return y, st_f, st_b


def _swa_kernel(q_ref, k_ref, kp_ref, kn_ref, kc_ref, v_ref, vp_ref, vn_ref, vc_ref, sink_ref, y_ref,
                s_ref, mb_ref, *, tq, seq, nq):
    t = pl.program_id(0)
    i = jnp.minimum(t, pl.num_programs(0) - 2) % nq
    w = SWA_WINDOW
    nloc = tq + 2 * w
    nk = nloc + kc_ref.shape[0]

    @pl.when(t == 0)
    def _():
        s_ref[...] = jnp.zeros(s_ref.shape, F32)
        mb_ref[...] = jnp.zeros(mb_ref.shape, F32)

    q_pos = i * tq + lax.broadcasted_iota(jnp.int32, (tq, nloc), 0)
    k_pos = i * tq - w + lax.broadcasted_iota(jnp.int32, (tq, nloc), 1)
    valid = (jnp.abs(k_pos - q_pos) <= w) & (k_pos >= 0) & (k_pos < seq)
    ones = jnp.ones((nk, HEAD_DIM), BF16)
    for kv in range(SWA_KV_HEADS):
        ks = pl.ds(kv * HEAD_DIM, HEAD_DIM)
        kall = jnp.concatenate([kp_ref[:, ks], k_ref[:, ks], kn_ref[:, ks], kc_ref[:, ks]], axis=0)
        vall = jnp.concatenate(
            [jnp.concatenate([vp_ref[:, ks], v_ref[:, ks], vn_ref[:, ks], vc_ref[:, ks]], axis=0), ones], axis=1)
        for g in range(SWA_GROUP):
            hq = kv * SWA_GROUP + g
            hs = pl.ds(hq * HEAD_DIM, HEAD_DIM)
            sink = sink_ref[hq, 0:1, 0:1] * LOG2E
            m_prev = mb_ref[hq]
            p = jnp.exp2(s_ref[hq] - jnp.concatenate([m_prev] * (nk // LANES), axis=1))
            acc = _dot(p.astype(BF16), vall)
            den = acc[:, HEAD_DIM:2 * HEAD_DIM] + jnp.exp2(sink - m_prev)
            y_ref[:, hs] = (acc[:, 0:HEAD_DIM] / den).astype(y_ref.dtype)
            s = _dot_nt(q_ref[:, hs], kall)
            s = jnp.concatenate([jnp.where(valid, s[:, 0:nloc], NEG_INF), s[:, nloc:]], axis=1)
            s_ref[hq] = s
            m = jnp.maximum(jnp.max(s, axis=-1, keepdims=True), sink)
            mb_ref[hq] = jnp.broadcast_to(m, (tq, LANES))


def _swa_latent(swa3, swa_ctx3, sink, layer):
    bsz, seq, _ = swa3.shape
    t_ctx = swa_ctx3.shape[1]
    w = SWA_WINDOW
    tq = 256 if seq % 256 == 0 else w
    r = tq // w
    nblk = seq // w
    nq = seq // tq
    n_blocks = bsz * nq
    kvw = SWA_KV_HEADS * HEAD_DIM
    qw = SWA_HEADS * HEAD_DIM
    k_col, v_col = qw // kvw, qw // kvw + 1
    nk = tq + 2 * w + t_ctx
    assert nk % LANES == 0

    def cur(t):
        t = jnp.minimum(t, n_blocks - 1)
        return t // nq, t % nq

    def prev(t):
        return cur(jnp.maximum(t - 1, 0))

    def at(fn, spec):
        def index_map(t):
            b, i = fn(t)
            return spec(b, i)
        return index_map

    before = lambda i: jnp.maximum(i * r - 1, 0)
    after = lambda i: jnp.minimum((i + 1) * r, nblk - 1)

    def kv_specs(fn, col):
        return [
            pl.BlockSpec((None, tq, kvw), at(fn, lambda b, i: (b, i, col))),
            pl.BlockSpec((None, w, kvw), at(fn, lambda b, i: (b, before(i), col))),
            pl.BlockSpec((None, w, kvw), at(fn, lambda b, i: (b, after(i), col))),
            pl.BlockSpec((None, t_ctx, kvw), at(fn, lambda b, i: (b, 0, col))),
        ]

    return pl.pallas_call(
        functools.partial(_swa_kernel, tq=tq, seq=seq, nq=nq),
        grid=(n_blocks + 1,),
        in_specs=[pl.BlockSpec((None, tq, qw), at(cur, lambda b, i: (b, i, 0)))]
        + kv_specs(cur, k_col) + kv_specs(prev, v_col)
        + [pl.BlockSpec((None, SWA_HEADS, SUBLANES, LANES), lambda t: (layer, 0, 0, 0))],
        out_specs=pl.BlockSpec((None, tq, qw), at(prev, lambda b, i: (b, i, 0))),
        out_shape=jax.ShapeDtypeStruct((bsz, seq, qw), BF16),
        scratch_shapes=[pltpu.VMEM((SWA_HEADS, tq, nk), F32), pltpu.VMEM((SWA_HEADS, tq, LANES), F32)],
        name="swa_latent",
        compiler_params=_cparams("arbitrary"),
    )(swa3, swa3, swa3, swa3, swa_ctx3, swa3, swa3, swa3, swa_ctx3, sink)


def _mla_proj_kernel(x_ref, qn_ref, kvn_ref, wq_ref, wk_ref, wv_ref, cos_ref, slo_ref, shi_ref,
                     q_ref, k_ref, v_ref, qs, kns, vs, krs):
    t = pl.program_id(0)
    nq = MLA_ROPE_DIM // 4

    @pl.when(t == 0)
    def _():
        qs[...] = jnp.zeros(qs.shape, F32)
        kns[...] = jnp.zeros(kns.shape, F32)
        vs[...] = jnp.zeros(vs.shape, vs.dtype)
        krs[...] = jnp.zeros(krs.shape, F32)

    cos, slo, shi = cos_ref[...], slo_ref[...], shi_ref[...]
    v_ref[...] = vs[...]
    kr = _rope(krs[...], cos, slo, shi, nq).astype(k_ref.dtype)
    for h in range(MLA_HEADS):
        a = h * MLA_QK_PAD
        q_ref[:, a:a + MLA_NOPE_DIM] = qs[:, a:a + MLA_NOPE_DIM].astype(q_ref.dtype)
        q_ref[:, a + MLA_NOPE_DIM:a + MLA_QK_PAD] = _rope(
            qs[:, a + MLA_NOPE_DIM:a + MLA_QK_PAD], cos, slo, shi, nq).astype(q_ref.dtype)
        k_ref[:, a:a + MLA_NOPE_DIM] = kns[:, h * MLA_NOPE_DIM:(h + 1) * MLA_NOPE_DIM].astype(k_ref.dtype)
        k_ref[:, a + MLA_NOPE_DIM:a + MLA_QK_PAD] = kr

    cq = (_rms(x_ref[:, 0:MLA_Q_RANK]) * qn_ref[...]).astype(BF16)
    qs[...] = _dot(cq, wq_ref[...]) * (MLA_SCALE * LOG2E)
    ckv = (_rms(x_ref[:, MLA_Q_RANK:MLA_Q_RANK + MLA_KV_RANK]) * kvn_ref[...]).astype(BF16)
    kns[...] = _dot(ckv, wk_ref[...])
    vs[...] = _dot(ckv, wv_ref[...]).astype(vs.dtype)
    krs[...] = x_ref[:, MLA_Q_RANK + MLA_KV_RANK:MLA_COLS_PAD]


def _mla_project(mla2, seq, qn, kvn, wq, wk, wv, tabs, layer, tm):
    m_rows = mla2.shape[0]
    cos, slo, shi = tabs
    tpb = seq // tm
    qkw = MLA_HEADS * MLA_QK_PAD
    vw = MLA_HEADS * MLA_V_DIM
    lay = lambda i: (layer, 0, 0)
    n_tiles = m_rows // tm
    prev = lambda t: jnp.maximum(t - 1, 0)
    tab = pl.BlockSpec((tm, LANES), lambda t: (prev(t) % tpb, 0))
    out = lambda w_: pl.BlockSpec((tm, w_), lambda t: (prev(t), 0))
    return pl.pallas_call(
        _mla_proj_kernel,
        grid=(n_tiles + 1,),
        in_specs=[
            pl.BlockSpec((tm, MLA_COLS_PAD), lambda t: (jnp.minimum(t, n_tiles - 1), 0)),
            pl.BlockSpec((None, 1, MLA_Q_RANK), lay),
            pl.BlockSpec((None, 1, MLA_KV_RANK), lay),
            pl.BlockSpec((None, MLA_Q_RANK, qkw), lay),
            pl.BlockSpec((None, MLA_KV_RANK, vw), lay),
            pl.BlockSpec((None, MLA_KV_RANK, vw), lay),
            tab, tab, tab,
        ],
        out_specs=[out(qkw), out(qkw), out(vw)],
        out_shape=[
            jax.ShapeDtypeStruct((m_rows, qkw), BF16),
            jax.ShapeDtypeStruct((m_rows, qkw), BF16),
            jax.ShapeDtypeStruct((m_rows, vw), BF16),
        ],
        scratch_shapes=[
            pltpu.VMEM((tm, qkw), F32),
            pltpu.VMEM((tm, MLA_HEADS * MLA_NOPE_DIM), F32),
            pltpu.VMEM((tm, vw), BF16),
            pltpu.VMEM((tm, LANES), F32),
        ],
        name="mla_project",
        compiler_params=_cparams("arbitrary"),
    )(mla2, qn, kvn, wq, wk, wv, cos, slo, shi)


def _attn_kernel(*refs, n_src, q_scale, has_sink, nq):
    q_ref = refs[0]
    k_refs = refs[1:1 + n_src]
    v_refs = refs[1 + n_src:1 + 2 * n_src]
    pos = 1 + 2 * n_src
    sink_ref = refs[pos] if has_sink else None
    sink_b_ref = refs[pos + 1] if has_sink else None
    o_ref = refs[pos + (2 if has_sink else 0)]
    scratch = refs[pos + (3 if has_sink else 1):]
    s_refs = scratch[0:n_src]
    va_refs = scratch[n_src:2 * n_src]
    mb_ref = scratch[2 * n_src]
    t = pl.program_id(0)
    tq = q_ref.shape[0]
    dv = o_ref.shape[-1]
    tks = [s_ref.shape[-1] for s_ref in s_refs]

    @pl.when(t == 0)
    def _():
        for s_ref in s_refs:
            s_ref[...] = jnp.zeros(s_ref.shape, F32)
        mb_ref[...] = jnp.zeros(mb_ref.shape, F32)

    @pl.when(jnp.maximum(t - 1, 0) % nq == 0)
    def _():
        for v_ref, va in zip(v_refs, va_refs):
            va[:, 0:dv] = v_ref[...].astype(BF16)
            va[:, dv:2 * dv] = jnp.ones((va.shape[0], dv), BF16)

    q = q_ref[...]
    if q_scale != 1.0:
        q = q.astype(F32) * q_scale
    q = q.astype(BF16)
    m_prev = mb_ref[...]

    def lane_max(m_vec, s):
        for c in range(s.shape[-1] // LANES):
            m_vec = jnp.maximum(m_vec, s[:, c * LANES:(c + 1) * LANES])
        return m_vec

    m_vec = jnp.full((tq, LANES), NEG_INF, F32)
    acc = jnp.zeros((tq, 2 * dv), F32)
    for k_ref, s_ref, va, tkk in zip(k_refs, s_refs, va_refs, tks):
        for j in range(s_ref.shape[0]):
            rows = pl.ds(j * tkk, tkk)
            p = jnp.exp2(s_ref[j] - jnp.concatenate([m_prev] * (tkk // LANES), axis=1))
            acc = acc + _dot(p.astype(BF16), va[rows, :])
            s = _dot_nt(q, k_ref[rows, :].astype(BF16))
            s_ref[j] = s
            m_vec = lane_max(m_vec, s)

    l = acc[:, dv:2 * dv]
    if has_sink:
        l = l + jnp.exp2(sink_b_ref[0:1, 0:1] * LOG2E - m_prev)
    o_ref[...] = (acc[:, 0:dv] / l).astype(o_ref.dtype)

    m_row = jnp.max(m_vec, axis=-1, keepdims=True)
    if has_sink:
        m_row = jnp.maximum(m_row, sink_ref[0:1, 0:1] * LOG2E)
    mb_ref[...] = jnp.broadcast_to(m_row, (tq, LANES))


def _attention(q3, ks, vs, *, n_heads, group, dq, dv, q_col0, k_col0, v_col0, q_scale, sink=None, layer=0,
               tq=1024, tk=512):
    bsz, lq, _ = q3.shape
    tq = min(tq, lq)
    assert dv == LANES
    n_src = len(ks)
    tks = [min(tk, k.shape[1]) for k in ks]
    nq = lq // tq
    n_blocks = bsz * n_heads * nq

    def cur(t):
        t = jnp.minimum(t, n_blocks - 1)
        return t // (n_heads * nq), (t // nq) % n_heads, t % nq

    def prev(t):
        return cur(jnp.maximum(t - 1, 0))

    def at(fn, spec):
        def index_map(t):
            b, h, i = fn(t)
            return spec(b, h, i)
        return index_map

    in_specs = [pl.BlockSpec((None, tq, dq), at(cur, lambda b, h, i: (b, i, q_col0 + h)))]
    in_specs += [pl.BlockSpec((None, k.shape[1], dq), at(cur, lambda b, h, i: (b, 0, k_col0 + h // group)))
                 for k in ks]
    in_specs += [pl.BlockSpec((None, v.shape[1], dv), at(prev, lambda b, h, i: (b, 0, v_col0 + h // group)))
                 for v in vs]
    args = [q3] + list(ks) + list(vs)
    if sink is not None:
        for fn in (cur, prev):
            in_specs.append(pl.BlockSpec((None, None, SUBLANES, LANES),
                                         at(fn, lambda b, h, i: (layer, h, 0, 0))))
            args.append(sink)
    scratch = [pltpu.VMEM((k.shape[1] // tkk, tq, tkk), F32) for k, tkk in zip(ks, tks)]
    scratch += [pltpu.VMEM((v.shape[1], 2 * dv), BF16) for v in vs]
    scratch += [pltpu.VMEM((tq, LANES), F32)]
    return pl.pallas_call(
        functools.partial(_attn_kernel, n_src=n_src, q_scale=q_scale, has_sink=sink is not None, nq=nq),
        grid=(n_blocks + 1,),
        in_specs=in_specs,
        out_specs=pl.BlockSpec((None, tq, dv), at(prev, lambda b, h, i: (b, i, h))),
        out_shape=jax.ShapeDtypeStruct((bsz, lq, n_heads * dv), BF16),
        scratch_shapes=scratch,
        name="attention",
        compiler_params=_cparams("arbitrary"),
    )(*args)


def _outproj_kernel(yr_ref, ys_ref, ym_ref, x_ref, mod_ref, w_ref, g_ref, b_ref, o_ref, *, alpha):
    a, b = RET_DIM, RET_DIM + SWA_HEADS * HEAD_DIM
    mix = (_dot(yr_ref[...], w_ref[0:a, :]) + _dot(ys_ref[...], w_ref[a:b, :])
           + _dot(ym_ref[...], w_ref[b:MIX_WIDTH, :]))
    z = alpha * x_ref[...] + (1.0 + mod_ref[2:3, :]) * mix
    o_ref[...] = _layer_norm(z, g_ref[...], b_ref[...])


def _outproj_ln(y_ret, y_swa, y_mla, x2, mods, w_o, ln_g, ln_b, layer, mod_row0, rows_per_mod, tm, alpha):
    m_rows, d = x2.shape
    mod_idx = lambda i: (layer * MOD_ROWS + mod_row0 + (i * tm) // rows_per_mod, 0, 0)
    lay = lambda i: (layer, 0, 0)
    row = lambda w: pl.BlockSpec((tm, w), lambda i: (i, 0))
    return pl.pallas_call(
        functools.partial(_outproj_kernel, alpha=alpha),
        grid=(m_rows // tm,),
        in_specs=[
            row(RET_DIM), row(SWA_HEADS * HEAD_DIM), row(MLA_HEADS * MLA_V_DIM), row(d),
            pl.BlockSpec((None, N_MOD, d), mod_idx),
            _resident((None, MIX_WIDTH, d), lay),
            pl.BlockSpec((None, 1, d), lay),
            pl.BlockSpec((None, 1, d), lay),
        ],
        out_specs=row(d),
        out_shape=jax.ShapeDtypeStruct((m_rows, d), F32),
        name="outproj_ln",
        compiler_params=_cparams("parallel"),
    )(y_ret, y_swa, y_mla, x2, mods, w_o, ln_g, ln_b)


def _ffn_kernel(x_hbm, xp_ref, xn_ref, mod_ref, wu_ref, wg_ref, cw_ref, cb_ref, wd_ref, g_ref, b_ref,
                o_ref, xbuf, h_ext, g_ext, sem, *, tm, seq, alpha):
    i = pl.program_id(0)
    f = pl.program_id(1)

    def x_copy(tile):
        rows = pl.ds(pl.multiple_of(tile * tm, tm), tm)
        return pltpu.make_async_copy(x_hbm.at[rows, :], xbuf, sem)

    @pl.when(f == 0)
    def _():
        @pl.when(i == 0)
        def _():
            x_copy(i).start()

        x_copy(i).wait()
        scale = 1.0 + mod_ref[4:5, :]
        shift = mod_ref[3:4, :]
        has_prev = jnp.where((i * tm) % seq != 0, 1.0, 0.0).astype(F32)
        has_next = jnp.where(((i + 1) * tm) % seq != 0, 1.0, 0.0).astype(F32)
        h_ext[0:HALO, :] = ((xp_ref[...] * scale + shift) * has_prev).astype(BF16)
        h_ext[HALO:HALO + tm, :] = (xbuf[...] * scale + shift).astype(BF16)
        h_ext[HALO + tm:tm + 2 * HALO, :] = ((xn_ref[...] * scale + shift) * has_next).astype(BF16)
        o_ref[...] = alpha * xbuf[...]

        @pl.when(i + 1 < pl.num_programs(0))
        def _():
            x_copy(i + 1).start()

    g_ext[...] = _dot(h_ext[...], wg_ref[...])
    u = _dot(h_ext[HALO:HALO + tm, :], wu_ref[...])
    g_prev = g_ext[HALO - 1:HALO - 1 + tm, :]
    g_next = g_ext[HALO + 1:HALO + 1 + tm, :]
    if tm > seq:
        r = lax.broadcasted_iota(jnp.int32, (tm, 1), 0) % seq
        g_prev = jnp.where(r != 0, g_prev, 0.0)
        g_next = jnp.where(r != seq - 1, g_next, 0.0)
    gc = (g_prev * cw_ref[0:1, :] + g_ext[HALO:HALO + tm, :] * cw_ref[1:2, :]
          + g_next * cw_ref[2:3, :] + cb_ref[...])
    a = (_silu(gc) * u).astype(BF16)
    o_ref[...] += (1.0 + mod_ref[5:6, :]) * _dot(a, wd_ref[...])

    @pl.when(f == pl.num_programs(1) - 1)
    def _():
        o_ref[...] = _layer_norm(o_ref[...], g_ref[...], b_ref[...])


def _ffn(x2, seq, mods, w_up, conv_w, conv_b, w_down, ln_g, ln_b, layer, mod_row0, rows_per_mod, tm, alpha):
    m_rows, d = x2.shape
    dff = w_down.shape[1]
    tf = 512 if dff % 512 == 0 else (256 if dff % 256 == 0 else LANES)
    nf = dff // tf
    hb = tm // HALO
    nhb = m_rows // HALO
    assert m_rows % tm == 0 and (tm % seq == 0 or seq % tm == 0)
    mod_idx = lambda i, f: (layer * MOD_ROWS + mod_row0 + (i * tm) // rows_per_mod, 0, 0)
    lay = lambda i, f: (layer, 0, 0)
    return pl.pallas_call(
        functools.partial(_ffn_kernel, tm=tm, seq=seq, alpha=alpha),
        grid=(m_rows // tm, nf),
        in_specs=[
            pl.BlockSpec(memory_space=pl.ANY),
            pl.BlockSpec((HALO, d), lambda i, f: (jnp.maximum(i * hb - 1, 0), 0)),
            pl.BlockSpec((HALO, d), lambda i, f: (jnp.minimum((i + 1) * hb, nhb - 1), 0)),
            pl.BlockSpec((None, N_MOD, d), mod_idx),
            pl.BlockSpec((None, d, tf), lambda i, f: (layer, 0, f)),
            pl.BlockSpec((None, d, tf), lambda i, f: (layer, 0, nf + f)),
            pl.BlockSpec((None, 3, tf), lambda i, f: (layer, 0, f)),
            pl.BlockSpec((None, 1, tf), lambda i, f: (layer, 0, f)),
            pl.BlockSpec((None, tf, d), lambda i, f: (layer, f, 0)),
            pl.BlockSpec((None, 1, d), lay),
            pl.BlockSpec((None, 1, d), lay),
        ],
        out_specs=pl.BlockSpec((tm, d), lambda i, f: (i, 0)),
        out_shape=jax.ShapeDtypeStruct((m_rows, d), F32),
        scratch_shapes=[
            pltpu.VMEM((tm, d), F32),
            pltpu.VMEM((tm + 2 * HALO, d), BF16),
            pltpu.VMEM((tm + 2 * HALO, tf), F32),
            pltpu.SemaphoreType.DMA(()),
        ],
        name="conv_ffn",
        compiler_params=_cparams("arbitrary", "arbitrary"),
    )(x2, x2, x2, mods, w_up, w_up, conv_w, conv_b, w_down, ln_g, ln_b)


def _rope_tables(n_tokens, dim):
    rows = n_tokens // GRID_W
    r = np.repeat(np.arange(rows, dtype=np.float32), GRID_W)
    cc = np.tile(np.arange(GRID_W, dtype=np.float32), rows)
    n_freq = dim // 4
    inv = jnp.asarray(ROPE_THETA, F32) ** (-jnp.arange(n_freq, dtype=F32) / n_freq)
    ang_r = jnp.asarray(r)[:, None] * inv
    ang_c = jnp.asarray(cc)[:, None] * inv
    ang = jnp.concatenate([ang_r, ang_r, ang_c, ang_c], axis=-1)
    cos, sin = jnp.cos(ang), jnp.sin(ang)
    lane = np.arange(dim)
    lo = jnp.asarray((lane % (2 * n_freq)) < n_freq)
    sin_lo = jnp.where(lo, -sin, 0.0)
    sin_hi = jnp.where(lo, 0.0, sin)
    pad = LANES - dim
    if pad:
        cos = jnp.pad(cos, ((0, 0), (0, pad)), constant_values=1.0)
        sin_lo = jnp.pad(sin_lo, ((0, 0), (0, pad)))
        sin_hi = jnp.pad(sin_hi, ((0, 0), (0, pad)))
    return cos, sin_lo, sin_hi


def _identity_tables(n_tokens):
    return (jnp.ones((n_tokens, LANES), F32), jnp.zeros((n_tokens, LANES), F32),
            jnp.zeros((n_tokens, LANES), F32))


def _lane_bcast(p):
    return jnp.broadcast_to(p.astype(F32)[..., None, None], p.shape + (SUBLANES, LANES))


def kernel(x, c, ctx, c_ctx, ada_w, ada_b, w_in, ret_decay_fwd, ret_decay_bwd, swa_sink, mla_q_norm, mla_w_uq,
           mla_kv_norm, mla_w_ukv, w_o, ln1_g, ln1_b, ffn_w_up, ffn_conv_w, ffn_conv_b, ffn_w_down, ln2_g, ln2_b):
    bsz, seq, d = x.shape
    t = ctx.shape[1]
    depth = w_in.shape[0]
    assert bsz + 1 <= MOD_ROWS and seq % RET_CHUNK == 0 and t % RET_CHUNK == 0
    assert seq % GRID_W == 0 and d % LANES == 0
    alpha = (2 * depth) ** 0.25

    w_in_b = jnp.pad(w_in.astype(BF16), ((0, 0), (0, 0), (0, IN_WIDTH_PAD - IN_WIDTH)))
    w_o_b = w_o.astype(BF16)
    w_up_b = ffn_w_up.astype(BF16)
    w_down_b = ffn_w_down.astype(BF16)
    hq = MLA_NOPE_DIM + MLA_ROPE_DIM
    wq = mla_w_uq.reshape(depth, MLA_Q_RANK, MLA_HEADS, hq)
    wq = jnp.pad(wq, ((0, 0), (0, 0), (0, 0), (0, MLA_QK_PAD - hq)))
    wq = wq.reshape(depth, MLA_Q_RANK, MLA_HEADS * MLA_QK_PAD).astype(BF16)
    wkv = mla_w_ukv.reshape(depth, MLA_KV_RANK, MLA_HEADS, MLA_NOPE_DIM + MLA_V_DIM)
    wk = wkv[..., :MLA_NOPE_DIM].reshape(depth, MLA_KV_RANK, MLA_HEADS * MLA_NOPE_DIM).astype(BF16)
    wv = wkv[..., MLA_NOPE_DIM:].reshape(depth, MLA_KV_RANK, MLA_HEADS * MLA_V_DIM).astype(BF16)
    qn = mla_q_norm.reshape(depth, 1, MLA_Q_RANK)
    kvn = mla_kv_norm.reshape(depth, 1, MLA_KV_RANK)
    g1, b1 = ln1_g.reshape(depth, 1, d), ln1_b.reshape(depth, 1, d)
    g2, b2 = ln2_g.reshape(depth, 1, d), ln2_b.reshape(depth, 1, d)
    conv_b = ffn_conv_b.reshape(depth, 1, -1)
    dec = _lane_bcast(jnp.stack([ret_decay_fwd, ret_decay_bwd], axis=1))
    sink = _lane_bcast(swa_sink)

    tabs_h = _rope_tables(seq, HEAD_DIM)
    tabs_m = _rope_tables(seq, MLA_ROPE_DIM)
    tabs_id = _identity_tables(t)

    cond = jnp.zeros((MOD_ROWS, d), F32).at[:bsz].set(c).at[bsz].set(c_ctx)
    mods = _ada_mod(cond, ada_w, ada_b).reshape(depth * MOD_ROWS, N_MOD, d)

    tm_x = 512 if seq % 512 == 0 else RET_CHUNK
    tm_f = 1024 if seq % 1024 == 0 else tm_x
    tm_p = 1024 if seq % 1024 == 0 else tm_x
    tm_c = t if t <= 512 else RET_CHUNK
    seqs = max([j for j in range(1, bsz + 1) if bsz % j == 0 and t * j <= 1024], default=0)
    tm_cf = t * seqs if seqs else tm_c
    zeros_state = jnp.zeros((bsz, RET_HEADS, HEAD_DIM, HEAD_DIM), F32)

    x2 = x.reshape(bsz * seq, d)
    xc2 = ctx.reshape(bsz * t, d)
    for l in range(depth):
        last = l == depth - 1
        ret_c, swa_c, mla_c = _inproj(xc2, t, mods, w_in_b, tabs_id, l, bsz, bsz * t, tm_c)
        swa_c3 = swa_c.reshape(bsz, t, SWA_COLS)
        y_ret_c, st_f, st_b = _retention(ret_c.reshape(-1, bsz, t, HEAD_DIM), dec, l, zeros_state, zeros_state)
        q_c, k_c, v_c = _mla_project(mla_c, t, qn, kvn, wq, wk, wv, tabs_id, l, tm_c)
        k_c3 = k_c.reshape(bsz, t, -1)
        v_c3 = v_c.reshape(bsz, t, -1)

        ret_x, swa_x, mla_x = _inproj(x2, seq, mods, w_in_b, tabs_h, l, 0, seq, tm_x)
        y_ret, _, _ = _retention(ret_x.reshape(-1, bsz, seq, HEAD_DIM), dec, l, st_f, st_b)
        y_swa = _swa_latent(swa_x.reshape(bsz, seq, SWA_COLS), swa_c3, sink, l)
        q_x, k_x, v_x = _mla_project(mla_x, seq, qn, kvn, wq, wk, wv, tabs_m, l, tm_p)
        y_mla = _attention(q_x.reshape(bsz, seq, -1), [k_x.reshape(bsz, seq, -1), k_c3],
                           [v_x.reshape(bsz, seq, -1), v_c3], n_heads=MLA_HEADS, group=1,
                           dq=MLA_QK_PAD, dv=MLA_V_DIM, q_col0=0, k_col0=0, v_col0=0, q_scale=1.0)
        x_a = _outproj_ln(y_ret.reshape(bsz * seq, -1), y_swa.reshape(bsz * seq, -1),
                          y_mla.reshape(bsz * seq, -1), x2, mods, w_o_b, g1, b1, l, 0, seq, tm_x, alpha)
        x_new = _ffn(x_a, seq, mods, w_up_b, ffn_conv_w, conv_b, w_down_b, g2, b2, l, 0, seq, tm_f, alpha)

        if not last:
            y_swa_c = _attention(swa_c3, [swa_c3], [swa_c3], n_heads=SWA_HEADS, group=SWA_GROUP,
                                 dq=HEAD_DIM, dv=HEAD_DIM, q_col0=0, k_col0=SWA_HEADS,
                                 v_col0=SWA_HEADS + SWA_KV_HEADS, q_scale=1.0, sink=sink, layer=l)
            y_mla_c = _attention(q_c.reshape(bsz, t, -1), [k_c3], [v_c3], n_heads=MLA_HEADS, group=1,
                                 dq=MLA_QK_PAD, dv=MLA_V_DIM, q_col0=0, k_col0=0, v_col0=0, q_scale=1.0)
            xc_a = _outproj_ln(y_ret_c.reshape(bsz * t, -1), y_swa_c.reshape(bsz * t, -1),
                               y_mla_c.reshape(bsz * t, -1), xc2, mods, w_o_b, g1, b1, l, bsz, bsz * t,
                               tm_c, alpha)
            xc2 = _ffn(xc_a, t, mods, w_up_b, ffn_conv_w, conv_b, w_down_b, g2, b2, l, bsz, bsz * t,
                       tm_cf, alpha)
        x2 = x_new
    return x2.reshape(bsz, seq, d)
```
